```python
import math
import jax
import jax.numpy as jnp
from jax import lax
import numpy as np

D_MODEL = 1024
BATCH = 8
SEQ = 8192
DEPTH = 1
DEC_BATCH = 128
DEC_SEQ = 1
PAST_LEN = 8192
PAGE_SIZE = 128

NSA_HEADS = 8
NSA_KV_HEADS = 2
HEAD_DIM = 64
NSA_GROUP = NSA_HEADS // NSA_KV_HEADS
NSA_WIDTH = NSA_HEADS * HEAD_DIM
KV_WIDTH = NSA_KV_HEADS * HEAD_DIM
CMP_BLOCK = 32
SLC_BLOCK = 64
SLC_TOPN = 16
WINDOW = 512
QUERY_BLOCK = 128
ATTN_SCALE = HEAD_DIM ** -0.5
RET_HEADS = 8
RET_KDIM = 64
RET_VDIM = 64
RET_WIDTH = RET_HEADS * RET_VDIM
RET_CHUNK = 128
ROPE_BASE = 10000.0
REL_BUCKETS = 32
REL_MAX_DIST = 1024
EPS = 1e-6
MIX_WIDTH = NSA_WIDTH + RET_WIDTH
SPLIT_WIDTHS = (NSA_WIDTH, KV_WIDTH, KV_WIDTH, KV_WIDTH, KV_WIDTH, KV_WIDTH, KV_WIDTH,
                3 * NSA_HEADS, NSA_WIDTH, RET_HEADS * RET_KDIM, RET_HEADS * RET_KDIM, RET_WIDTH, RET_WIDTH)
IN_WIDTH = sum(SPLIT_WIDTHS)
SPLIT_POINTS = tuple(int(v) for v in np.cumsum(SPLIT_WIDTHS)[:-1])

kernel_name = 'hybrid_nsa_retention_step'


def rms_norm(x, g):
    xf = x.astype(jnp.float32)
    y = xf * lax.rsqrt(jnp.mean(xf * xf, axis=-1, keepdims=True) + EPS)
    return (y * g.astype(jnp.float32)).astype(x.dtype)


def head_group_norm(o, g):
    b, t = o.shape[:2]
    of = o.astype(jnp.float32)
    mu = jnp.mean(of, axis=-1, keepdims=True)
    var = jnp.mean(jnp.square(of - mu), axis=-1, keepdims=True)
    y = ((of - mu) * lax.rsqrt(var + EPS)).reshape(b, t, -1)
    return (y * g.astype(jnp.float32)).astype(o.dtype)


def rotary(x, pos):
    half = x.shape[-1] // 2
    inv = ROPE_BASE ** (-jnp.arange(half, dtype=jnp.float32) / half)
    ang = pos.astype(jnp.float32)[:, None] * inv[None, :]
    cos = jnp.cos(ang)[None, :, None, :]
    sin = jnp.sin(ang)[None, :, None, :]
    xf = x.astype(jnp.float32)
    x1, x2 = xf[..., :half], xf[..., half:]
    return jnp.concatenate([x1 * cos - x2 * sin, x1 * sin + x2 * cos], axis=-1).astype(x.dtype)


def t5_bucket(dist):
    n = jnp.maximum(dist, 0)
    exact = REL_BUCKETS // 2
    log_ratio = jnp.log(jnp.maximum(n, 1).astype(jnp.float32) / exact) / math.log(REL_MAX_DIST / exact)
    large = jnp.minimum(exact + (log_ratio * (REL_BUCKETS - exact)).astype(jnp.int32), REL_BUCKETS - 1)
    return jnp.where(n < exact, n, large)


def masked_softmax(logits, mask):
    logits = jnp.where(mask, logits, -jnp.inf)
    m = jnp.max(logits, axis=-2, keepdims=True)
    m = jnp.where(jnp.isfinite(m), m, 0.0)
    e = jnp.exp(logits - m)
    s = jnp.sum(e, axis=-2, keepdims=True)
    return e / jnp.where(s > 0.0, s, 1.0)


def attend_shared(q, k, v, dist, valid, rel_bias):
    b, nq = q.shape[:2]
    nk = k.shape[1]
    qg = q.reshape(b, nq, NSA_KV_HEADS, NSA_GROUP, HEAD_DIM)
    bias = rel_bias[t5_bucket(dist)].reshape(nq, nk, NSA_KV_HEADS, NSA_GROUP).transpose(0, 2, 1, 3)
    logits = jnp.einsum('bqhgd,bkhd->bqhkg', qg, k).astype(jnp.float32) * ATTN_SCALE + bias.astype(jnp.float32)[None]
    p = masked_softmax(logits, valid[None, :, None, :, None])
    o = jnp.einsum('bqhkg,bkhd->bqhgd', p.astype(v.dtype), v)
    return o.reshape(b, nq, NSA_HEADS, HEAD_DIM), p


def attend_gathered(q, k, v, dist, valid, rel_bias):
    b, nq = q.shape[:2]
    qg = q.reshape(b, nq, NSA_KV_HEADS, NSA_GROUP, HEAD_DIM)
    table = rel_bias.reshape(REL_BUCKETS, NSA_KV_HEADS, NSA_GROUP).transpose(1, 0, 2)
    h_idx = jnp.arange(NSA_KV_HEADS)[None, None, :, None]
    bias = table[h_idx, t5_bucket(dist)]
    logits = jnp.einsum('bqhgd,bqhrd->bqhrg', qg, k).astype(jnp.float32) * ATTN_SCALE + bias.astype(jnp.float32)
    p = masked_softmax(logits, valid[..., None])
    o = jnp.einsum('bqhrg,bqhrd->bqhgd', p.astype(v.dtype), v)
    return o.reshape(b, nq, NSA_HEADS, HEAD_DIM), p


def compress(rows, pe, w1, w2):
    b, l, h, d = rows.shape
    nc = l // CMP_BLOCK
    blocks = rows.reshape(b, nc, CMP_BLOCK, h, d) + pe[None, None, :, None, :]
    flat = blocks.transpose(0, 1, 3, 2, 4).reshape(b, nc, h, CMP_BLOCK * d)
    return jax.nn.silu(flat @ w1) @ w2


def select_blocks(p_cmp, qpos, n_slc):
    b, nq, h, nc, _ = p_cmp.shape
    ratio = SLC_BLOCK // CMP_BLOCK
    imp = jnp.sum(p_cmp, axis=-1)
    imp = jnp.pad(imp, ((0, 0), (0, 0), (0, 0), (0, n_slc * ratio - nc))).reshape(b, nq, h, n_slc, ratio).sum(-1)
    j = jnp.arange(n_slc)[None, :]
    cur = (qpos // SLC_BLOCK)[:, None]
    valid = j <= cur
    forced = (j == 0) | (j == cur) | (j == cur - 1)
    score = jnp.where(valid[None, :, None, :], jnp.where(forced[None, :, None, :], jnp.inf, imp), -jnp.inf)
    top, idx = lax.top_k(score, min(SLC_TOPN, n_slc))
    return idx, top > -jnp.inf


def nsa_combine(gates, o_c, o_s, o_w):
    return gates[..., 0:1] * o_c + gates[..., 1:2] * o_s + gates[..., 2:3] * o_w


def project(x, pos, norm_g, w_in, q_norm_g, k_norm_g):
    b, t = x.shape[:2]
    parts = jnp.split(rms_norm(x, norm_g) @ w_in, SPLIT_POINTS, axis=-1)
    kv = lambda a: a.reshape(b, t, NSA_KV_HEADS, HEAD_DIM)
    q = rms_norm(parts[0].reshape(b, t, NSA_HEADS, HEAD_DIM), q_norm_g)
    k_cmp, v_cmp = kv(parts[1]), kv(parts[2])
    k_slc, v_slc = rms_norm(kv(parts[3]), k_norm_g), kv(parts[4])
    k_win, v_win = rms_norm(kv(parts[5]), k_norm_g), kv(parts[6])
    gates = jax.nn.sigmoid(parts[7].reshape(b, t, NSA_HEADS, 3))
    q_r = rotary(parts[9].reshape(b, t, RET_HEADS, RET_KDIM), pos)
    k_r = rotary(parts[10].reshape(b, t, RET_HEADS, RET_KDIM), pos) * (RET_KDIM ** -0.5)
    v_r = parts[11].reshape(b, t, RET_HEADS, RET_VDIM)
    return (q, k_cmp, v_cmp, k_slc, v_slc, k_win, v_win, gates, parts[8], q_r, k_r, v_r, parts[12])


def nsa_prompt(q, k_cmp, v_cmp, k_slc, v_slc, k_win, v_win, gates, k_norm_g,
               cmp_pe_k, cmp_w1_k, cmp_w2_k, cmp_pe_v, cmp_w1_v, cmp_w2_v, rel_bias):
    b, t = q.shape[:2]
    ck = rms_norm(compress(k_cmp, cmp_pe_k, cmp_w1_k, cmp_w2_k), k_norm_g)
    cv = compress(v_cmp, cmp_pe_v, cmp_w1_v, cmp_w2_v)
    cend = jnp.arange(ck.shape[1]) * CMP_BLOCK + (CMP_BLOCK - 1)
    n_slc = t // SLC_BLOCK

    def to_blocks(a):
        return a.reshape(b, n_slc, SLC_BLOCK, NSA_KV_HEADS, HEAD_DIM).transpose(0, 3, 1, 2, 4)

    kb, vb = to_blocks(k_slc), to_blocks(v_slc)
    pad = ((0, 0), (WINDOW, 0), (0, 0), (0, 0))
    kw_pad, vw_pad = jnp.pad(k_win, pad), jnp.pad(v_win, pad)
    b_idx = jnp.arange(b)[:, None, None, None]
    h_idx = jnp.arange(NSA_KV_HEADS)[None, None, :, None]
    offs = jnp.arange(SLC_BLOCK)

    def block(qb):
        t0 = qb * QUERY_BLOCK
        qpos = t0 + jnp.arange(QUERY_BLOCK)
        qblk = lax.dynamic_slice_in_dim(q, t0, QUERY_BLOCK, axis=1)
        gblk = lax.dynamic_slice_in_dim(gates, t0, QUERY_BLOCK, axis=1)
        o_c, p_c = attend_shared(qblk, ck, cv, qpos[:, None] - cend[None, :], cend[None, :] <= qpos[:, None], rel_bias)
        idx, ok = select_blocks(p_c, qpos, n_slc)
        kpos = idx[..., None] * SLC_BLOCK + offs
        valid = (ok[..., None] & (kpos <= qpos[None, :, None, None, None])).reshape(b, QUERY_BLOCK, NSA_KV_HEADS, -1)
        kpos = kpos.reshape(b, QUERY_BLOCK, NSA_KV_HEADS, -1)
        kg = kb[b_idx, h_idx, idx].reshape(b, QUERY_BLOCK, NSA_KV_HEADS, -1, HEAD_DIM)
        vg = vb[b_idx, h_idx, idx].reshape(b, QUERY_BLOCK, NSA_KV_HEADS, -1, HEAD_DIM)
        o_s, _ = attend_gathered(qblk, kg, vg, qpos[None, :, None, None] - kpos, valid, rel_bias)
        wpos = t0 - WINDOW + jnp.arange(WINDOW + QUERY_BLOCK)
        kw = lax.dynamic_slice_in_dim(kw_pad, t0, WINDOW + QUERY_BLOCK, axis=1)
        vw = lax.dynamic_slice_in_dim(vw_pad, t0, WINDOW + QUERY_BLOCK, axis=1)
        dist = qpos[:, None] - wpos[None, :]
        wvalid = (wpos[None, :] >= 0) & (dist >= 0) & (dist < WINDOW)
        o_w, _ = attend_shared(qblk, kw, vw, dist, wvalid, rel_bias)
        return nsa_combine(gblk, o_c, o_s, o_w)

    out = lax.map(block, jnp.arange(t // QUERY_BLOCK))
    return out.transpose(1, 0, 2, 3, 4).reshape(b, t, NSA_HEADS, HEAD_DIM)


def nsa_sample(q, k_cmp, v_cmp, k_slc, v_slc, k_win, v_win, gates,
               cache_cmp_k, cache_cmp_v, cache_slc_k, cache_slc_v, cache_win_k, cache_win_v,
               page_table, k_norm_g, cmp_pe_k, cmp_w1_k, cmp_w2_k, cmp_pe_v, cmp_w1_v, cmp_w2_v, rel_bias):
    b, s = q.shape[:2]
    n_pages = page_table.shape[1]
    page = cache_cmp_k.shape[1]
    past = n_pages * page
    total = past + s
    qpos = past + jnp.arange(s)

    def past_rows(pool):
        return pool[page_table].reshape(b, past, NSA_KV_HEADS, HEAD_DIM)

    nc = total // CMP_BLOCK
    kc_all = jnp.concatenate([past_rows(cache_cmp_k), k_cmp], axis=1)[:, :nc * CMP_BLOCK]
    vc_all = jnp.concatenate([past_rows(cache_cmp_v), v_cmp], axis=1)[:, :nc * CMP_BLOCK]
    ck = rms_norm(compress(kc_all, cmp_pe_k, cmp_w1_k, cmp_w2_k), k_norm_g)
    cv = compress(vc_all, cmp_pe_v, cmp_w1_v, cmp_w2_v)
    cend = jnp.arange(nc) * CMP_BLOCK + (CMP_BLOCK - 1)
    o_c, p_c = attend_shared(q, ck, cv, qpos[:, None] - cend[None, :], cend[None, :] <= qpos[:, None], rel_bias)
    n_slc = -(-total // SLC_BLOCK)
    idx, ok = select_blocks(p_c, qpos, n_slc)
    kpos = idx[..., None] * SLC_BLOCK + jnp.arange(SLC_BLOCK)
    valid = (ok[..., None] & (kpos <= qpos[None, :, None, None, None])).reshape(b, s, NSA_KV_HEADS, -1)
    kpos = kpos.reshape(b, s, NSA_KV_HEADS, -1)
    b_idx = jnp.arange(b)[:, None, None, None]
    h_idx = jnp.arange(NSA_KV_HEADS)[None, None, :, None]
    in_past = (kpos < past)[..., None]
    phys = page_table[b_idx, jnp.minimum(kpos // page, n_pages - 1)] * page + kpos % page
    new_i = jnp.clip(kpos - past, 0, s - 1)

    def gather_rows(pool, new):
        flat = pool.reshape(-1, NSA_KV_HEADS, HEAD_DIM)
        return jnp.where(in_past, flat[phys, h_idx], new[b_idx, new_i, h_idx])

    o_s, _ = attend_gathered(q, gather_rows(cache_slc_k, k_slc), gather_rows(cache_slc_v, v_slc),
                             qpos[None, :, None, None] - kpos, valid, rel_bias)
    wb = cache_win_k.shape[1]
    kw_all = jnp.concatenate([cache_win_k, k_win], axis=1)
    vw_all = jnp.concatenate([cache_win_v, v_win], axis=1)
    wpos = past - wb + jnp.arange(wb + s)
    dist = qpos[:, None] - wpos[None, :]
    wvalid = (dist >= 0) & (dist < WINDOW)
    o_w, _ = attend_shared(q, kw_all, vw_all, dist, wvalid, rel_bias)
    keep = min(WINDOW, wb + s)
    return nsa_combine(gates, o_c, o_s, o_w), kw_all[:, -keep:], vw_all[:, -keep:]


def retention_chunk(state, q, k, v):
    c = q.shape[1]
    log_g = jnp.log1p(-jnp.exp2(-5.0 - jnp.arange(RET_HEADS, dtype=jnp.float32)))
    i = jnp.arange(c, dtype=jnp.float32)
    diff = i[:, None] - i[None, :]
    decay = jnp.where(diff >= 0, jnp.exp(jnp.maximum(diff, 0.0)[None] * log_g[:, None, None]), 0.0).astype(q.dtype)
    scores = jnp.einsum('bihd,bjhd->bhij', q, k) * decay[None]
    o = jnp.einsum('bhij,bjhv->bihv', scores, v)
    q_decay = jnp.exp((i[:, None] + 1.0) * log_g[None, :]).astype(q.dtype)
    o = o + jnp.einsum('bihd,bhdv->bihv', q, state) * q_decay[None, :, :, None]
    k_decay = jnp.exp((c - 1.0 - i)[:, None] * log_g[None, :]).astype(k.dtype)
    new_state = state * jnp.exp(c * log_g).astype(state.dtype)[None, :, None, None] \
        + jnp.einsum('bjhd,bjhv->bhdv', k * k_decay[None, :, :, None], v)
    return o, new_state


def retention_prompt(q, k, v):
    b, t = q.shape[:2]
    n = t // RET_CHUNK

    def to_chunks(a):
        return a.reshape(b, n, RET_CHUNK, RET_HEADS, a.shape[-1]).transpose(1, 0, 2, 3, 4)

    def step(state, chunk):
        o, new_state = retention_chunk(state, chunk[0], chunk[1], chunk[2])
        return new_state, o

    state0 = jnp.zeros((b, RET_HEADS, RET_KDIM, RET_VDIM), q.dtype)
    final, o = lax.scan(step, state0, (to_chunks(q), to_chunks(k), to_chunks(v)))
    return o.transpose(1, 0, 2, 3, 4).reshape(b, t, RET_HEADS, RET_VDIM), final


def mix_output(x, o_nsa, z_nsa, o_ret, z_ret, ret_gn_g, w_out):
    b, t = x.shape[:2]
    y_nsa = o_nsa.reshape(b, t, NSA_WIDTH) * jax.nn.silu(z_nsa)
    y_ret = head_group_norm(o_ret, ret_gn_g) * jax.nn.silu(z_ret)
    return x + jnp.concatenate([y_nsa, y_ret], axis=-1) @ w_out


def setup_inputs(seed: int = 0) -> dict:
    key = jax.random.key(seed)
    ks = jax.random.split(key, 24)
    n_pages = PAST_LEN // PAGE_SIZE
    n_phys = (DEC_BATCH * n_pages * 5) // 4
    win_buf = min(WINDOW, PAST_LEN)
    nrm = lambda k, shape, scale: jax.random.normal(k, shape, jnp.float32) * scale
    pool = (n_phys, PAGE_SIZE, NSA_KV_HEADS, HEAD_DIM)
    flat_in = CMP_BLOCK * HEAD_DIM
    return {
        'x_prompt': nrm(ks[0], (BATCH, SEQ, D_MODEL), 1.0),
        'x_sample': nrm(ks[1], (DEC_BATCH, DEC_SEQ, D_MODEL), 1.0),
        'cache_cmp_k': nrm(ks[2], pool, 1.0),
        'cache_cmp_v': nrm(ks[3], pool, 1.0),
        'cache_slc_k': nrm(ks[4], pool, 1.0),
        'cache_slc_v': nrm(ks[5], pool, 1.0),
        'cache_win_k': nrm(ks[6], (DEC_BATCH, win_buf, NSA_KV_HEADS, HEAD_DIM), 1.0),
        'cache_win_v': nrm(ks[7], (DEC_BATCH, win_buf, NSA_KV_HEADS, HEAD_DIM), 1.0),
        'state_ret': nrm(ks[8], (DEC_BATCH, RET_HEADS, RET_KDIM, RET_VDIM), 1.0),
        'page_table': jax.random.permutation(ks[9], n_phys)[:DEC_BATCH * n_pages].reshape(DEC_BATCH, n_pages).astype(jnp.int32),
        'norm_g': 1.0 + nrm(ks[10], (D_MODEL,), 0.02),
        'w_in': nrm(ks[11], (D_MODEL, IN_WIDTH), D_MODEL ** -0.5),
        'q_norm_g': 1.0 + nrm(ks[12], (HEAD_DIM,), 0.02),
        'k_norm_g': 1.0 + nrm(ks[13], (HEAD_DIM,), 0.02),
        'cmp_pe_k': nrm(ks[14], (CMP_BLOCK, HEAD_DIM), 0.1),
        'cmp_w1_k': nrm(ks[15], (flat_in, HEAD_DIM), flat_in ** -0.5),
        'cmp_w2_k': nrm(ks[16], (HEAD_DIM, HEAD_DIM), HEAD_DIM ** -0.5),
        'cmp_pe_v': nrm(ks[17], (CMP_BLOCK, HEAD_DIM), 0.1),
        'cmp_w1_v': nrm(ks[18], (flat_in, HEAD_DIM), flat_in ** -0.5),
        'cmp_w2_v': nrm(ks[19], (HEAD_DIM, HEAD_DIM), HEAD_DIM ** -0.5),
        'rel_bias': nrm(ks[20], (REL_BUCKETS, NSA_HEADS), 0.5),
        'ret_gn_g': 1.0 + nrm(ks[21], (RET_WIDTH,), 0.02),
        'w_out': nrm(ks[22], (MIX_WIDTH, D_MODEL), MIX_WIDTH ** -0.5),
    }


def reference(x_prompt, x_sample, cache_cmp_k, cache_cmp_v, cache_slc_k, cache_slc_v,
              cache_win_k, cache_win_v, state_ret, page_table, norm_g, w_in, q_norm_g, k_norm_g,
              cmp_pe_k, cmp_w1_k, cmp_w2_k, cmp_pe_v, cmp_w1_v, cmp_w2_v, rel_bias, ret_gn_g, w_out):
    past = page_table.shape[1] * cache_cmp_k.shape[1]
    y_prompt, y_sample = x_prompt, x_sample
    for _ in range(DEPTH):
        (q, kc, vc, ksl, vsl, kw, vw, g, zn, qr, kr, vr, zr) = project(
            y_prompt, jnp.arange(y_prompt.shape[1]), norm_g, w_in, q_norm_g, k_norm_g)
        o_nsa = nsa_prompt(q, kc, vc, ksl, vsl, kw, vw, g, k_norm_g,
                           cmp_pe_k, cmp_w1_k, cmp_w2_k, cmp_pe_v, cmp_w1_v, cmp_w2_v, rel_bias)
        o_ret, p_ret = retention_prompt(qr, kr, vr)
        keep = min(WINDOW, kw.shape[1])
        p_cmp_k, p_cmp_v, p_slc_k, p_slc_v = kc, vc, ksl, vsl
        p_win_k, p_win_v = kw[:, -keep:], vw[:, -keep:]
        y_prompt = mix_output(y_prompt, o_nsa, zn, o_ret, zr, ret_gn_g, w_out)
        (q, kc, vc, ksl, vsl, kw, vw, g, zn, qr, kr, vr, zr) = project(
            y_sample, past + jnp.arange(y_sample.shape[1]), norm_g, w_in, q_norm_g, k_norm_g)
        o_nsa, s_win_k, s_win_v = nsa_sample(
            q, kc, vc, ksl, vsl, kw, vw, g, cache_cmp_k, cache_cmp_v, cache_slc_k, cache_slc_v,
            cache_win_k, cache_win_v, page_table, k_norm_g,
            cmp_pe_k, cmp_w1_k, cmp_w2_k, cmp_pe_v, cmp_w1_v, cmp_w2_v, rel_bias)
        o_ret, s_ret = retention_chunk(state_ret, qr, kr, vr)
        s_cmp_k, s_cmp_v, s_slc_k, s_slc_v = kc, vc, ksl, vsl
        y_sample = mix_output(y_sample, o_nsa, zn, o_ret, zr, ret_gn_g, w_out)
    return (y_prompt, y_sample, p_cmp_k, p_cmp_v, p_slc_k, p_slc_v, p_win_k, p_win_v, p_ret,
            s_cmp_k, s_cmp_v, s_slc_k, s_slc_v, s_win_k, s_win_v, s_ret)
```

```python
import functools

import numpy as np
import jax
import jax.numpy as jnp
from jax import lax
from jax.experimental import pallas as pl
from jax.experimental.pallas import tpu as pltpu

F32, BF16 = jnp.float32, jnp.bfloat16

NSA_HEADS = 8
NSA_KV_HEADS = 2
HEAD_DIM = 64
NSA_GROUP = NSA_HEADS // NSA_KV_HEADS
NSA_WIDTH = NSA_HEADS * HEAD_DIM
KV_WIDTH = NSA_KV_HEADS * HEAD_DIM
CMP_BLOCK = 32
SLC_BLOCK = 64
SLC_TOPN = 16
WINDOW = 512
ATTN_SCALE = HEAD_DIM ** -0.5
RET_HEADS = 8
RET_KDIM = 64
RET_VDIM = 64
RET_WIDTH = RET_HEADS * RET_VDIM
ROPE_BASE = 10000.0
REL_BUCKETS = 32
REL_MAX_DIST = 1024
EPS = 1e-6
SPLIT_WIDTHS = (NSA_WIDTH, KV_WIDTH, KV_WIDTH, KV_WIDTH, KV_WIDTH, KV_WIDTH, KV_WIDTH,
                3 * NSA_HEADS, NSA_WIDTH, RET_HEADS * RET_KDIM, RET_HEADS * RET_KDIM, RET_WIDTH, RET_WIDTH)

LANES = 128
VMEM_LIMIT = 56 * 1024 * 1024

PROJ_ROWS = 256
Q_TILE = 128
RET_TILE = 128
RET_SAMPLE_SEQS = 16

MASK_VALUE = -float(2 ** 30)

COL_Q = 0
COL_KC = COL_Q + NSA_WIDTH
COL_VC = COL_KC + LANES
COL_KS = COL_VC + LANES
COL_VS = COL_KS + LANES
COL_KW = COL_VS + LANES
COL_VW = COL_KW + LANES
COL_G = COL_VW + LANES
COL_ZN = COL_G + LANES
COL_QR = COL_ZN + NSA_WIDTH
COL_KR = COL_QR + RET_WIDTH
COL_VR = COL_KR + RET_WIDTH
COL_ZR = COL_VR + RET_WIDTH
COL_END = COL_ZR + RET_WIDTH


def _bucket_lower_bounds():
    exact = REL_BUCKETS // 2
    ratio = REL_MAX_DIST // exact
    lows = list(range(exact))
    n = exact
    for k in range(REL_BUCKETS - exact):
        while n ** (REL_BUCKETS - exact) < exact ** (REL_BUCKETS - exact) * ratio ** k:
            n += 1
        lows.append(n)
    return tuple(lows)


BUCKET_LOW = _bucket_lower_bounds()
FAR_DIST = BUCKET_LOW[-1]
RET_GAMMA_LOG = tuple(float(np.log1p(-np.exp2(-5.0 - h))) for h in range(RET_HEADS))


def _dot(a, b):
    return jnp.dot(a, b, preferred_element_type=F32)


def _dot_nt(a, b):
    return lax.dot_general(a, b, (((1,), (1,)), ((), ())), preferred_element_type=F32)


def _segment_sum(v, seg):
    hi = v.astype(BF16)
    lo = (v - hi.astype(F32)).astype(BF16)
    return _dot(hi, seg) + _dot(lo, seg)


def _head_rms(y, g, seg):
    ms = _segment_sum(y * y, seg) * (1.0 / HEAD_DIM)
    return y * lax.rsqrt(ms + EPS) * g


def _silu(x):
    return x * jax.nn.sigmoid(x)


def _rel_bias_rows(dist, rel_ref, heads):
    out = [jnp.full(dist.shape, rel_ref[0, hd], F32) for hd in heads]
    for bkt in range(1, REL_BUCKETS):
        hit = dist >= BUCKET_LOW[bkt]
        out = [jnp.where(hit, rel_ref[bkt, hd], o) for hd, o in zip(heads, out)]
    return [o - rel_ref[REL_BUCKETS - 1, hd] for hd, o in zip(heads, out)]


def _softmax_step(carry, s, v):
    m, l, acc = carry
    m_new = jnp.maximum(m, jnp.max(s, axis=1, keepdims=True))
    alpha = jnp.exp(m - m_new)
    p = jnp.exp(s - m_new)
    l = alpha * l + jnp.sum(p, axis=1, keepdims=True)
    acc = alpha * acc + _dot(p.astype(BF16), v)
    return m_new, l, acc


def _top_blocks(score, n_pick):
    col = lax.broadcasted_iota(jnp.int32, score.shape, 1).astype(F32)
    sel = jnp.zeros(score.shape, F32)
    neg_inf = -jnp.inf
    for _ in range(n_pick):
        m = jnp.max(score, axis=1, keepdims=True)
        first = jnp.min(jnp.where(score == m, col, float(score.shape[1])), axis=1, keepdims=True)
        hit = col == first
        sel = jnp.where(hit & (m > neg_inf), 1.0, sel)
        score = jnp.where(hit, neg_inf, score)
    return sel


def _proj_kernel(x_ref, cos_ref, sin_ref, ng_ref, w_ref, qg_ref, kg_ref, seg_ref,
                 q_o, kc_o, vc_o, ks_o, vs_o, kw_o, vw_o, ksb_o, vsb_o, kwb_o, vwb_o,
                 g_o, zn_o, qr_o, kr_o, vr_o, zr_o):
    x = x_ref[0]
    inv = lax.rsqrt(jnp.mean(x * x, axis=-1, keepdims=True) + EPS)
    xn = (x * inv * ng_ref[...]).astype(BF16)
    seg = seg_ref[...]

    def mm(col, width):
        return _dot(xn, w_ref[:, col:col + width])

    for c in range(NSA_WIDTH // LANES):
        y = mm(COL_Q + LANES * c, LANES)
        q_o[0, :, LANES * c:LANES * (c + 1)] = _head_rms(y, qg_ref[...], seg).astype(BF16)
    kc_o[0] = mm(COL_KC, LANES)
    vc_o[0] = mm(COL_VC, LANES)
    ks = _head_rms(mm(COL_KS, LANES), kg_ref[...], seg)
    ks_o[0] = ks
    ksb_o[0] = ks.astype(BF16)
    vs = mm(COL_VS, LANES)
    vs_o[0] = vs
    vsb_o[0] = vs.astype(BF16)
    kw = _head_rms(mm(COL_KW, LANES), kg_ref[...], seg)
    kw_o[0] = kw
    kwb_o[0] = kw.astype(BF16)
    vw = mm(COL_VW, LANES)
    vw_o[0] = vw
    vwb_o[0] = vw.astype(BF16)
    g_o[0] = jax.nn.sigmoid(mm(COL_G, LANES))
    zn_o[0] = mm(COL_ZN, NSA_WIDTH)

    cos = cos_ref[...]
    sin = sin_ref[...]
    lane = lax.broadcasted_iota(jnp.int32, cos.shape, 1)
    first_half = (lane % HEAD_DIM) < (HEAD_DIM // 2)
    for c in range(RET_WIDTH // LANES):
        for col, o_ref, scale in ((COL_QR, qr_o, None), (COL_KR, kr_o, RET_KDIM ** -0.5)):
            y = mm(col + LANES * c, LANES)
            partner = jnp.where(first_half, pltpu.roll(y, LANES - HEAD_DIM // 2, 1),
                                pltpu.roll(y, HEAD_DIM // 2, 1))
            r = y * cos + partner * sin
            if scale is not None:
                r = r * scale
            o_ref[0, :, LANES * c:LANES * (c + 1)] = r.astype(BF16)
    vr_o[0] = mm(COL_VR, RET_WIDTH).astype(BF16)
    zr_o[0] = mm(COL_ZR, RET_WIDTH)


def _project(x, cos, sin, norm_g, w, qg, kg, seg):
    nb, rows, d = x.shape
    tm = min(PROJ_ROWS, rows)
    assert rows % tm == 0
    row_spec = lambda width: pl.BlockSpec((1, tm, width), lambda i, j: (i, j, 0))
    const = lambda shape: pl.BlockSpec(shape, lambda i, j: (0,) * len(shape))
    outs = [(NSA_WIDTH, BF16)] + [(LANES, F32)] * 6 + [(LANES, BF16)] * 4 + [(LANES, F32), (NSA_WIDTH, F32),
            (RET_WIDTH, BF16), (RET_WIDTH, BF16), (RET_WIDTH, BF16), (RET_WIDTH, F32)]
    return pl.pallas_call(
        _proj_kernel,
        grid=(nb, rows // tm),
        in_specs=[row_spec(d), pl.BlockSpec((tm, LANES), lambda i, j: (j, 0)),
                  pl.BlockSpec((tm, LANES), lambda i, j: (j, 0)), const((1, d)), const(w.shape),
                  const((1, LANES)), const((1, LANES)), const((LANES, LANES))],
        out_specs=[row_spec(wd) for wd, _ in outs],
        out_shape=[jax.ShapeDtypeStruct((nb, rows, wd), dt) for wd, dt in outs],
        compiler_params=pltpu.CompilerParams(dimension_semantics=("parallel", "parallel"),
                                             vmem_limit_bytes=VMEM_LIMIT),
        name="input_projection",
    )(x, cos, sin, norm_g, w, qg, kg, seg)


def _compress_rows(buf_ref, pe_ref, w1_ref, w2_ref, n_blk):
    parts = []
    for i in range(CMP_BLOCK):
        rows = buf_ref[pl.ds(i, n_blk, stride=CMP_BLOCK), :] + pe_ref[i:i + 1, :]
        parts.append(rows.astype(BF16))
    flat = jnp.concatenate(parts, axis=1)
    hidden = _silu(_dot(flat, w1_ref[...]))
    return _dot(hidden.astype(BF16), w2_ref[...])


def _compress_kernel(kc_ref, vc_ref, pek_ref, w1k_ref, w2k_ref, pev_ref, w1v_ref, w2v_ref, kg_ref, seg_ref,
                     ck_o, cv_o):
    n_blk = ck_o.shape[1]
    ck = _compress_rows(kc_ref.at[0], pek_ref, w1k_ref, w2k_ref, n_blk)
    ck_o[0] = _head_rms(ck, kg_ref[...], seg_ref[...]).astype(BF16)
    cv_o[0] = _compress_rows(vc_ref.at[0], pev_ref, w1v_ref, w2v_ref, n_blk).astype(BF16)


def _compress_prompt(kc, vc, cw, kg, seg):
    b, t, _ = kc.shape
    n_blk = t // CMP_BLOCK
    seq = pl.BlockSpec((1, t, LANES), lambda i: (i, 0, 0))
    const = lambda a: pl.BlockSpec(a.shape, lambda i: (0,) * a.ndim)
    out = pl.BlockSpec((1, n_blk, LANES), lambda i: (i, 0, 0))
    return pl.pallas_call(
        _compress_kernel,
        grid=(b,),
        in_specs=[seq, seq] + [const(a) for a in cw] + [const(kg), const(seg)],
        out_specs=[out, out],
        out_shape=[jax.ShapeDtypeStruct((b, n_blk, LANES), BF16)] * 2,
        compiler_params=pltpu.CompilerParams(dimension_semantics=("parallel",), vmem_limit_bytes=VMEM_LIMIT),
        name="compress_prompt",
    )(kc, vc, *cw, kg, seg)


def _compress_sample_kernel(pt_ref, pk_ref, pv_ref, pek_ref, w1k_ref, w2k_ref, pev_ref, w1v_ref, w2v_ref,
                            kg_ref, seg_ref, ck_o, cv_o, kbuf, vbuf):
    del pt_ref
    page = pl.program_id(1)
    rows = pk_ref.shape[1]
    r0 = pl.multiple_of(page * rows, rows)
    kbuf[pl.ds(r0, rows), :] = pk_ref[0]
    vbuf[pl.ds(r0, rows), :] = pv_ref[0]

    @pl.when(page == pl.num_programs(1) - 1)
    def _():
        n_blk = ck_o.shape[1]
        ck = _compress_rows(kbuf, pek_ref, w1k_ref, w2k_ref, n_blk)
        ck_o[0] = _head_rms(ck, kg_ref[...], seg_ref[...]).astype(BF16)
        cv_o[0] = _compress_rows(vbuf, pev_ref, w1v_ref, w2v_ref, n_blk).astype(BF16)


def _compress_sample(page_table, pool_k, pool_v, cw, kg, seg):
    n_seq, n_pages = page_table.shape
    page_rows = pool_k.shape[1]
    past = n_pages * page_rows
    n_blk = past // CMP_BLOCK
    page_spec = pl.BlockSpec((1, page_rows, LANES), lambda i, p, pt: (pt[i * n_pages + p], 0, 0))
    const = lambda a: pl.BlockSpec(a.shape, lambda i, p, pt: (0,) * a.ndim)
    out = pl.BlockSpec((1, n_blk, LANES), lambda i, p, pt: (i, 0, 0))
    return pl.pallas_call(
        _compress_sample_kernel,
        grid_spec=pltpu.PrefetchScalarGridSpec(
            num_scalar_prefetch=1,
            grid=(n_seq, n_pages),
            in_specs=[page_spec, page_spec] + [const(a) for a in cw] + [const(kg), const(seg)],
            out_specs=[out, out],
            scratch_shapes=[pltpu.VMEM((past, LANES), F32), pltpu.VMEM((past, LANES), F32)]),
        out_shape=[jax.ShapeDtypeStruct((n_seq, n_blk, LANES), BF16)] * 2,
        compiler_params=pltpu.CompilerParams(dimension_semantics=("arbitrary", "arbitrary"),
                                             vmem_limit_bytes=VMEM_LIMIT),
        name="compress_sample",
    )(page_table.reshape(-1), pool_k, pool_v, *cw, kg, seg)


def _nsa_prompt_kernel(rel_ref, q_ref, ck_ref, cv_ref, ks_ref, vs_ref, kw_ref, vw_ref, e_ref, g_ref, z_ref,
                       y_ref, biasc_ref, band_ref):
    tq = q_ref.shape[1]
    n_cmp = ck_ref.shape[1]
    n_slc = n_cmp // (SLC_BLOCK // CMP_BLOCK)
    rows = NSA_GROUP * tq
    near = -(-(FAR_DIST + tq - 1) // tq)
    win_tiles = WINDOW // tq
    qb = pl.program_id(0)
    b = pl.program_id(1)
    t0 = qb * tq
    all_heads = tuple(range(NSA_HEADS))

    lane = lax.broadcasted_iota(jnp.int32, (tq, LANES), 1)
    row = lax.broadcasted_iota(jnp.int32, (tq, LANES), 0)

    @pl.when((qb == 0) & (b == 0))
    def _():
        for c in range(near):
            dist = row - lane + tq * (near - 1 - c)
            for hd, bias in zip(all_heads, _rel_bias_rows(dist, rel_ref, all_heads)):
                band_ref[c, hd] = bias

    ccol = lax.broadcasted_iota(jnp.int32, (tq, n_cmp), 1)
    crow = lax.broadcasted_iota(jnp.int32, (tq, n_cmp), 0)
    cblock = (SLC_BLOCK // CMP_BLOCK) * (ccol % n_slc) + ccol // n_slc
    cdist = t0 + crow - (CMP_BLOCK * cblock + CMP_BLOCK - 1)

    @pl.when(b == 0)
    def _():
        for hd, bias in zip(all_heads, _rel_bias_rows(cdist, rel_ref, all_heads)):
            biasc_ref[hd] = bias

    q_all = q_ref[0]
    gates = g_ref[0]
    ck = ck_ref[0]
    cv = cv_ref[0]
    neg_inf = -jnp.inf
    carry0 = (jnp.full((rows, 1), neg_inf, F32), jnp.zeros((rows, 1), F32), jnp.zeros((rows, LANES), F32))

    for h in range(NSA_KV_HEADS):
        own_half = (lane // HEAD_DIM) == h
        q_pad = jnp.concatenate(
            [jnp.where(own_half, q_all[:, LANES * g:LANES * (g + 1)], jnp.zeros((), BF16)) for g in range(NSA_GROUP)],
            axis=0)

        s_c = _dot_nt(q_pad, ck).reshape(NSA_GROUP, tq, n_cmp) + biasc_ref[NSA_GROUP * h:NSA_GROUP * (h + 1)]
        s_c = jnp.where((cdist >= 0)[None], s_c, neg_inf)
        m_c = jnp.max(s_c, axis=-1, keepdims=True)
        m_c = jnp.where(m_c == neg_inf, 0.0, m_c)
        e_c = jnp.exp(s_c - m_c)
        sum_c = jnp.sum(e_c, axis=-1, keepdims=True)
        p_c = e_c / jnp.where(sum_c > 0.0, sum_c, 1.0)
        o_c = _dot(p_c.reshape(rows, n_cmp).astype(BF16), cv)

        imp = p_c[0]
        for g in range(1, NSA_GROUP):
            imp = imp + p_c[g]
        imp = imp[:, :n_slc] + imp[:, n_slc:]
        if n_slc < LANES:
            imp = jnp.concatenate([imp, jnp.zeros((tq, LANES - n_slc), F32)], axis=1)
        cur = (t0 + row) // SLC_BLOCK
        forced = (lane == 0) | (lane == cur) | (lane == cur - 1)
        score = jnp.where(lane <= cur, jnp.where(forced, jnp.inf, imp), neg_inf)
        sel = _top_blocks(score, SLC_TOPN)
        neg_mask = (sel - 1.0).astype(BF16)
        q_aug = jnp.concatenate([q_pad, jnp.concatenate([neg_mask] * NSA_GROUP, axis=0)], axis=1)

        def slc_tile(kt, carry, is_near):
            k0 = pl.multiple_of(kt * tq, tq)
            k_aug = jnp.concatenate([ks_ref[0, pl.ds(k0, tq), :], e_ref[pl.ds(k0, tq), :]], axis=1)
            s = _dot_nt(q_aug, k_aug)
            if is_near:
                s = s.reshape(NSA_GROUP, tq, tq) + band_ref[kt - qb + near - 1, NSA_GROUP * h:NSA_GROUP * (h + 1)]
                causal = (lane <= row) | (kt < qb)
                s = jnp.where(causal[None], s, MASK_VALUE).reshape(rows, tq)
            return _softmax_step(carry, s, vs_ref[0, pl.ds(k0, tq), :])

        n_far = jnp.maximum(qb - (near - 1), 0)
        carry = lax.fori_loop(0, n_far, functools.partial(slc_tile, is_near=False), carry0)
        _, l_s, acc_s = lax.fori_loop(n_far, qb + 1, functools.partial(slc_tile, is_near=True), carry)
        o_s = acc_s / l_s

        carry = carry0
        for c in range(win_tiles + 1):
            kt = qb - win_tiles + c
            k0 = pl.multiple_of(jnp.maximum(kt, 0) * tq, tq)
            s = _dot_nt(q_pad, kw_ref[0, pl.ds(k0, tq), :]).reshape(NSA_GROUP, tq, tq)
            s = s + band_ref[near - 1 - win_tiles + c, NSA_GROUP * h:NSA_GROUP * (h + 1)]
            ok = kt >= 0
            if c == 0:
                ok = ok & (lane > row)
            elif c == win_tiles:
                ok = ok & (lane <= row)
            else:
                ok = jnp.broadcast_to(ok, (tq, LANES))
            s = jnp.where(ok[None], s, MASK_VALUE).reshape(rows, tq)
            carry = _softmax_step(carry, s, vw_ref[0, pl.ds(k0, tq), :])
        _, l_w, acc_w = carry
        o_w = acc_w / l_w

        mixed = []
        for g in range(NSA_GROUP):
            hd = NSA_GROUP * h + g
            rs = slice(g * tq, (g + 1) * tq)
            mixed.append(gates[:, hd:hd + 1] * o_c[rs]
                         + gates[:, NSA_HEADS + hd:NSA_HEADS + hd + 1] * o_s[rs]
                         + gates[:, 2 * NSA_HEADS + hd:2 * NSA_HEADS + hd + 1] * o_w[rs])
        for pair in range(NSA_GROUP // 2):
            a, c = mixed[2 * pair], mixed[2 * pair + 1]
            if h == 0:
                both = jnp.where(lane < HEAD_DIM, a, pltpu.roll(c, HEAD_DIM, 1))
            else:
                both = jnp.where(lane < HEAD_DIM, pltpu.roll(a, HEAD_DIM, 1), c)
            col = (NSA_GROUP * h + 2 * pair) * HEAD_DIM
            y_ref[0, :, col:col + LANES] = (both * _silu(z_ref[0, :, col:col + LANES])).astype(BF16)


def _nsa_prompt(rel_bias, q, ck, cv, ks, vs, kw, vw, e_mat, gates, z):
    b, t, _ = q.shape
    tq = Q_TILE
    n_cmp = ck.shape[1]
    assert t % tq == 0 and WINDOW % tq == 0 and t // SLC_BLOCK <= LANES
    near = -(-(FAR_DIST + tq - 1) // tq)
    assert near - 1 >= WINDOW // tq
    tile = lambda width: pl.BlockSpec((1, tq, width), lambda i, j: (j, i, 0))
    seq = lambda rows: pl.BlockSpec((1, rows, LANES), lambda i, j: (j, 0, 0))
    return pl.pallas_call(
        _nsa_prompt_kernel,
        grid=(t // tq, b),
        in_specs=[pl.BlockSpec(memory_space=pltpu.SMEM), tile(NSA_WIDTH), seq(n_cmp), seq(n_cmp),
                  seq(t), seq(t), seq(t), seq(t), pl.BlockSpec((t, LANES), lambda i, j: (0, 0)),
                  tile(LANES), tile(NSA_WIDTH)],
        out_specs=tile(NSA_WIDTH),
        out_shape=jax.ShapeDtypeStruct((b, t, NSA_WIDTH), BF16),
        scratch_shapes=[pltpu.VMEM((NSA_HEADS, tq, n_cmp), F32),
                        pltpu.VMEM((near, NSA_HEADS, tq, tq), F32)],
        compiler_params=pltpu.CompilerParams(dimension_semantics=("arbitrary", "arbitrary"),
                                             vmem_limit_bytes=VMEM_LIMIT),
        name="nsa_prompt",
    )(rel_bias, q, ck, cv, ks, vs, kw, vw, e_mat, gates, z)


def _nsa_sample_kernel(pt_ref, rel_ref, q_ref, ck_ref, cv_ref, pk_ref, pv_ref, e_ref, kwc_ref, vwc_ref, new_ref,
                       g_ref, z_ref, y_ref, kbuf, vbuf, biasc_ref, biass_ref, biasw_ref):
    del pt_ref
    seq = pl.program_id(0)
    page = pl.program_id(1)
    page_rows = pk_ref.shape[1]
    past = kbuf.shape[0]
    n_cmp = ck_ref.shape[1]
    n_slc = n_cmp // (SLC_BLOCK // CMP_BLOCK)
    win_rows = kwc_ref.shape[1]
    heads = tuple(range(NSA_HEADS))

    r0 = pl.multiple_of(page * page_rows, page_rows)
    kbuf[pl.ds(r0, page_rows), :] = pk_ref[0]
    vbuf[pl.ds(r0, page_rows), :] = pv_ref[0]

    @pl.when((seq == 0) & (page == 0))
    def _():
        ccol = lax.broadcasted_iota(jnp.int32, (1, n_cmp), 1)
        cblock = (SLC_BLOCK // CMP_BLOCK) * (ccol % n_slc) + ccol // n_slc
        cdist = past - (CMP_BLOCK * cblock + CMP_BLOCK - 1)
        biasc_ref[...] = jnp.concatenate(_rel_bias_rows(cdist, rel_ref, heads), axis=0)
        sdist = past - lax.broadcasted_iota(jnp.int32, (1, past), 1)
        biass_ref[...] = jnp.concatenate(_rel_bias_rows(sdist, rel_ref, heads), axis=0)
        wdist = win_rows - lax.broadcasted_iota(jnp.int32, (1, win_rows), 1)
        biasw_ref[...] = jnp.concatenate(_rel_bias_rows(wdist, rel_ref, heads), axis=0)

    @pl.when(page == pl.num_programs(1) - 1)
    def _():
        q8 = q_ref[0]
        q8f = q8.astype(F32)
        bias0 = jnp.concatenate(
            [jnp.full((1, 1), rel_ref[0, hd] - rel_ref[REL_BUCKETS - 1, hd], F32) for hd in heads], axis=0)
        lane = lax.broadcasted_iota(jnp.int32, (NSA_KV_HEADS, LANES), 1)

        s_c = _dot_nt(q8, ck_ref[0]) + biasc_ref[...]
        e_c = jnp.exp(s_c - jnp.max(s_c, axis=1, keepdims=True))
        p_c = e_c / jnp.sum(e_c, axis=1, keepdims=True)
        o_c = _dot(p_c.astype(BF16), cv_ref[0])

        imps = []
        for h in range(NSA_KV_HEADS):
            acc = p_c[NSA_GROUP * h:NSA_GROUP * h + 1]
            for g in range(1, NSA_GROUP):
                acc = acc + p_c[NSA_GROUP * h + g:NSA_GROUP * h + g + 1]
            imps.append(acc)
        imp = jnp.concatenate(imps, axis=0)
        imp = imp[:, :n_slc] + imp[:, n_slc:]
        if n_slc < LANES:
            imp = jnp.concatenate([imp, jnp.zeros((NSA_KV_HEADS, LANES - n_slc), F32)], axis=1)
        forced = (lane == 0) | (lane == n_slc - 1)
        score = jnp.where(lane < n_slc, jnp.where(forced, jnp.inf, imp), -jnp.inf)
        sel = _top_blocks(score, SLC_TOPN - 1)
        neg_mask = (sel - 1.0).astype(BF16)
        neg_rows = jnp.concatenate([neg_mask[h:h + 1] for h in range(NSA_KV_HEADS) for _ in range(NSA_GROUP)], axis=0)
        q_aug = jnp.concatenate([q8, neg_rows], axis=1)

        def with_new_token(s, s_new, v, v_new):
            m = jnp.maximum(jnp.max(s, axis=1, keepdims=True), s_new)
            p = jnp.exp(s - m)
            p_new = jnp.exp(s_new - m)
            total = jnp.sum(p, axis=1, keepdims=True) + p_new
            return (_dot(p.astype(BF16), v) + p_new.astype(BF16).astype(F32) * v_new.astype(F32)) / total

        k_aug = jnp.concatenate([kbuf[...].astype(BF16), e_ref[...]], axis=1)
        s_s = _dot_nt(q_aug, k_aug) + biass_ref[...]
        s_new = jnp.sum(q8f * new_ref[0, 0:1, :].astype(F32), axis=1, keepdims=True) + bias0
        o_s = with_new_token(s_s, s_new, vbuf[...].astype(BF16), new_ref[0, 1:2, :])

        wcol = lax.broadcasted_iota(jnp.int32, (1, win_rows), 1)
        s_w = _dot_nt(q8, kwc_ref[0].astype(BF16)) + biasw_ref[...]
        s_w = jnp.where(win_rows - wcol < WINDOW, s_w, MASK_VALUE)
        s_new = jnp.sum(q8f * new_ref[0, 2:3, :].astype(F32), axis=1, keepdims=True) + bias0
        o_w = with_new_token(s_w, s_new, vwc_ref[0].astype(BF16), new_ref[0, 3:4, :])

        gates = g_ref[0]
        low = []
        for hd in heads:
            r = (gates[:, hd:hd + 1] * o_c[hd:hd + 1]
                 + gates[:, NSA_HEADS + hd:NSA_HEADS + hd + 1] * o_s[hd:hd + 1]
                 + gates[:, 2 * NSA_HEADS + hd:2 * NSA_HEADS + hd + 1] * o_w[hd:hd + 1])
            low.append(r if hd < NSA_GROUP else pltpu.roll(r, HEAD_DIM, 1))
        lane1 = lax.broadcasted_iota(jnp.int32, (1, LANES), 1)
        for pair in range(NSA_HEADS // 2):
            both = jnp.where(lane1 < HEAD_DIM, low[2 * pair], pltpu.roll(low[2 * pair + 1], HEAD_DIM, 1))
            cols = slice(LANES * pair, LANES * (pair + 1))
            y_ref[0, :, cols] = (both * _silu(z_ref[0, :, cols])).astype(BF16)


def _nsa_sample(page_table, rel_bias, q8, ck, cv, pool_k, pool_v, e_mat, win_k, win_v, new_rows, gates, z):
    n_seq, n_pages = page_table.shape
    page_rows = pool_k.shape[1]
    past = n_pages * page_rows
    n_cmp = ck.shape[1]
    win_rows = win_k.shape[1]
    assert past % SLC_BLOCK == 0 and past // SLC_BLOCK <= LANES
    page_spec = pl.BlockSpec((1, page_rows, LANES), lambda i, p, pt: (pt[i * n_pages + p], 0, 0))
    per_seq = lambda r, w: pl.BlockSpec((1, r, w), lambda i, p, pt: (i, 0, 0))
    return pl.pallas_call(
        _nsa_sample_kernel,
        grid_spec=pltpu.PrefetchScalarGridSpec(
            num_scalar_prefetch=1,
            grid=(n_seq, n_pages),
            in_specs=[pl.BlockSpec(memory_space=pltpu.SMEM), per_seq(NSA_HEADS, LANES), per_seq(n_cmp, LANES),
                      per_seq(n_cmp, LANES), page_spec, page_spec,
                      pl.BlockSpec((past, LANES), lambda i, p, pt: (0, 0)),
                      per_seq(win_rows, LANES), per_seq(win_rows, LANES), per_seq(4, LANES),
                      per_seq(1, LANES), per_seq(1, NSA_WIDTH)],
            out_specs=per_seq(1, NSA_WIDTH),
            scratch_shapes=[pltpu.VMEM((past, LANES), F32), pltpu.VMEM((past, LANES), F32),
                            pltpu.VMEM((NSA_HEADS, n_cmp), F32), pltpu.VMEM((NSA_HEADS, past), F32),
                            pltpu.VMEM((NSA_HEADS, win_rows), F32)]),
        out_shape=jax.ShapeDtypeStruct((n_seq, 1, NSA_WIDTH), BF16),
        compiler_params=pltpu.CompilerParams(dimension_semantics=("arbitrary", "arbitrary"),
                                             vmem_limit_bytes=VMEM_LIMIT),
        name="nsa_sample",
    )(page_table.reshape(-1), rel_bias, q8, ck, cv, pool_k, pool_v, e_mat, win_k, win_v, new_rows, gates, z)


def _group_norm_gate(o, z, gn, seg):
    mu = _segment_sum(o, seg) * (1.0 / RET_VDIM)
    d = o - mu
    var = _segment_sum(d * d, seg) * (1.0 / RET_VDIM)
    return (d * lax.rsqrt(var + EPS) * gn * _silu(z)).astype(BF16)


def _retention_prompt_kernel(q_ref, k_ref, v_ref, z_ref, gn_ref, seg_ref, dec_ref, qdec_ref, kdec_ref, cdec_ref,
                             y_ref, st_ref, state):
    chunk = pl.program_id(1)
    tc = q_ref.shape[1]
    lane = lax.broadcasted_iota(jnp.int32, (tc, LANES), 1)
    srow = lax.broadcasted_iota(jnp.int32, (LANES, LANES), 0)
    scol = lax.broadcasted_iota(jnp.int32, (LANES, LANES), 1)
    same_head = (srow // RET_KDIM) == (scol // RET_VDIM)

    @pl.when(chunk == 0)
    def _():
        state[...] = jnp.zeros_like(state)

    for pair in range(RET_HEADS // 2):
        cols = slice(LANES * pair, LANES * (pair + 1))
        q = q_ref[0, :, cols]
        k = k_ref[0, :, cols]
        v = v_ref[0, :, cols]
        s_old = state[pair]
        cross = _dot(q, s_old.astype(BF16)) * qdec_ref[:, cols]
        halves = []
        for e in range(2):
            qe = jnp.where((lane // RET_KDIM) == e, q, jnp.zeros((), BF16))
            scores = _dot_nt(qe, k) * dec_ref[2 * pair + e]
            halves.append(_dot(scores.astype(BF16), v))
        o = jnp.where(lane < RET_VDIM, halves[0], halves[1]) + cross
        y_ref[0, :, cols] = _group_norm_gate(o, z_ref[0, :, cols], gn_ref[:, cols], seg_ref[...])
        kd_t = (k.astype(F32) * kdec_ref[:, cols]).T.astype(BF16)
        s_new = s_old * cdec_ref[:, cols] + jnp.where(same_head, _dot(kd_t, v), 0.0)
        state[pair] = s_new

        @pl.when(chunk == pl.num_programs(1) - 1)
        def _():
            for e in range(2):
                st_ref[0, 2 * pair + e] = s_new[RET_KDIM * e:RET_KDIM * (e + 1), RET_VDIM * e:RET_VDIM * (e + 1)]


def _retention_tables(tc):
    log_g = jnp.asarray(RET_GAMMA_LOG, F32)
    i = jnp.arange(tc, dtype=F32)
    diff = i[:, None] - i[None, :]
    decay = jnp.where(diff >= 0, jnp.exp(jnp.maximum(diff, 0.0)[None] * log_g[:, None, None]), 0.0)
    widen = lambda a: jnp.repeat(a, RET_VDIM, axis=-1)
    q_decay = widen(jnp.exp((i[:, None] + 1.0) * log_g[None, :]))
    k_decay = widen(jnp.exp((tc - 1.0 - i)[:, None] * log_g[None, :]))
    c_decay = widen(jnp.exp(tc * log_g)[None, :])
    return decay, q_decay, k_decay, c_decay


def _retention_prompt(q, k, v, z, gn, seg):
    b, t, _ = q.shape
    tc = RET_TILE
    assert t % tc == 0
    decay, q_decay, k_decay, c_decay = _retention_tables(tc)
    tile = pl.BlockSpec((1, tc, RET_WIDTH), lambda i, j: (i, j, 0))
    const = lambda a: pl.BlockSpec(a.shape, lambda i, j: (0,) * a.ndim)
    return pl.pallas_call(
        _retention_prompt_kernel,
        grid=(b, t // tc),
        in_specs=[tile, tile, tile, tile, const(gn), const(seg), const(decay), const(q_decay), const(k_decay),
                  const(c_decay)],
        out_specs=[tile, pl.BlockSpec((1, RET_HEADS, RET_KDIM, RET_VDIM), lambda i, j: (i, 0, 0, 0))],
        out_shape=[jax.ShapeDtypeStruct((b, t, RET_WIDTH), BF16),
                   jax.ShapeDtypeStruct((b, RET_HEADS, RET_KDIM, RET_VDIM), F32)],
        scratch_shapes=[pltpu.VMEM((RET_HEADS // 2, LANES, LANES), F32)],
        compiler_params=pltpu.CompilerParams(dimension_semantics=("arbitrary", "arbitrary"),
                                             vmem_limit_bytes=VMEM_LIMIT),
        name="retention_prompt",
    )(q, k, v, z, gn, seg, decay, q_decay, k_decay, c_decay)


def _retention_sample_kernel(qt_ref, kt_ref, v_ref, z_ref, gn_ref, gam_ref, st_ref, y_ref, so_ref):
    n = v_ref.shape[1]
    qt = qt_ref[0]
    kt = kt_ref[0]
    gam = gam_ref[...]
    qk = jnp.sum(qt * kt, axis=1)
    for s in range(n):
        st = st_ref[s]
        vs = v_ref[0, s]
        cross = jnp.sum(qt[:, :, s:s + 1] * st, axis=1)
        so_ref[s] = st * gam[:, :, None] + kt[:, :, s:s + 1] * vs[:, None, :]
        o = qk[:, s:s + 1] * vs + gam * cross
        mu = jnp.mean(o, axis=-1, keepdims=True)
        d = o - mu
        var = jnp.mean(d * d, axis=-1, keepdims=True)
        y_ref[0, s] = (d * lax.rsqrt(var + EPS) * gn_ref[...] * _silu(z_ref[0, s])).astype(BF16)


def _retention_sample(q, k, v, z, gn, state):
    n_seq = q.shape[0]
    n = min(RET_SAMPLE_SEQS, n_seq)
    assert n_seq % n == 0
    steps = n_seq // n
    to_cols = lambda a: a.astype(F32).reshape(steps, n, RET_HEADS, RET_KDIM).transpose(0, 2, 3, 1)
    to_rows = lambda a: a.astype(F32).reshape(steps, n, RET_HEADS, RET_VDIM)
    gam = jnp.exp(jnp.asarray(RET_GAMMA_LOG, F32)).reshape(RET_HEADS, 1)
    cols = pl.BlockSpec((1, RET_HEADS, RET_KDIM, n), lambda i: (i, 0, 0, 0))
    rws = pl.BlockSpec((1, n, RET_HEADS, RET_VDIM), lambda i: (i, 0, 0, 0))
    st = pl.BlockSpec((n, RET_HEADS, RET_KDIM, RET_VDIM), lambda i: (i, 0, 0, 0))
    y, new_state = pl.pallas_call(
        _retention_sample_kernel,
        grid=(steps,),
        in_specs=[cols, cols, rws, rws, pl.BlockSpec((RET_HEADS, RET_VDIM), lambda i: (0, 0)),
                  pl.BlockSpec((RET_HEADS, 1), lambda i: (0, 0)), st],
        out_specs=[rws, st],
        out_shape=[jax.ShapeDtypeStruct((steps, n, RET_HEADS, RET_VDIM), BF16),
                   jax.ShapeDtypeStruct(state.shape, F32)],
        compiler_params=pltpu.CompilerParams(dimension_semantics=("parallel",), vmem_limit_bytes=VMEM_LIMIT),
        name="retention_sample",
    )(to_cols(q), to_cols(k), to_rows(v), to_rows(z), gn.reshape(RET_HEADS, RET_VDIM), gam, state)
    return y.reshape(n_seq, RET_WIDTH), new_state


def _out_kernel(x_ref, yn_ref, yr_ref, w_ref, o_ref):
    y = jnp.concatenate([yn_ref[0], yr_ref[0]], axis=1)
    o_ref[0] = x_ref[0] + _dot(y, w_ref[...])


def _out_project(x, y_nsa, y_ret, w):
    nb, rows, d = x.shape
    tm = min(PROJ_ROWS, rows)
    spec = lambda width: pl.BlockSpec((1, tm, width), lambda i, j: (i, j, 0))
    return pl.pallas_call(
        _out_kernel,
        grid=(nb, rows // tm),
        in_specs=[spec(d), spec(NSA_WIDTH), spec(RET_WIDTH), pl.BlockSpec(w.shape, lambda i, j: (0, 0))],
        out_specs=spec(d),
        out_shape=jax.ShapeDtypeStruct(x.shape, F32),
        compiler_params=pltpu.CompilerParams(dimension_semantics=("parallel", "parallel"),
                                             vmem_limit_bytes=VMEM_LIMIT),
        name="output_projection",
    )(x, y_nsa, y_ret, w)


def _rotary_tables(pos):
    half = RET_KDIM // 2
    inv = ROPE_BASE ** (-jnp.arange(half, dtype=F32) / half)
    ang = pos.astype(F32)[:, None] * inv[None, :]
    cos, sin = jnp.cos(ang), jnp.sin(ang)
    reps = LANES // RET_KDIM
    return jnp.tile(jnp.concatenate([cos, cos], axis=1), (1, reps)), jnp.tile(jnp.concatenate([-sin, sin], axis=1), (1, reps))


def _arrange_w_in(w_in):
    d = w_in.shape[0]
    parts = jnp.split(w_in, np.cumsum(SPLIT_WIDTHS)[:-1].tolist(), axis=1)
    order = [hd for g in range(NSA_GROUP) for hd in (g, NSA_GROUP + g)]
    q = parts[0].reshape(d, NSA_HEADS, HEAD_DIM)[:, order].reshape(d, NSA_WIDTH)
    gates = parts[7].reshape(d, NSA_HEADS, 3).transpose(0, 2, 1).reshape(d, 3 * NSA_HEADS)
    gates = jnp.pad(gates, ((0, 0), (0, LANES - 3 * NSA_HEADS)))
    w = jnp.concatenate([q] + parts[1:7] + [gates] + parts[8:], axis=1)
    assert w.shape[1] == COL_END
    return w.astype(BF16)


def _compress_weights(pe, w1, w2):
    w1r = w1.reshape(CMP_BLOCK, HEAD_DIM, HEAD_DIM)
    w1_both = jnp.zeros((CMP_BLOCK, NSA_KV_HEADS, HEAD_DIM, NSA_KV_HEADS, HEAD_DIM), F32)
    w2_both = jnp.zeros((NSA_KV_HEADS, HEAD_DIM, NSA_KV_HEADS, HEAD_DIM), F32)
    for h in range(NSA_KV_HEADS):
        w1_both = w1_both.at[:, h, :, h, :].set(w1r)
        w2_both = w2_both.at[h, :, h, :].set(w2)
    return (jnp.tile(pe, (1, NSA_KV_HEADS)), w1_both.reshape(CMP_BLOCK * KV_WIDTH, KV_WIDTH).astype(BF16),
            w2_both.reshape(KV_WIDTH, KV_WIDTH).astype(BF16))


def _even_odd(c):
    n, nc, w = c.shape
    ratio = SLC_BLOCK // CMP_BLOCK
    return c.reshape(n, nc // ratio, ratio, w).transpose(0, 2, 1, 3).reshape(n, nc, w)


def _block_membership(n_keys):
    k = jnp.arange(n_keys)[:, None] // SLC_BLOCK
    return jnp.where(k == jnp.arange(LANES)[None, :], -MASK_VALUE, 0.0).astype(BF16)


def kernel(x_prompt, x_sample, cache_cmp_k, cache_cmp_v, cache_slc_k, cache_slc_v, cache_win_k, cache_win_v,
           state_ret, page_table, norm_g, w_in, q_norm_g, k_norm_g, cmp_pe_k, cmp_w1_k, cmp_w2_k, cmp_pe_v,
           cmp_w1_v, cmp_w2_v, rel_bias, ret_gn_g, w_out):
    b, t, d = x_prompt.shape
    n_seq, dec_len, _ = x_sample.shape
    n_pages = page_table.shape[1]
    page_rows = cache_cmp_k.shape[1]
    past = n_pages * page_rows
    assert dec_len == 1 and past % CMP_BLOCK == 0
    kv4 = lambda a: a.reshape(a.shape[0], a.shape[1], NSA_KV_HEADS, HEAD_DIM)

    w = _arrange_w_in(w_in)
    w_o = w_out.astype(BF16)
    ng = norm_g.reshape(1, d)
    qg = (jnp.tile(q_norm_g, LANES // HEAD_DIM) * ATTN_SCALE).reshape(1, LANES)
    kg = jnp.tile(k_norm_g, LANES // HEAD_DIM).reshape(1, LANES)
    gn = ret_gn_g.reshape(1, RET_WIDTH)
    lane = np.arange(LANES)
    seg = jnp.asarray(lane[:, None] // HEAD_DIM == lane[None, :] // HEAD_DIM, BF16)
    cw = _compress_weights(cmp_pe_k, cmp_w1_k, cmp_w2_k) + _compress_weights(cmp_pe_v, cmp_w1_v, cmp_w2_v)

    cos, sin = _rotary_tables(jnp.arange(t))
    (q, kc, vc, ks, vs, kw, vw, ksb, vsb, kwb, vwb, gates, zn, qr, kr, vr, zr) = _project(
        x_prompt, cos, sin, ng, w, qg, kg, seg)
    ck, cv = _compress_prompt(kc, vc, cw, kg, seg)
    y_nsa = _nsa_prompt(rel_bias, q, _even_odd(ck), _even_odd(cv), ksb, vsb, kwb, vwb, _block_membership(t), gates, zn)
    y_ret, p_ret = _retention_prompt(qr, kr, vr, zr, gn, seg)
    y_prompt = _out_project(x_prompt, y_nsa, y_ret, w_o)
    keep = min(WINDOW, t)
    prompt_out = (y_prompt, kv4(kc), kv4(vc), kv4(ks), kv4(vs), kv4(kw[:, t - keep:]), kv4(vw[:, t - keep:]), p_ret)

    cos, sin = _rotary_tables(jnp.full((n_seq,), past))
    xs = x_sample.reshape(1, n_seq, d)
    (q, kc, vc, ks, vs, kw, vw, ksb, vsb, kwb, vwb, gates, zn, qr, kr, vr, zr) = [
        a[0] for a in _project(xs, cos, sin, ng, w, qg, kg, seg)]
    pool = lambda a: a.reshape(a.shape[0], page_rows, KV_WIDTH)
    ck, cv = _compress_sample(page_table, pool(cache_cmp_k), pool(cache_cmp_v), cw, kg, seg)
    half = jnp.asarray(lane[None, :] // HEAD_DIM == (np.arange(NSA_HEADS) // NSA_GROUP)[:, None])
    q8 = jnp.where(half[None], jnp.tile(q.reshape(n_seq, NSA_GROUP, LANES), (1, NSA_KV_HEADS, 1)), jnp.zeros((), BF16))
    new_rows = jnp.stack([ksb, vsb, kwb, vwb], axis=1)
    win = lambda a: a.reshape(a.shape[0], a.shape[1], KV_WIDTH)
    y_nsa = _nsa_sample(page_table, rel_bias, q8, _even_odd(ck), _even_odd(cv), pool(cache_slc_k), pool(cache_slc_v),
                        _block_membership(past), win(cache_win_k), win(cache_win_v), new_rows,
                        gates.reshape(n_seq, 1, LANES), zn.reshape(n_seq, 1, NSA_WIDTH))
    y_ret, s_ret = _retention_sample(qr, kr, vr, zr, gn, state_ret)
    y_sample = _out_project(xs, y_nsa.reshape(1, n_seq, NSA_WIDTH), y_ret.reshape(1, n_seq, RET_WIDTH), w_o)
    keep = min(WINDOW, cache_win_k.shape[1] + 1)
    new4 = lambda a: a.reshape(n_seq, 1, NSA_KV_HEADS, HEAD_DIM)
    s_win_k = jnp.concatenate([cache_win_k, new4(kw)], axis=1)[:, -keep:]
    s_win_v = jnp.concatenate([cache_win_v, new4(vw)], axis=1)[:, -keep:]
    sample_out = (y_sample.reshape(n_seq, 1, d), new4(kc), new4(vc), new4(ks), new4(vs), s_win_k, s_win_v, s_ret)

    return (prompt_out[0], sample_out[0]) + prompt_out[1:] + sample_out[1:]
```

```python
import functools

import numpy as np
import jax
import jax.numpy as jnp
from jax import lax
from jax.experimental import pallas as pl
from jax.experimental.pallas import tpu as pltpu

F32, BF16 = jnp.float32, jnp.bfloat16

NSA_HEADS = 8
NSA_KV_HEADS = 2
HEAD_DIM = 64
NSA_GROUP = NSA_HEADS // NSA_KV_HEADS
NSA_WIDTH = NSA_HEADS * HEAD_DIM
KV_WIDTH = NSA_KV_HEADS * HEAD_DIM
CMP_BLOCK = 32
SLC_BLOCK = 64
SLC_TOPN = 16
WINDOW = 512
ATTN_SCALE = HEAD_DIM ** -0.5
RET_HEADS = 8
RET_KDIM = 64
RET_VDIM = 64
RET_WIDTH = RET_HEADS * RET_VDIM
ROPE_BASE = 10000.0
REL_BUCKETS = 32
REL_MAX_DIST = 1024
EPS = 1e-6
SPLIT_WIDTHS = (NSA_WIDTH, KV_WIDTH, KV_WIDTH, KV_WIDTH, KV_WIDTH, KV_WIDTH, KV_WIDTH,
                3 * NSA_HEADS, NSA_WIDTH, RET_HEADS * RET_KDIM, RET_HEADS * RET_KDIM, RET_WIDTH, RET_WIDTH)

LANES = 128
VMEM_LIMIT = 56 * 1024 * 1024

PROJ_ROWS = 256
Q_TILE = 128
SLC_KEYS = 512
RET_TILE = 128
RET_SAMPLE_SEQS = 16

MASK_VALUE = -float(2 ** 30)

COL_Q = 0
COL_KC = COL_Q + NSA_WIDTH
COL_VC = COL_KC + LANES
COL_KS = COL_VC + LANES
COL_VS = COL_KS + LANES
COL_KW = COL_VS + LANES
COL_VW = COL_KW + LANES
COL_G = COL_VW + LANES
COL_ZN = COL_G + LANES
COL_QR = COL_ZN + NSA_WIDTH
COL_KR = COL_QR + RET_WIDTH
COL_VR = COL_KR + RET_WIDTH
COL_ZR = COL_VR + RET_WIDTH
COL_END = COL_ZR + RET_WIDTH


def _bucket_lower_bounds():
    exact = REL_BUCKETS // 2
    ratio = REL_MAX_DIST // exact
    lows = list(range(exact))
    n = exact
    for k in range(REL_BUCKETS - exact):
        while n ** (REL_BUCKETS - exact) < exact ** (REL_BUCKETS - exact) * ratio ** k:
            n += 1
        lows.append(n)
    return tuple(lows)


BUCKET_LOW = _bucket_lower_bounds()
FAR_DIST = BUCKET_LOW[-1]
RET_GAMMA_LOG = tuple(float(np.log1p(-np.exp2(-5.0 - h))) for h in range(RET_HEADS))


def _dot(a, b):
    return jnp.dot(a, b, preferred_element_type=F32)


def _dot_nt(a, b):
    return lax.dot_general(a, b, (((1,), (1,)), ((), ())), preferred_element_type=F32)


def _segment_sum(v, seg):
    hi = v.astype(BF16)
    lo = (v - hi.astype(F32)).astype(BF16)
    return _dot(hi, seg) + _dot(lo, seg)


def _head_rms(y, g, seg):
    ms = _segment_sum(y * y, seg) * (1.0 / HEAD_DIM)
    return y * lax.rsqrt(ms + EPS) * g


def _silu(x):
    return x * jax.nn.sigmoid(x)


def _rel_bias_rows(dist, rel_ref, heads):
    out = [jnp.full(dist.shape, rel_ref[0, hd], F32) for hd in heads]
    for bkt in range(1, REL_BUCKETS):
        hit = dist >= BUCKET_LOW[bkt]
        out = [jnp.where(hit, rel_ref[bkt, hd], o) for hd, o in zip(heads, out)]
    return [o - rel_ref[REL_BUCKETS - 1, hd] for hd, o in zip(heads, out)]


def _softmax_step(carry, s, v):
    m, acc = carry
    m_new = jnp.maximum(m, jnp.max(s, axis=1, keepdims=True))
    alpha = jnp.exp(m - m_new)
    p = jnp.exp(s - m_new)
    acc = alpha * acc + _dot(p.astype(BF16), v)
    return m_new, acc


def _top_blocks(score, n_pick):
    col = lax.broadcasted_iota(jnp.int32, score.shape, 1).astype(F32)
    sel = jnp.zeros(score.shape, F32)
    neg_inf = -jnp.inf
    for _ in range(n_pick):
        m = jnp.max(score, axis=1, keepdims=True)
        first = jnp.min(jnp.where(score == m, col, float(score.shape[1])), axis=1, keepdims=True)
        hit = col == first
        sel = jnp.where(hit & (m > neg_inf), 1.0, sel)
        score = jnp.where(hit, neg_inf, score)
    return sel


def _top_blocks_cols(score, n_pick):
    n_rows = score.shape[0]
    rowi = lax.broadcasted_iota(jnp.int32, score.shape, 0).astype(F32)
    sel = jnp.zeros(score.shape, F32)
    neg_inf = -jnp.inf
    for _ in range(n_pick):
        m = jnp.max(score, axis=0, keepdims=True)
        first = jnp.min(jnp.where(score == m, rowi, float(n_rows)), axis=0, keepdims=True)
        hit = rowi == first
        sel = sel + jnp.where(hit, jnp.where(m > neg_inf, 1.0, 0.0), 0.0)
        score = jnp.where(hit, neg_inf, score)
    return sel


def _proj_kernel(x_ref, cos_ref, sin_ref, ng_ref, w_ref, qg_ref, kg_ref, seg_ref,
                 q_o, kc_o, vc_o, ks_o, vs_o, kw_o, vw_o, ksb_o, vsb_o, kwb_o, vwb_o,
                 g_o, zn_o, qr_o, kr_o, vr_o, zr_o):
    x = x_ref[0]
    inv = lax.rsqrt(jnp.mean(x * x, axis=-1, keepdims=True) + EPS)
    xn = (x * inv * ng_ref[...]).astype(BF16)
    seg = seg_ref[...]

    def mm(col, width):
        return _dot(xn, w_ref[:, col:col + width])

    for c in range(NSA_WIDTH // LANES):
        y = mm(COL_Q + LANES * c, LANES)
        q_o[0, :, LANES * c:LANES * (c + 1)] = _head_rms(y, qg_ref[...], seg).astype(BF16)
    kc_o[0] = mm(COL_KC, LANES)
    vc_o[0] = mm(COL_VC, LANES)
    ks = _head_rms(mm(COL_KS, LANES), kg_ref[...], seg)
    ks_o[0] = ks
    ksb_o[0] = ks.astype(BF16)
    vs = mm(COL_VS, LANES)
    vs_o[0] = vs
    vsb_o[0] = vs.astype(BF16)
    kw = _head_rms(mm(COL_KW, LANES), kg_ref[...], seg)
    kw_o[0] = kw
    kwb_o[0] = kw.astype(BF16)
    vw = mm(COL_VW, LANES)
    vw_o[0] = vw
    vwb_o[0] = vw.astype(BF16)
    g_o[0] = jax.nn.sigmoid(mm(COL_G, LANES))
    zn_o[0] = mm(COL_ZN, NSA_WIDTH)

    cos = cos_ref[...]
    sin = sin_ref[...]
    lane = lax.broadcasted_iota(jnp.int32, cos.shape, 1)
    first_half = (lane % HEAD_DIM) < (HEAD_DIM // 2)
    for c in range(RET_WIDTH // LANES):
        for col, o_ref, scale in ((COL_QR, qr_o, None), (COL_KR, kr_o, RET_KDIM ** -0.5)):
            y = mm(col + LANES * c, LANES)
            partner = jnp.where(first_half, pltpu.roll(y, LANES - HEAD_DIM // 2, 1),
                                pltpu.roll(y, HEAD_DIM // 2, 1))
            r = y * cos + partner * sin
            if scale is not None:
                r = r * scale
            o_ref[0, :, LANES * c:LANES * (c + 1)] = r.astype(BF16)
    vr_o[0] = mm(COL_VR, RET_WIDTH).astype(BF16)
    zr_o[0] = mm(COL_ZR, RET_WIDTH)


def _project(x, cos, sin, norm_g, w, qg, kg, seg):
    nb, rows, d = x.shape
    tm = min(PROJ_ROWS, rows)
    assert rows % tm == 0
    row_spec = lambda width: pl.BlockSpec((1, tm, width), lambda i, j: (i, j, 0))
    const = lambda shape: pl.BlockSpec(shape, lambda i, j: (0,) * len(shape))
    outs = [(NSA_WIDTH, BF16)] + [(LANES, F32)] * 6 + [(LANES, BF16)] * 4 + [(LANES, F32), (NSA_WIDTH, F32),
            (RET_WIDTH, BF16), (RET_WIDTH, BF16), (RET_WIDTH, BF16), (RET_WIDTH, F32)]
    return pl.pallas_call(
        _proj_kernel,
        grid=(nb, rows // tm),
        in_specs=[row_spec(d), pl.BlockSpec((tm, LANES), lambda i, j: (j, 0)),
                  pl.BlockSpec((tm, LANES), lambda i, j: (j, 0)), const((1, d)), const(w.shape),
                  const((1, LANES)), const((1, LANES)), const((LANES, LANES))],
        out_specs=[row_spec(wd) for wd, _ in outs],
        out_shape=[jax.ShapeDtypeStruct((nb, rows, wd), dt) for wd, dt in outs],
        compiler_params=pltpu.CompilerParams(dimension_semantics=("parallel", "parallel"),
                                             vmem_limit_bytes=VMEM_LIMIT),
        name="input_projection",
    )(x, cos, sin, norm_g, w, qg, kg, seg)


def _compress_rows(buf_ref, pe_ref, w1_ref, w2_ref, n_blk):
    parts = []
    for i in range(CMP_BLOCK):
        rows = buf_ref[pl.ds(i, n_blk, stride=CMP_BLOCK), :] + pe_ref[i:i + 1, :]
        parts.append(rows.astype(BF16))
    flat = jnp.concatenate(parts, axis=1)
    hidden = _silu(_dot(flat, w1_ref[...]))
    return _dot(hidden.astype(BF16), w2_ref[...])


def _compress_kernel(kc_ref, vc_ref, pek_ref, w1k_ref, w2k_ref, pev_ref, w1v_ref, w2v_ref, kg_ref, seg_ref,
                     ck_o, cv_o):
    n_blk = ck_o.shape[1]
    ck = _compress_rows(kc_ref.at[0], pek_ref, w1k_ref, w2k_ref, n_blk)
    ck_o[0] = _head_rms(ck, kg_ref[...], seg_ref[...]).astype(BF16)
    cv_o[0] = _compress_rows(vc_ref.at[0], pev_ref, w1v_ref, w2v_ref, n_blk).astype(BF16)


def _compress_prompt(kc, vc, cw, kg, seg):
    b, t, _ = kc.shape
    n_blk = t // CMP_BLOCK
    seq = pl.BlockSpec((1, t, LANES), lambda i: (i, 0, 0))
    const = lambda a: pl.BlockSpec(a.shape, lambda i: (0,) * a.ndim)
    out = pl.BlockSpec((1, n_blk, LANES), lambda i: (i, 0, 0))
    return pl.pallas_call(
        _compress_kernel,
        grid=(b,),
        in_specs=[seq, seq] + [const(a) for a in cw] + [const(kg), const(seg)],
        out_specs=[out, out],
        out_shape=[jax.ShapeDtypeStruct((b, n_blk, LANES), BF16)] * 2,
        compiler_params=pltpu.CompilerParams(dimension_semantics=("parallel",), vmem_limit_bytes=VMEM_LIMIT),
        name="compress_prompt",
    )(kc, vc, *cw, kg, seg)


def _page_copy(pt_ref, pool_ref, buf_ref, sem, seq, page, n_pages):
    rows = pool_ref.shape[2]
    return pltpu.make_async_copy(pool_ref.at[pt_ref[seq * n_pages + page]],
                                 buf_ref.at[:, pl.ds(page * rows, rows)], sem)


def _gather_pages(pt_ref, pools, bufs, sems, seq, slot, n_pages, wait):
    for pool_ref, buf_ref, sem in zip(pools, bufs, sems):
        for page in range(n_pages):
            copy = _page_copy(pt_ref, pool_ref, buf_ref.at[slot], sem.at[slot], seq, page, n_pages)
            if wait:
                copy.wait()
            else:
                copy.start()


def _double_buffered_gather(pt_ref, pools, bufs, sems, n_pages):
    seq = pl.program_id(0)
    slot = seq % 2

    @pl.when(seq == 0)
    def _():
        _gather_pages(pt_ref, pools, bufs, sems, seq, slot, n_pages, wait=False)

    @pl.when(seq + 1 < pl.num_programs(0))
    def _():
        _gather_pages(pt_ref, pools, bufs, sems, seq + 1, 1 - slot, n_pages, wait=False)

    _gather_pages(pt_ref, pools, bufs, sems, seq, slot, n_pages, wait=True)
    return slot


def _compress_sample_kernel(pt_ref, pk_ref, pv_ref, pek_ref, w1k_ref, w2k_ref, pev_ref, w1v_ref, w2v_ref,
                            kg_ref, seg_ref, ck_o, cv_o, kbuf, vbuf, rows_ref, ksem, vsem):
    page_rows = pk_ref.shape[2]
    n_pages = kbuf.shape[2] // page_rows
    n_blk = ck_o.shape[1]
    slot = _double_buffered_gather(pt_ref, (pk_ref, pv_ref), (kbuf, vbuf), (ksem, vsem), n_pages)

    def token_major(buf):
        for page in range(n_pages):
            cols = slice(page * page_rows, (page + 1) * page_rows)
            rows_ref[cols, :] = buf[slot, :, cols].T

    token_major(kbuf)
    ck = _compress_rows(rows_ref, pek_ref, w1k_ref, w2k_ref, n_blk)
    ck_o[0] = _head_rms(ck, kg_ref[...], seg_ref[...]).astype(BF16)
    token_major(vbuf)
    cv_o[0] = _compress_rows(rows_ref, pev_ref, w1v_ref, w2v_ref, n_blk).astype(BF16)


def _compress_sample(page_table, pool_k, pool_v, cw, kg, seg):
    n_seq, n_pages = page_table.shape
    page_rows = pool_k.shape[2]
    assert page_rows == LANES
    past = n_pages * page_rows
    n_blk = past // CMP_BLOCK
    hbm = pl.BlockSpec(memory_space=pl.ANY)
    const = lambda a: pl.BlockSpec(a.shape, lambda i, pt: (0,) * a.ndim)
    out = pl.BlockSpec((1, n_blk, LANES), lambda i, pt: (i, 0, 0))
    return pl.pallas_call(
        _compress_sample_kernel,
        grid_spec=pltpu.PrefetchScalarGridSpec(
            num_scalar_prefetch=1,
            grid=(n_seq,),
            in_specs=[hbm, hbm] + [const(a) for a in cw] + [const(kg), const(seg)],
            out_specs=[out, out],
            scratch_shapes=[pltpu.VMEM((2, KV_WIDTH, past), F32), pltpu.VMEM((2, KV_WIDTH, past), F32),
                            pltpu.VMEM((past, KV_WIDTH), F32),
                            pltpu.SemaphoreType.DMA((2,)), pltpu.SemaphoreType.DMA((2,))]),
        out_shape=[jax.ShapeDtypeStruct((n_seq, n_blk, LANES), BF16)] * 2,
        compiler_params=pltpu.CompilerParams(dimension_semantics=("arbitrary",), vmem_limit_bytes=VMEM_LIMIT),
        name="compress_sample",
    )(page_table.reshape(-1), pool_k, pool_v, *cw, kg, seg)


def _band_geometry(tq):
    near = -(-(FAR_DIST + tq - 1) // tq)
    rel_max = near + SLC_KEYS // tq - 2
    rel_min = -max(WINDOW // tq, SLC_KEYS // tq - 1)
    return near, rel_max, rel_max - rel_min + 1


def _nsa_prompt_kernel(rel_ref, q_ref, ck_ref, cv_ref, ks_ref, vs_ref, kw_ref, vw_ref, e_ref, g_ref, z_ref,
                       y_ref, biasc_ref, band_ref):
    tq = q_ref.shape[1]
    n_cmp = ck_ref.shape[1]
    n_slc = n_cmp // (SLC_BLOCK // CMP_BLOCK)
    rows = NSA_GROUP * tq
    kt = SLC_KEYS
    chunks = kt // tq
    near, rel_max, n_band = _band_geometry(tq)
    win_tiles = WINDOW // tq
    win_keys = WINDOW + tq
    qb = pl.program_id(0)
    b = pl.program_id(1)
    t0 = qb * tq
    all_heads = tuple(range(NSA_HEADS))

    lane = lax.broadcasted_iota(jnp.int32, (tq, LANES), 1)
    row = lax.broadcasted_iota(jnp.int32, (tq, LANES), 0)

    @pl.when((qb == 0) & (b == 0))
    def _():
        for c in range(n_band):
            dist = row - lane + tq * (rel_max - c)
            for hd, bias in zip(all_heads, _rel_bias_rows(dist, rel_ref, all_heads)):
                band_ref[c, hd] = bias

    def band_bias(first_chunk, n_chunks, heads):
        base = rel_max - qb + first_chunk
        return jnp.concatenate([band_ref[base + u, heads] for u in range(n_chunks)], axis=-1)

    ccol = lax.broadcasted_iota(jnp.int32, (tq, n_cmp), 1)
    crow = lax.broadcasted_iota(jnp.int32, (tq, n_cmp), 0)
    cblock = (SLC_BLOCK // CMP_BLOCK) * (ccol % n_slc) + ccol // n_slc
    cdist = t0 + crow - (CMP_BLOCK * cblock + CMP_BLOCK - 1)

    @pl.when(b == 0)
    def _():
        for hd, bias in zip(all_heads, _rel_bias_rows(cdist, rel_ref, all_heads)):
            biasc_ref[hd] = bias

    q_all = q_ref[0]
    gates = g_ref[0]
    ck = ck_ref[0]
    cv = cv_ref[0]
    neg_inf = -jnp.inf
    carry0 = (jnp.full((rows, 1), neg_inf, F32), jnp.zeros((rows, LANES), F32))
    tile_off = lax.broadcasted_iota(jnp.int32, (tq, kt), 1) - lax.broadcasted_iota(jnp.int32, (tq, kt), 0)
    win_off = (lax.broadcasted_iota(jnp.int32, (tq, win_keys), 1)
               - lax.broadcasted_iota(jnp.int32, (tq, win_keys), 0))
    ones = jnp.ones((), BF16)

    def with_ones(v, h):
        own = (lax.broadcasted_iota(jnp.int32, v.shape, 1) // HEAD_DIM) == h
        return jnp.where(own, v, ones)

    q_pads, q_augs, o_cs = [], [], []
    for h in range(NSA_KV_HEADS):
        own_half = (lane // HEAD_DIM) == h
        q_pad = jnp.concatenate(
            [jnp.where(own_half, q_all[:, LANES * g:LANES * (g + 1)], jnp.zeros((), BF16)) for g in range(NSA_GROUP)],
            axis=0)

        s_c = _dot_nt(q_pad, ck).reshape(NSA_GROUP, tq, n_cmp) + biasc_ref[NSA_GROUP * h:NSA_GROUP * (h + 1)]
        s_c = jnp.where((cdist >= 0)[None], s_c, neg_inf)
        m_c = jnp.max(s_c, axis=-1, keepdims=True)
        m_c = jnp.where(m_c == neg_inf, 0.0, m_c)
        e_c = jnp.exp(s_c - m_c)
        sum_c = jnp.sum(e_c, axis=-1, keepdims=True)
        p_c = e_c / jnp.where(sum_c > 0.0, sum_c, 1.0)
        o_cs.append(_dot(p_c.reshape(rows, n_cmp).astype(BF16), cv))

        imp = p_c[0]
        for g in range(1, NSA_GROUP):
            imp = imp + p_c[g]
        imp = imp[:, :n_slc] + imp[:, n_slc:]
        if n_slc < LANES:
            imp = jnp.concatenate([imp, jnp.zeros((tq, LANES - n_slc), F32)], axis=1)
        cur = (t0 + row) // SLC_BLOCK
        forced = (lane == 0) | (lane == cur) | (lane == cur - 1)
        score = jnp.where(lane <= cur, jnp.where(forced, jnp.inf, imp), neg_inf)
        sel = _top_blocks_cols(score.T, SLC_TOPN).T
        neg_mask = (sel - 1.0).astype(BF16)
        q_pads.append(q_pad)
        q_augs.append(jnp.concatenate([q_pad, jnp.concatenate([neg_mask] * NSA_GROUP, axis=0)], axis=1))

    for h in range(NSA_KV_HEADS):
        heads = slice(NSA_GROUP * h, NSA_GROUP * (h + 1))
        sum_lane = HEAD_DIM * (1 - h)
        q_pad, q_aug, o_c = q_pads[h], q_augs[h], o_cs[h]

        def logits(j):
            k0 = pl.multiple_of(j * kt, kt)
            k_aug = jnp.concatenate([ks_ref[0, pl.ds(k0, kt), :], e_ref[pl.ds(k0, kt), :]], axis=1)
            return _dot_nt(q_aug, k_aug)

        def absorb(j, s, carry, is_near):
            k0 = pl.multiple_of(j * kt, kt)
            if is_near:
                s = s.reshape(NSA_GROUP, tq, kt) + band_bias(j * chunks, chunks, heads)
                causal = tile_off <= t0 - k0
                s = jnp.where(causal[None], s, MASK_VALUE).reshape(rows, kt)
            return _softmax_step(carry, s, with_ones(vs_ref[0, pl.ds(k0, kt), :], h))

        def slc_tile(j, carry, is_near):
            return absorb(j, logits(j), carry, is_near)

        n_far = jnp.maximum(qb - (near - 1), 0) // chunks
        carry = lax.fori_loop(0, n_far, functools.partial(slc_tile, is_near=False), carry0)
        _, acc_s = lax.fori_loop(n_far, qb // chunks + 1, functools.partial(slc_tile, is_near=True), carry)
        o_s = acc_s / acc_s[:, sum_lane:sum_lane + 1]

        w0 = pl.multiple_of(jnp.maximum(t0 - WINDOW, 0), tq)
        s_w = _dot_nt(q_pad, kw_ref[0, pl.ds(w0, win_keys), :]).reshape(NSA_GROUP, tq, win_keys)
        s_w = s_w + band_bias(w0 // tq, win_tiles + 1, heads)
        w_dist = (t0 - w0) - win_off
        s_w = jnp.where(((w_dist >= 0) & (w_dist < WINDOW))[None], s_w, MASK_VALUE).reshape(rows, win_keys)
        p_w = jnp.exp(s_w - jnp.max(s_w, axis=1, keepdims=True))
        acc_w = _dot(p_w.astype(BF16), with_ones(vw_ref[0, pl.ds(w0, win_keys), :], h))
        o_w = acc_w / acc_w[:, sum_lane:sum_lane + 1]

        mixed = []
        for g in range(NSA_GROUP):
            hd = NSA_GROUP * h + g
            rs = slice(g * tq, (g + 1) * tq)
            mixed.append(gates[:, hd:hd + 1] * o_c[rs]
                         + gates[:, NSA_HEADS + hd:NSA_HEADS + hd + 1] * o_s[rs]
                         + gates[:, 2 * NSA_HEADS + hd:2 * NSA_HEADS + hd + 1] * o_w[rs])
        for pair in range(NSA_GROUP // 2):
            a, c = mixed[2 * pair], mixed[2 * pair + 1]
            if h == 0:
                both = jnp.where(lane < HEAD_DIM, a, pltpu.roll(c, HEAD_DIM, 1))
            else:
                both = jnp.where(lane < HEAD_DIM, pltpu.roll(a, HEAD_DIM, 1), c)
            col = (NSA_GROUP * h + 2 * pair) * HEAD_DIM
            y_ref[0, :, col:col + LANES] = (both * _silu(z_ref[0, :, col:col + LANES])).astype(BF16)


def _nsa_prompt(rel_bias, q, ck, cv, ks, vs, kw, vw, e_mat, gates, z):
    b, t, _ = q.shape
    tq = Q_TILE
    n_cmp = ck.shape[1]
    assert t % SLC_KEYS == 0 and SLC_KEYS % tq == 0 and WINDOW % tq == 0 and t >= WINDOW + tq
    assert t // SLC_BLOCK <= LANES
    _, _, n_band = _band_geometry(tq)
    tile = lambda width: pl.BlockSpec((1, tq, width), lambda i, j: (j, i, 0))
    seq = lambda rows: pl.BlockSpec((1, rows, LANES), lambda i, j: (j, 0, 0))
    return pl.pallas_call(
        _nsa_prompt_kernel,
        grid=(t // tq, b),
        in_specs=[pl.BlockSpec(memory_space=pltpu.SMEM), tile(NSA_WIDTH), seq(n_cmp), seq(n_cmp),
                  seq(t), seq(t), seq(t), seq(t), pl.BlockSpec((t, LANES), lambda i, j: (0, 0)),
                  tile(LANES), tile(NSA_WIDTH)],
        out_specs=tile(NSA_WIDTH),
        out_shape=jax.ShapeDtypeStruct((b, t, NSA_WIDTH), BF16),
        scratch_shapes=[pltpu.VMEM((NSA_HEADS, tq, n_cmp), F32),
                        pltpu.VMEM((n_band, NSA_HEADS, tq, tq), F32)],
        compiler_params=pltpu.CompilerParams(dimension_semantics=("arbitrary", "arbitrary"),
                                             vmem_limit_bytes=VMEM_LIMIT),
        name="nsa_prompt",
    )(rel_bias, q, ck, cv, ks, vs, kw, vw, e_mat, gates, z)


def _nsa_sample_kernel(pt_ref, rel_ref, q_ref, ck_ref, cv_ref, pk_ref, pv_ref, e_ref, kwc_ref, vwc_ref, new_ref,
                       g_ref, z_ref, y_ref, kbuf, vbuf, kaug_ref, biasc_ref, biass_ref, biasw_ref, ksem, vsem):
    page_rows = pk_ref.shape[2]
    past = kbuf.shape[2]
    n_cmp = ck_ref.shape[1]
    n_slc = n_cmp // (SLC_BLOCK // CMP_BLOCK)
    win_rows = kwc_ref.shape[2]
    heads = tuple(range(NSA_HEADS))

    @pl.when(pl.program_id(0) == 0)
    def _():
        ccol = lax.broadcasted_iota(jnp.int32, (1, n_cmp), 1)
        cblock = (SLC_BLOCK // CMP_BLOCK) * (ccol % n_slc) + ccol // n_slc
        cdist = past - (CMP_BLOCK * cblock + CMP_BLOCK - 1)
        biasc_ref[...] = jnp.concatenate(_rel_bias_rows(cdist, rel_ref, heads), axis=0)
        sdist = past - lax.broadcasted_iota(jnp.int32, (1, past), 1)
        biass_ref[...] = jnp.concatenate(_rel_bias_rows(sdist, rel_ref, heads), axis=0)
        wdist = win_rows - lax.broadcasted_iota(jnp.int32, (1, win_rows), 1)
        biasw_ref[...] = jnp.concatenate(_rel_bias_rows(wdist, rel_ref, heads), axis=0)
        kaug_ref[KV_WIDTH:, :] = e_ref[...]

    slot = _double_buffered_gather(pt_ref, (pk_ref, pv_ref), (kbuf, vbuf), (ksem, vsem), past // page_rows)

    q8 = q_ref[0]
    q8f = q8.astype(F32)
    bias0 = jnp.concatenate(
        [jnp.full((1, 1), rel_ref[0, hd] - rel_ref[REL_BUCKETS - 1, hd], F32) for hd in heads], axis=0)
    lane = lax.broadcasted_iota(jnp.int32, (NSA_KV_HEADS, LANES), 1)

    s_c = _dot_nt(q8, ck_ref[0]) + biasc_ref[...]
    e_c = jnp.exp(s_c - jnp.max(s_c, axis=1, keepdims=True))
    p_c = e_c / jnp.sum(e_c, axis=1, keepdims=True)
    o_c = _dot(p_c.astype(BF16), cv_ref[0])

    imps = []
    for h in range(NSA_KV_HEADS):
        acc = p_c[NSA_GROUP * h:NSA_GROUP * h + 1]
        for g in range(1, NSA_GROUP):
            acc = acc + p_c[NSA_GROUP * h + g:NSA_GROUP * h + g + 1]
        imps.append(acc)
    imp = jnp.concatenate(imps, axis=0)
    imp = imp[:, :n_slc] + imp[:, n_slc:]
    if n_slc < LANES:
        imp = jnp.concatenate([imp, jnp.zeros((NSA_KV_HEADS, LANES - n_slc), F32)], axis=1)
    forced = (lane == 0) | (lane == n_slc - 1)
    score = jnp.where(lane < n_slc, jnp.where(forced, jnp.inf, imp), -jnp.inf)
    sel = _top_blocks(score, SLC_TOPN - 1)
    neg_mask = (sel - 1.0).astype(BF16)
    neg_rows = jnp.concatenate([neg_mask[h:h + 1] for h in range(NSA_KV_HEADS) for _ in range(NSA_GROUP)], axis=0)
    q_aug = jnp.concatenate([q8, neg_rows], axis=1)

    def with_new_token(s, s_new, v_t, v_new):
        m = jnp.maximum(jnp.max(s, axis=1, keepdims=True), s_new)
        p = jnp.exp(s - m)
        p_new = jnp.exp(s_new - m)
        total = jnp.sum(p, axis=1, keepdims=True) + p_new
        return (_dot_nt(p.astype(BF16), v_t) + p_new.astype(BF16).astype(F32) * v_new.astype(F32)) / total

    kaug_ref[:KV_WIDTH, :] = kbuf[slot].astype(BF16)
    s_s = _dot(q_aug, kaug_ref[...]) + biass_ref[...]
    s_new = jnp.sum(q8f * new_ref[0, 0:1, :].astype(F32), axis=1, keepdims=True) + bias0
    o_s = with_new_token(s_s, s_new, vbuf[slot].astype(BF16), new_ref[0, 1:2, :])

    wcol = lax.broadcasted_iota(jnp.int32, (1, win_rows), 1)
    s_w = _dot(q8, kwc_ref[0].astype(BF16)) + biasw_ref[...]
    s_w = jnp.where(win_rows - wcol < WINDOW, s_w, MASK_VALUE)
    s_new = jnp.sum(q8f * new_ref[0, 2:3, :].astype(F32), axis=1, keepdims=True) + bias0
    o_w = with_new_token(s_w, s_new, vwc_ref[0].astype(BF16), new_ref[0, 3:4, :])

    gates = g_ref[0]
    low = []
    for hd in heads:
        r = (gates[:, hd:hd + 1] * o_c[hd:hd + 1]
             + gates[:, NSA_HEADS + hd:NSA_HEADS + hd + 1] * o_s[hd:hd + 1]
             + gates[:, 2 * NSA_HEADS + hd:2 * NSA_HEADS + hd + 1] * o_w[hd:hd + 1])
        low.append(r if hd < NSA_GROUP else pltpu.roll(r, HEAD_DIM, 1))
    lane1 = lax.broadcasted_iota(jnp.int32, (1, LANES), 1)
    for pair in range(NSA_HEADS // 2):
        both = jnp.where(lane1 < HEAD_DIM, low[2 * pair], pltpu.roll(low[2 * pair + 1], HEAD_DIM, 1))
        cols = slice(LANES * pair, LANES * (pair + 1))
        y_ref[0, :, cols] = (both * _silu(z_ref[0, :, cols])).astype(BF16)


def _nsa_sample(page_table, rel_bias, q8, ck, cv, pool_k, pool_v, e_mat, win_k, win_v, new_rows, gates, z):
    n_seq, n_pages = page_table.shape
    page_rows = pool_k.shape[2]
    past = n_pages * page_rows
    n_cmp = ck.shape[1]
    win_rows = win_k.shape[2]
    assert past % SLC_BLOCK == 0 and past // SLC_BLOCK <= LANES and page_rows % LANES == 0
    hbm = pl.BlockSpec(memory_space=pl.ANY)
    per_seq = lambda r, w: pl.BlockSpec((1, r, w), lambda i, pt: (i, 0, 0))
    return pl.pallas_call(
        _nsa_sample_kernel,
        grid_spec=pltpu.PrefetchScalarGridSpec(
            num_scalar_prefetch=1,
            grid=(n_seq,),
            in_specs=[pl.BlockSpec(memory_space=pltpu.SMEM), per_seq(NSA_HEADS, LANES), per_seq(n_cmp, LANES),
                      per_seq(n_cmp, LANES), hbm, hbm,
                      pl.BlockSpec((LANES, past), lambda i, pt: (0, 0)),
                      per_seq(KV_WIDTH, win_rows), per_seq(KV_WIDTH, win_rows), per_seq(4, LANES),
                      per_seq(1, LANES), per_seq(1, NSA_WIDTH)],
            out_specs=per_seq(1, NSA_WIDTH),
            scratch_shapes=[pltpu.VMEM((2, KV_WIDTH, past), F32), pltpu.VMEM((2, KV_WIDTH, past), F32),
                            pltpu.VMEM((KV_WIDTH + LANES, past), BF16),
                            pltpu.VMEM((NSA_HEADS, n_cmp), F32), pltpu.VMEM((NSA_HEADS, past), F32),
                            pltpu.VMEM((NSA_HEADS, win_rows), F32),
                            pltpu.SemaphoreType.DMA((2,)), pltpu.SemaphoreType.DMA((2,))]),
        out_shape=jax.ShapeDtypeStruct((n_seq, 1, NSA_WIDTH), BF16),
        compiler_params=pltpu.CompilerParams(dimension_semantics=("arbitrary",), vmem_limit_bytes=VMEM_LIMIT),
        name="nsa_sample",
    )(page_table.reshape(-1), rel_bias, q8, ck, cv, pool_k, pool_v, e_mat, win_k, win_v, new_rows, gates, z)


def _group_norm_gate(o, z, gn, seg):
    mu = _segment_sum(o, seg) * (1.0 / RET_VDIM)
    d = o - mu
    var = _segment_sum(d * d, seg) * (1.0 / RET_VDIM)
    return (d * lax.rsqrt(var + EPS) * gn * _silu(z)).astype(BF16)


def _retention_prompt_kernel(q_ref, k_ref, v_ref, z_ref, gn_ref, seg_ref, dec_ref, qdec_ref, kdec_ref, cdec_ref,
                             y_ref, st_ref, state):
    chunk = pl.program_id(1)
    tc = q_ref.shape[1]
    lane = lax.broadcasted_iota(jnp.int32, (tc, LANES), 1)
    srow = lax.broadcasted_iota(jnp.int32, (LANES, LANES), 0)
    scol = lax.broadcasted_iota(jnp.int32, (LANES, LANES), 1)
    same_head = (srow // RET_KDIM) == (scol // RET_VDIM)

    @pl.when(chunk == 0)
    def _():
        state[...] = jnp.zeros_like(state)

    for pair in range(RET_HEADS // 2):
        cols = slice(LANES * pair, LANES * (pair + 1))
        q = q_ref[0, :, cols]
        k = k_ref[0, :, cols]
        v = v_ref[0, :, cols]
        s_old = state[pair]
        cross = _dot(q, s_old.astype(BF16)) * qdec_ref[:, cols]
        halves = []
        for e in range(2):
            qe = jnp.where((lane // RET_KDIM) == e, q, jnp.zeros((), BF16))
            scores = _dot_nt(qe, k) * dec_ref[2 * pair + e]
            halves.append(_dot(scores.astype(BF16), v))
        o = jnp.where(lane < RET_VDIM, halves[0], halves[1]) + cross
        y_ref[0, :, cols] = _group_norm_gate(o, z_ref[0, :, cols], gn_ref[:, cols], seg_ref[...])
        kd_t = (k.astype(F32) * kdec_ref[:, cols]).T.astype(BF16)
        s_new = s_old * cdec_ref[:, cols] + jnp.where(same_head, _dot(kd_t, v), 0.0)
        state[pair] = s_new

        @pl.when(chunk == pl.num_programs(1) - 1)
        def _():
            for e in range(2):
                st_ref[0, 2 * pair + e] = s_new[RET_KDIM * e:RET_KDIM * (e + 1), RET_VDIM * e:RET_VDIM * (e + 1)]


def _retention_tables(tc):
    log_g = jnp.asarray(RET_GAMMA_LOG, F32)
    i = jnp.arange(tc, dtype=F32)
    diff = i[:, None] - i[None, :]
    decay = jnp.where(diff >= 0, jnp.exp(jnp.maximum(diff, 0.0)[None] * log_g[:, None, None]), 0.0)
    widen = lambda a: jnp.repeat(a, RET_VDIM, axis=-1)
    q_decay = widen(jnp.exp((i[:, None] + 1.0) * log_g[None, :]))
    k_decay = widen(jnp.exp((tc - 1.0 - i)[:, None] * log_g[None, :]))
    c_decay = widen(jnp.exp(tc * log_g)[None, :])
    return decay, q_decay, k_decay, c_decay


def _retention_prompt(q, k, v, z, gn, seg):
    b, t, _ = q.shape
    tc = RET_TILE
    assert t % tc == 0
    decay, q_decay, k_decay, c_decay = _retention_tables(tc)
    tile = pl.BlockSpec((1, tc, RET_WIDTH), lambda i, j: (i, j, 0))
    const = lambda a: pl.BlockSpec(a.shape, lambda i, j: (0,) * a.ndim)
    return pl.pallas_call(
        _retention_prompt_kernel,
        grid=(b, t // tc),
        in_specs=[tile, tile, tile, tile, const(gn), const(seg), const(decay), const(q_decay), const(k_decay),
                  const(c_decay)],
        out_specs=[tile, pl.BlockSpec((1, RET_HEADS, RET_KDIM, RET_VDIM), lambda i, j: (i, 0, 0, 0))],
        out_shape=[jax.ShapeDtypeStruct((b, t, RET_WIDTH), BF16),
                   jax.ShapeDtypeStruct((b, RET_HEADS, RET_KDIM, RET_VDIM), F32)],
        scratch_shapes=[pltpu.VMEM((RET_HEADS // 2, LANES, LANES), F32)],
        compiler_params=pltpu.CompilerParams(dimension_semantics=("arbitrary", "arbitrary"),
                                             vmem_limit_bytes=VMEM_LIMIT),
        name="retention_prompt",
    )(q, k, v, z, gn, seg, decay, q_decay, k_decay, c_decay)


def _retention_sample_kernel(qt_ref, kt_ref, v_ref, z_ref, gn_ref, gam_ref, st_ref, y_ref, so_ref):
    n = v_ref.shape[1]
    qt = qt_ref[0]
    kt = kt_ref[0]
    gam = gam_ref[...]
    qk = jnp.sum(qt * kt, axis=1)
    for s in range(n):
        st = st_ref[s]
        vs = v_ref[0, s]
        cross = jnp.sum(qt[:, :, s:s + 1] * st, axis=1)
        so_ref[s] = st * gam[:, :, None] + kt[:, :, s:s + 1] * vs[:, None, :]
        o = qk[:, s:s + 1] * vs + gam * cross
        mu = jnp.mean(o, axis=-1, keepdims=True)
        d = o - mu
        var = jnp.mean(d * d, axis=-1, keepdims=True)
        y_ref[0, s] = (d * lax.rsqrt(var + EPS) * gn_ref[...] * _silu(z_ref[0, s])).astype(BF16)


def _retention_sample(q, k, v, z, gn, state):
    n_seq = q.shape[0]
    n = min(RET_SAMPLE_SEQS, n_seq)
    assert n_seq % n == 0
    steps = n_seq // n
    to_cols = lambda a: a.astype(F32).reshape(steps, n, RET_HEADS, RET_KDIM).transpose(0, 2, 3, 1)
    to_rows = lambda a: a.astype(F32).reshape(steps, n, RET_HEADS, RET_VDIM)
    gam = jnp.exp(jnp.asarray(RET_GAMMA_LOG, F32)).reshape(RET_HEADS, 1)
    cols = pl.BlockSpec((1, RET_HEADS, RET_KDIM, n), lambda i: (i, 0, 0, 0))
    rws = pl.BlockSpec((1, n, RET_HEADS, RET_VDIM), lambda i: (i, 0, 0, 0))
    st = pl.BlockSpec((n, RET_HEADS, RET_KDIM, RET_VDIM), lambda i: (i, 0, 0, 0))
    y, new_state = pl.pallas_call(
        _retention_sample_kernel,
        grid=(steps,),
        in_specs=[cols, cols, rws, rws, pl.BlockSpec((RET_HEADS, RET_VDIM), lambda i: (0, 0)),
                  pl.BlockSpec((RET_HEADS, 1), lambda i: (0, 0)), st],
        out_specs=[rws, st],
        out_shape=[jax.ShapeDtypeStruct((steps, n, RET_HEADS, RET_VDIM), BF16),
                   jax.ShapeDtypeStruct(state.shape, F32)],
        compiler_params=pltpu.CompilerParams(dimension_semantics=("parallel",), vmem_limit_bytes=VMEM_LIMIT),
        name="retention_sample",
    )(to_cols(q), to_cols(k), to_rows(v), to_rows(z), gn.reshape(RET_HEADS, RET_VDIM), gam, state)
    return y.reshape(n_seq, RET_WIDTH), new_state


def _out_kernel(x_ref, yn_ref, yr_ref, w_ref, o_ref):
    y = jnp.concatenate([yn_ref[0], yr_ref[0]], axis=1)
    o_ref[0] = x_ref[0] + _dot(y, w_ref[...])


def _out_project(x, y_nsa, y_ret, w):
    nb, rows, d = x.shape
    tm = min(PROJ_ROWS, rows)
    spec = lambda width: pl.BlockSpec((1, tm, width), lambda i, j: (i, j, 0))
    return pl.pallas_call(
        _out_kernel,
        grid=(nb, rows // tm),
        in_specs=[spec(d), spec(NSA_WIDTH), spec(RET_WIDTH), pl.BlockSpec(w.shape, lambda i, j: (0, 0))],
        out_specs=spec(d),
        out_shape=jax.ShapeDtypeStruct(x.shape, F32),
        compiler_params=pltpu.CompilerParams(dimension_semantics=("parallel", "parallel"),
                                             vmem_limit_bytes=VMEM_LIMIT),
        name="output_projection",
    )(x, y_nsa, y_ret, w)


def _rotary_tables(pos):
    half = RET_KDIM // 2
    inv = ROPE_BASE ** (-jnp.arange(half, dtype=F32) / half)
    ang = pos.astype(F32)[:, None] * inv[None, :]
    cos, sin = jnp.cos(ang), jnp.sin(ang)
    reps = LANES // RET_KDIM
    return jnp.tile(jnp.concatenate([cos, cos], axis=1), (1, reps)), jnp.tile(jnp.concatenate([-sin, sin], axis=1), (1, reps))


def _arrange_w_in(w_in):
    d = w_in.shape[0]
    parts = jnp.split(w_in, np.cumsum(SPLIT_WIDTHS)[:-1].tolist(), axis=1)
    order = [hd for g in range(NSA_GROUP) for hd in (g, NSA_GROUP + g)]
    q = parts[0].reshape(d, NSA_HEADS, HEAD_DIM)[:, order].reshape(d, NSA_WIDTH)
    gates = parts[7].reshape(d, NSA_HEADS, 3).transpose(0, 2, 1).reshape(d, 3 * NSA_HEADS)
    gates = jnp.pad(gates, ((0, 0), (0, LANES - 3 * NSA_HEADS)))
    w = jnp.concatenate([q] + parts[1:7] + [gates] + parts[8:], axis=1)
    assert w.shape[1] == COL_END
    return w.astype(BF16)


def _compress_weights(pe, w1, w2):
    w1r = w1.reshape(CMP_BLOCK, HEAD_DIM, HEAD_DIM)
    w1_both = jnp.zeros((CMP_BLOCK, NSA_KV_HEADS, HEAD_DIM, NSA_KV_HEADS, HEAD_DIM), F32)
    w2_both = jnp.zeros((NSA_KV_HEADS, HEAD_DIM, NSA_KV_HEADS, HEAD_DIM), F32)
    for h in range(NSA_KV_HEADS):
        w1_both = w1_both.at[:, h, :, h, :].set(w1r)
        w2_both = w2_both.at[h, :, h, :].set(w2)
    return (jnp.tile(pe, (1, NSA_KV_HEADS)), w1_both.reshape(CMP_BLOCK * KV_WIDTH, KV_WIDTH).astype(BF16),
            w2_both.reshape(KV_WIDTH, KV_WIDTH).astype(BF16))


def _even_odd(c):
    n, nc, w = c.shape
    ratio = SLC_BLOCK // CMP_BLOCK
    return c.reshape(n, nc // ratio, ratio, w).transpose(0, 2, 1, 3).reshape(n, nc, w)


def _block_membership(n_keys):
    k = jnp.arange(n_keys)[:, None] // SLC_BLOCK
    return jnp.where(k == jnp.arange(LANES)[None, :], -MASK_VALUE, 0.0).astype(BF16)


def kernel(x_prompt, x_sample, cache_cmp_k, cache_cmp_v, cache_slc_k, cache_slc_v, cache_win_k, cache_win_v,
           state_ret, page_table, norm_g, w_in, q_norm_g, k_norm_g, cmp_pe_k, cmp_w1_k, cmp_w2_k, cmp_pe_v,
           cmp_w1_v, cmp_w2_v, rel_bias, ret_gn_g, w_out):
    b, t, d = x_prompt.shape
    n_seq, dec_len, _ = x_sample.shape
    n_pages = page_table.shape[1]
    page_rows = cache_cmp_k.shape[1]
    past = n_pages * page_rows
    assert dec_len == 1 and past % CMP_BLOCK == 0
    kv4 = lambda a: a.reshape(a.shape[0], a.shape[1], NSA_KV_HEADS, HEAD_DIM)

    w = _arrange_w_in(w_in)
    w_o = w_out.astype(BF16)
    ng = norm_g.reshape(1, d)
    qg = (jnp.tile(q_norm_g, LANES // HEAD_DIM) * ATTN_SCALE).reshape(1, LANES)
    kg = jnp.tile(k_norm_g, LANES // HEAD_DIM).reshape(1, LANES)
    gn = ret_gn_g.reshape(1, RET_WIDTH)
    lane = np.arange(LANES)
    seg = jnp.asarray(lane[:, None] // HEAD_DIM == lane[None, :] // HEAD_DIM, BF16)
    cw = _compress_weights(cmp_pe_k, cmp_w1_k, cmp_w2_k) + _compress_weights(cmp_pe_v, cmp_w1_v, cmp_w2_v)

    cos, sin = _rotary_tables(jnp.arange(t))
    (q, kc, vc, ks, vs, kw, vw, ksb, vsb, kwb, vwb, gates, zn, qr, kr, vr, zr) = _project(
        x_prompt, cos, sin, ng, w, qg, kg, seg)
    ck, cv = _compress_prompt(kc, vc, cw, kg, seg)
    y_nsa = _nsa_prompt(rel_bias, q, _even_odd(ck), _even_odd(cv), ksb, vsb, kwb, vwb, _block_membership(t), gates, zn)
    y_ret, p_ret = _retention_prompt(qr, kr, vr, zr, gn, seg)
    y_prompt = _out_project(x_prompt, y_nsa, y_ret, w_o)
    keep = min(WINDOW, t)
    prompt_out = (y_prompt, kv4(kc), kv4(vc), kv4(ks), kv4(vs), kv4(kw[:, t - keep:]), kv4(vw[:, t - keep:]), p_ret)

    cos, sin = _rotary_tables(jnp.full((n_seq,), past))
    xs = x_sample.reshape(1, n_seq, d)
    (q, kc, vc, ks, vs, kw, vw, ksb, vsb, kwb, vwb, gates, zn, qr, kr, vr, zr) = [
        a[0] for a in _project(xs, cos, sin, ng, w, qg, kg, seg)]
    chan = lambda a: jnp.transpose(a, (0, 2, 3, 1)).reshape(a.shape[0], KV_WIDTH, a.shape[1])
    ck, cv = _compress_sample(page_table, chan(cache_cmp_k), chan(cache_cmp_v), cw, kg, seg)
    half = jnp.asarray(lane[None, :] // HEAD_DIM == (np.arange(NSA_HEADS) // NSA_GROUP)[:, None])
    q8 = jnp.where(half[None], jnp.tile(q.reshape(n_seq, NSA_GROUP, LANES), (1, NSA_KV_HEADS, 1)), jnp.zeros((), BF16))
    new_rows = jnp.stack([ksb, vsb, kwb, vwb], axis=1)
    y_nsa = _nsa_sample(page_table, rel_bias, q8, _even_odd(ck), _even_odd(cv), chan(cache_slc_k), chan(cache_slc_v),
                        _block_membership(past).T, chan(cache_win_k), chan(cache_win_v), new_rows,
                        gates.reshape(n_seq, 1, LANES), zn.reshape(n_seq, 1, NSA_WIDTH))
    y_ret, s_ret = _retention_sample(qr, kr, vr, zr, gn, state_ret)
    y_sample = _out_project(xs, y_nsa.reshape(1, n_seq, NSA_WIDTH), y_ret.reshape(1, n_seq, RET_WIDTH), w_o)
    keep = min(WINDOW, cache_win_k.shape[1] + 1)
    new4 = lambda a: a.reshape(n_seq, 1, NSA_KV_HEADS, HEAD_DIM)
    s_win_k = jnp.concatenate([cache_win_k, new4(kw)], axis=1)[:, -keep:]
    s_win_v = jnp.concatenate([cache_win_v, new4(vw)], axis=1)[:, -keep:]
    sample_out = (y_sample.reshape(n_seq, 1, d), new4(kc), new4(vc), new4(ks), new4(vs), s_win_k, s_win_v, s_ret)

    return (prompt_out[0], sample_out[0]) + prompt_out[1:] + sample_out[1:]
```

```python
import functools

import numpy as np
import jax
import jax.numpy as jnp
from jax import lax
from jax.experimental import pallas as pl
from jax.experimental.pallas import tpu as pltpu

F32, BF16 = jnp.float32, jnp.bfloat16

NSA_HEADS = 8
NSA_KV_HEADS = 2
HEAD_DIM = 64
NSA_GROUP = NSA_HEADS // NSA_KV_HEADS
NSA_WIDTH = NSA_HEADS * HEAD_DIM
KV_WIDTH = NSA_KV_HEADS * HEAD_DIM
CMP_BLOCK = 32
SLC_BLOCK = 64
SLC_TOPN = 16
WINDOW = 512
ATTN_SCALE = HEAD_DIM ** -0.5
RET_HEADS = 8
RET_KDIM = 64
RET_VDIM = 64
RET_WIDTH = RET_HEADS * RET_VDIM
ROPE_BASE = 10000.0
REL_BUCKETS = 32
REL_MAX_DIST = 1024
EPS = 1e-6
SPLIT_WIDTHS = (NSA_WIDTH, KV_WIDTH, KV_WIDTH, KV_WIDTH, KV_WIDTH, KV_WIDTH, KV_WIDTH,
                3 * NSA_HEADS, NSA_WIDTH, RET_HEADS * RET_KDIM, RET_HEADS * RET_KDIM, RET_WIDTH, RET_WIDTH)

LANES = 128
VMEM_LIMIT = 56 * 1024 * 1024

PROJ_ROWS = 256
Q_TILE = 128
SLC_KEYS = 512
RET_TILE = 128
RET_SAMPLE_SEQS = 16
CMP_PITCH = CMP_BLOCK + 4

MASK_VALUE = -float(2 ** 30)

MXU_COLS = 256
COL_Q = 0
COL_KC = COL_Q + NSA_WIDTH
COL_VC = COL_KC + LANES
COL_KS = COL_VC + LANES
COL_VS = COL_KS + LANES
COL_KW = COL_VS + LANES
COL_VW = COL_KW + LANES
COL_ZN = COL_VW + LANES
COL_QR = COL_ZN + NSA_WIDTH
COL_KR = COL_QR + RET_WIDTH
COL_VR = COL_KR + RET_WIDTH
COL_ZR = COL_VR + RET_WIDTH
COL_G = COL_ZR + RET_WIDTH
COL_END = COL_G + LANES


def _bucket_lower_bounds():
    exact = REL_BUCKETS // 2
    ratio = REL_MAX_DIST // exact
    lows = list(range(exact))
    n = exact
    for k in range(REL_BUCKETS - exact):
        while n ** (REL_BUCKETS - exact) < exact ** (REL_BUCKETS - exact) * ratio ** k:
            n += 1
        lows.append(n)
    return tuple(lows)


BUCKET_LOW = _bucket_lower_bounds()
FAR_DIST = BUCKET_LOW[-1]
RET_GAMMA_LOG = tuple(float(np.log1p(-np.exp2(-5.0 - h))) for h in range(RET_HEADS))


def _dot(a, b):
    return jnp.dot(a, b, preferred_element_type=F32)


def _dot_nt(a, b):
    return lax.dot_general(a, b, (((1,), (1,)), ((), ())), preferred_element_type=F32)


def _segment_sum(v, seg):
    hi = v.astype(BF16)
    lo = (v - hi.astype(F32)).astype(BF16)
    return _dot(hi, seg) + _dot(lo, seg)


def _head_rms(y, g, seg):
    ms = _segment_sum(y * y, seg) * (1.0 / HEAD_DIM)
    return y * lax.rsqrt(ms + EPS) * g


def _silu(x):
    return x * jax.nn.sigmoid(x)


def _rel_bias_rows(dist, rel_ref, heads):
    out = [jnp.full(dist.shape, rel_ref[0, hd], F32) for hd in heads]
    for bkt in range(1, REL_BUCKETS):
        hit = dist >= BUCKET_LOW[bkt]
        out = [jnp.where(hit, rel_ref[bkt, hd], o) for hd, o in zip(heads, out)]
    return [o - rel_ref[REL_BUCKETS - 1, hd] for hd, o in zip(heads, out)]


def _top_blocks(score, n_pick):
    col = lax.broadcasted_iota(jnp.int32, score.shape, 1).astype(F32)
    sel = jnp.zeros(score.shape, F32)
    neg_inf = -jnp.inf
    for _ in range(n_pick):
        m = jnp.max(score, axis=1, keepdims=True)
        first = jnp.min(jnp.where(score == m, col, float(score.shape[1])), axis=1, keepdims=True)
        hit = col == first
        sel = jnp.where(hit & (m > neg_inf), 1.0, sel)
        score = jnp.where(hit, neg_inf, score)
    return sel


def _top_blocks_cols(score, n_pick):
    n_rows = score.shape[0]
    rowi = lax.broadcasted_iota(jnp.int32, score.shape, 0).astype(F32)
    sel = jnp.zeros(score.shape, F32)
    neg_inf = -jnp.inf
    for _ in range(n_pick):
        m = jnp.max(score, axis=0, keepdims=True)
        first = jnp.min(jnp.where(score == m, rowi, float(n_rows)), axis=0, keepdims=True)
        hit = rowi == first
        sel = sel + jnp.where(hit, jnp.where(m > neg_inf, 1.0, 0.0), 0.0)
        score = jnp.where(hit, neg_inf, score)
    return sel


def _proj_kernel(x_ref, cos_ref, sin_ref, ng_ref, w_ref, qg_ref, kg_ref, seg_ref,
                 q_o, kc_o, vc_o, ks_o, vs_o, kw_o, vw_o, ksb_o, vsb_o, kwb_o, vwb_o,
                 g_o, zn_o, qr_o, kr_o, vr_o, zr_o, vst_o, vwt_o):
    x = x_ref[0]
    inv = lax.rsqrt(jnp.mean(x * x, axis=-1, keepdims=True) + EPS)
    xn = (x * inv * ng_ref[...]).astype(BF16)
    seg = seg_ref[...]

    def mm(col, width):
        return _dot(xn, w_ref[:, col:col + width])

    def chunk(y, c):
        return y[:, LANES * c:LANES * (c + 1)]

    y = mm(COL_Q, NSA_WIDTH)
    for c in range(NSA_WIDTH // LANES):
        q_o[0, :, LANES * c:LANES * (c + 1)] = _head_rms(chunk(y, c), qg_ref[...], seg).astype(BF16)
    y = mm(COL_KC, 2 * LANES)
    kc_o[0] = chunk(y, 0)
    vc_o[0] = chunk(y, 1)
    for col, k_o, v_o, kb_o, vb_o, vt_o in ((COL_KS, ks_o, vs_o, ksb_o, vsb_o, vst_o),
                                            (COL_KW, kw_o, vw_o, kwb_o, vwb_o, vwt_o)):
        y = mm(col, 2 * LANES)
        k = _head_rms(chunk(y, 0), kg_ref[...], seg)
        k_o[0] = k
        kb_o[0] = k.astype(BF16)
        v = chunk(y, 1)
        v_o[0] = v
        vb_o[0] = v.astype(BF16)
        for c in range(v.shape[0] // LANES):
            vt_o[0, c] = v[LANES * c:LANES * (c + 1)].T.astype(BF16)
    zn_o[0] = mm(COL_ZN, NSA_WIDTH)

    cos = cos_ref[...]
    sin = sin_ref[...]
    lane = lax.broadcasted_iota(jnp.int32, cos.shape, 1)
    first_half = (lane % HEAD_DIM) < (HEAD_DIM // 2)
    for col, o_ref, scale in ((COL_QR, qr_o, None), (COL_KR, kr_o, RET_KDIM ** -0.5)):
        y = mm(col, RET_WIDTH)
        for c in range(RET_WIDTH // LANES):
            yc = chunk(y, c)
            partner = jnp.where(first_half, pltpu.roll(yc, LANES - HEAD_DIM // 2, 1),
                                pltpu.roll(yc, HEAD_DIM // 2, 1))
            r = yc * cos + partner * sin
            if scale is not None:
                r = r * scale
            o_ref[0, :, LANES * c:LANES * (c + 1)] = r.astype(BF16)
    vr_o[0] = mm(COL_VR, RET_WIDTH).astype(BF16)
    zr_o[0] = mm(COL_ZR, RET_WIDTH)
    g_o[0] = jax.nn.sigmoid(mm(COL_G, LANES))


def _project(x, cos, sin, norm_g, w, qg, kg, seg):
    nb, rows, d = x.shape
    tm = min(PROJ_ROWS, rows)
    assert rows % tm == 0 and tm % LANES == 0
    row_spec = lambda width: pl.BlockSpec((1, tm, width), lambda i, j: (i, j, 0))
    const = lambda shape: pl.BlockSpec(shape, lambda i, j: (0,) * len(shape))
    outs = [(NSA_WIDTH, BF16)] + [(LANES, F32)] * 6 + [(LANES, BF16)] * 4 + [(LANES, F32), (NSA_WIDTH, F32),
            (RET_WIDTH, BF16), (RET_WIDTH, BF16), (RET_WIDTH, BF16), (RET_WIDTH, F32)]
    blocks_spec = pl.BlockSpec((1, tm // LANES, LANES, LANES), lambda i, j: (i, j, 0, 0))
    blocks_shape = jax.ShapeDtypeStruct((nb, rows // LANES, LANES, LANES), BF16)
    return pl.pallas_call(
        _proj_kernel,
        grid=(nb, rows // tm),
        in_specs=[row_spec(d), pl.BlockSpec((tm, LANES), lambda i, j: (j, 0)),
                  pl.BlockSpec((tm, LANES), lambda i, j: (j, 0)), const((1, d)), const(w.shape),
                  const((1, LANES)), const((1, LANES)), const((LANES, LANES))],
        out_specs=[row_spec(wd) for wd, _ in outs] + [blocks_spec] * 2,
        out_shape=[jax.ShapeDtypeStruct((nb, rows, wd), dt) for wd, dt in outs] + [blocks_shape] * 2,
        compiler_params=pltpu.CompilerParams(dimension_semantics=("parallel", "parallel"),
                                             vmem_limit_bytes=VMEM_LIMIT),
        name="input_projection",
    )(x, cos, sin, norm_g, w, qg, kg, seg)


def _compress_rows(buf_ref, pe_ref, w1_ref, w2_ref, n_blk, pitch=CMP_BLOCK):
    parts = []
    for i in range(CMP_BLOCK):
        rows = buf_ref[pl.ds(i, n_blk, stride=pitch), :] + pe_ref[i:i + 1, :]
        parts.append(rows.astype(BF16))
    flat = jnp.concatenate(parts, axis=1)
    hidden = _silu(_dot(flat, w1_ref[...]))
    return _dot(hidden.astype(BF16), w2_ref[...])


def _compress_kernel(kc_ref, vc_ref, pek_ref, w1k_ref, w2k_ref, pev_ref, w1v_ref, w2v_ref, kg_ref, seg_ref,
                     ck_o, cv_o):
    n_blk = ck_o.shape[1]
    ck = _compress_rows(kc_ref.at[0], pek_ref, w1k_ref, w2k_ref, n_blk)
    ck_o[0] = _head_rms(ck, kg_ref[...], seg_ref[...]).astype(BF16)
    cv_o[0] = _compress_rows(vc_ref.at[0], pev_ref, w1v_ref, w2v_ref, n_blk).astype(BF16)


def _compress_prompt(kc, vc, cw, kg, seg):
    b, t, _ = kc.shape
    n_blk = t // CMP_BLOCK
    seq = pl.BlockSpec((1, t, LANES), lambda i: (i, 0, 0))
    const = lambda a: pl.BlockSpec(a.shape, lambda i: (0,) * a.ndim)
    out = pl.BlockSpec((1, n_blk, LANES), lambda i: (i, 0, 0))
    return pl.pallas_call(
        _compress_kernel,
        grid=(b,),
        in_specs=[seq, seq] + [const(a) for a in cw] + [const(kg), const(seg)],
        out_specs=[out, out],
        out_shape=[jax.ShapeDtypeStruct((b, n_blk, LANES), BF16)] * 2,
        compiler_params=pltpu.CompilerParams(dimension_semantics=("parallel",), vmem_limit_bytes=VMEM_LIMIT),
        name="compress_prompt",
    )(kc, vc, *cw, kg, seg)


def _page_copy(pt_ref, pool_ref, buf_ref, sem, seq, page, n_pages):
    rows = pool_ref.shape[2]
    return pltpu.make_async_copy(pool_ref.at[pt_ref[seq * n_pages + page]],
                                 buf_ref.at[:, pl.ds(page * rows, rows)], sem)


def _gather_pages(pt_ref, pools, bufs, sems, seq, slot, n_pages, wait):
    for pool_ref, buf_ref, sem in zip(pools, bufs, sems):
        for page in range(n_pages):
            copy = _page_copy(pt_ref, pool_ref, buf_ref.at[slot], sem.at[slot], seq, page, n_pages)
            if wait:
                copy.wait()
            else:
                copy.start()


def _double_buffered_gather(pt_ref, pools, bufs, sems, n_pages):
    seq = pl.program_id(0)
    slot = seq % 2

    @pl.when(seq == 0)
    def _():
        _gather_pages(pt_ref, pools, bufs, sems, seq, slot, n_pages, wait=False)

    @pl.when(seq + 1 < pl.num_programs(0))
    def _():
        _gather_pages(pt_ref, pools, bufs, sems, seq + 1, 1 - slot, n_pages, wait=False)

    _gather_pages(pt_ref, pools, bufs, sems, seq, slot, n_pages, wait=True)
    return slot


def _compress_sample_kernel(pt_ref, pk_ref, pv_ref, pek_ref, w1k_ref, w2k_ref, pev_ref, w1v_ref, w2v_ref,
                            kg_ref, seg_ref, ck_o, cv_o, kbuf, vbuf, rows_ref, ksem, vsem):
    page_rows = pk_ref.shape[2]
    n_pages = kbuf.shape[2] // page_rows
    n_blk = ck_o.shape[1]
    slot = _double_buffered_gather(pt_ref, (pk_ref, pv_ref), (kbuf, vbuf), (ksem, vsem), n_pages)

    blocks_per_page = page_rows // CMP_BLOCK

    def token_major(buf):
        for page in range(n_pages):
            tokens = buf[slot, :, page * page_rows:(page + 1) * page_rows].T
            for c in range(blocks_per_page):
                r0 = (page * blocks_per_page + c) * CMP_PITCH
                rows_ref[r0:r0 + CMP_BLOCK, :] = tokens[c * CMP_BLOCK:(c + 1) * CMP_BLOCK]

    token_major(kbuf)
    ck = _compress_rows(rows_ref, pek_ref, w1k_ref, w2k_ref, n_blk, CMP_PITCH)
    ck_o[0] = _head_rms(ck, kg_ref[...], seg_ref[...]).astype(BF16)
    token_major(vbuf)
    cv_o[0] = _compress_rows(rows_ref, pev_ref, w1v_ref, w2v_ref, n_blk, CMP_PITCH).astype(BF16)


def _compress_sample(page_table, pool_k, pool_v, cw, kg, seg):
    n_seq, n_pages = page_table.shape
    page_rows = pool_k.shape[2]
    assert page_rows == LANES
    past = n_pages * page_rows
    n_blk = past // CMP_BLOCK
    hbm = pl.BlockSpec(memory_space=pl.ANY)
    const = lambda a: pl.BlockSpec(a.shape, lambda i, pt: (0,) * a.ndim)
    out = pl.BlockSpec((1, n_blk, LANES), lambda i, pt: (i, 0, 0))
    return pl.pallas_call(
        _compress_sample_kernel,
        grid_spec=pltpu.PrefetchScalarGridSpec(
            num_scalar_prefetch=1,
            grid=(n_seq,),
            in_specs=[hbm, hbm] + [const(a) for a in cw] + [const(kg), const(seg)],
            out_specs=[out, out],
            scratch_shapes=[pltpu.VMEM((2, KV_WIDTH, past), F32), pltpu.VMEM((2, KV_WIDTH, past), F32),
                            pltpu.VMEM((n_blk * CMP_PITCH, KV_WIDTH), F32),
                            pltpu.SemaphoreType.DMA((2,)), pltpu.SemaphoreType.DMA((2,))]),
        out_shape=[jax.ShapeDtypeStruct((n_seq, n_blk, LANES), BF16)] * 2,
        compiler_params=pltpu.CompilerParams(dimension_semantics=("arbitrary",), vmem_limit_bytes=VMEM_LIMIT),
        name="compress_sample",
    )(page_table.reshape(-1), pool_k, pool_v, *cw, kg, seg)


def _band_geometry(tq):
    near = -(-(FAR_DIST + tq - 1) // tq)
    rel_max = near + SLC_KEYS // tq - 2
    rel_min = -max(WINDOW // tq, SLC_KEYS // tq - 1)
    return near, rel_max, rel_max - rel_min + 1


def _nsa_prompt_kernel(rel_ref, q_ref, ck_ref, cv_ref, ks_ref, vst_ref, kw_ref, vwt_ref, e_ref, g_ref, z_ref,
                       y_ref, biasc_ref, band_ref):
    tq = q_ref.shape[1]
    n_cmp = ck_ref.shape[1]
    n_slc = n_cmp // (SLC_BLOCK // CMP_BLOCK)
    rows = NSA_GROUP * tq
    kt = SLC_KEYS
    chunks = kt // tq
    near, rel_max, n_band = _band_geometry(tq)
    win_tiles = WINDOW // tq
    win_keys = WINDOW + tq
    qb = pl.program_id(0)
    b = pl.program_id(1)
    t0 = qb * tq
    all_heads = tuple(range(NSA_HEADS))

    lane = lax.broadcasted_iota(jnp.int32, (tq, LANES), 1)
    row = lax.broadcasted_iota(jnp.int32, (tq, LANES), 0)

    @pl.when((qb == 0) & (b == 0))
    def _():
        for c in range(n_band):
            dist = lane - row + tq * (rel_max - c)
            for hd, bias in zip(all_heads, _rel_bias_rows(dist, rel_ref, all_heads)):
                band_ref[c, hd] = bias

    def band_bias(first_chunk, n_chunks, h):
        base = rel_max - qb + first_chunk
        return jnp.concatenate(
            [jnp.concatenate([band_ref[base + u, NSA_GROUP * h + g] for g in range(NSA_GROUP)], axis=1)
             for u in range(n_chunks)], axis=0)

    ccol = lax.broadcasted_iota(jnp.int32, (tq, n_cmp), 1)
    crow = lax.broadcasted_iota(jnp.int32, (tq, n_cmp), 0)
    cblock = (SLC_BLOCK // CMP_BLOCK) * (ccol % n_slc) + ccol // n_slc
    cdist = t0 + crow - (CMP_BLOCK * cblock + CMP_BLOCK - 1)

    @pl.when(b == 0)
    def _():
        for hd, bias in zip(all_heads, _rel_bias_rows(cdist, rel_ref, all_heads)):
            biasc_ref[hd] = bias

    q_all = q_ref[0]
    gates = g_ref[0]
    ck = ck_ref[0]
    cv = cv_ref[0]
    neg_inf = -jnp.inf
    carry0 = (jnp.full((1, rows), neg_inf, F32), jnp.zeros((KV_WIDTH, rows), F32))

    def key_minus_query(n_keys):
        return (lax.broadcasted_iota(jnp.int32, (n_keys, rows), 0)
                - lax.broadcasted_iota(jnp.int32, (n_keys, rows), 1) % tq)

    ones = jnp.ones((), BF16)

    def with_ones(v_t, h):
        own = (lax.broadcasted_iota(jnp.int32, v_t.shape, 0) // HEAD_DIM) == h
        return jnp.where(own, v_t, ones)

    def softmax_step(carry, s, v_t):
        m, acc = carry
        m_new = jnp.maximum(m, jnp.max(s, axis=0, keepdims=True))
        p = jnp.exp(s - m_new)
        return m_new, jnp.exp(m - m_new) * acc + _dot(v_t, p.astype(BF16))

    def value_blocks(ref, first_chunk, n_chunks):
        return jnp.concatenate([ref[0, first_chunk + u] for u in range(n_chunks)], axis=1)

    q_pads, q_augs, o_cs = [], [], []
    for h in range(NSA_KV_HEADS):
        own_half = (lane // HEAD_DIM) == h
        q_pad = jnp.concatenate(
            [jnp.where(own_half, q_all[:, LANES * g:LANES * (g + 1)], jnp.zeros((), BF16)) for g in range(NSA_GROUP)],
            axis=0)

        s_c = _dot_nt(q_pad, ck).reshape(NSA_GROUP, tq, n_cmp) + biasc_ref[NSA_GROUP * h:NSA_GROUP * (h + 1)]
        s_c = jnp.where((cdist >= 0)[None], s_c, neg_inf)
        m_c = jnp.max(s_c, axis=-1, keepdims=True)
        m_c = jnp.where(m_c == neg_inf, 0.0, m_c)
        e_c = jnp.exp(s_c - m_c)
        sum_c = jnp.sum(e_c, axis=-1, keepdims=True)
        p_c = e_c / jnp.where(sum_c > 0.0, sum_c, 1.0)
        o_cs.append(_dot(p_c.reshape(rows, n_cmp).astype(BF16), cv))

        imp = p_c[0]
        for g in range(1, NSA_GROUP):
            imp = imp + p_c[g]
        imp = imp[:, :n_slc] + imp[:, n_slc:]
        if n_slc < LANES:
            imp = jnp.concatenate([imp, jnp.zeros((tq, LANES - n_slc), F32)], axis=1)
        cur = (t0 + row) // SLC_BLOCK
        forced = (lane == 0) | (lane == cur) | (lane == cur - 1)
        score = jnp.where(lane <= cur, jnp.where(forced, jnp.inf, imp), neg_inf)
        sel = _top_blocks_cols(score.T, SLC_TOPN).T
        neg_mask = (sel - 1.0).astype(BF16)
        q_pads.append(q_pad)
        q_augs.append(jnp.concatenate([q_pad, jnp.concatenate([neg_mask] * NSA_GROUP, axis=0)], axis=1))

    def slc_tile(j, carries, is_near):
        k0 = pl.multiple_of(j * kt, kt)
        k_aug = jnp.concatenate([ks_ref[0, pl.ds(k0, kt), :], e_ref[pl.ds(k0, kt), :]], axis=1)
        v_t = value_blocks(vst_ref, j * chunks, chunks)
        logits = [_dot_nt(k_aug, q_augs[h]) for h in range(NSA_KV_HEADS)]
        if is_near:
            causal = key_minus_query(kt) <= t0 - k0
            logits = [jnp.where(causal, s + band_bias(j * chunks, chunks, h), MASK_VALUE)
                      for h, s in enumerate(logits)]
        m_new = [jnp.maximum(carries[h][0], jnp.max(logits[h], axis=0, keepdims=True)) for h in range(NSA_KV_HEADS)]
        probs = [jnp.exp(logits[h] - m_new[h]).astype(BF16) for h in range(NSA_KV_HEADS)]
        return tuple((m_new[h], jnp.exp(carries[h][0] - m_new[h]) * carries[h][1] + _dot(with_ones(v_t, h), probs[h]))
                     for h in range(NSA_KV_HEADS))

    n_far = jnp.maximum(qb - (near - 1), 0) // chunks
    carries = lax.fori_loop(0, n_far, functools.partial(slc_tile, is_near=False), (carry0,) * NSA_KV_HEADS)
    carries = lax.fori_loop(n_far, qb // chunks + 1, functools.partial(slc_tile, is_near=True), carries)

    w0 = pl.multiple_of(jnp.maximum(t0 - WINDOW, 0), tq)
    k_win = kw_ref[0, pl.ds(w0, win_keys), :]
    v_win = value_blocks(vwt_ref, w0 // tq, win_tiles + 1)
    w_dist = (t0 - w0) - key_minus_query(win_keys)
    in_window = (w_dist >= 0) & (w_dist < WINDOW)
    gates_t = gates.T

    for h in range(NSA_KV_HEADS):
        sum_row = HEAD_DIM * (1 - h)
        o_c = o_cs[h]
        acc_s = carries[h][1]
        o_s = acc_s / acc_s[sum_row:sum_row + 1, :]

        s_w = _dot_nt(k_win, q_pads[h]) + band_bias(w0 // tq, win_tiles + 1, h)
        s_w = jnp.where(in_window, s_w, MASK_VALUE)
        p_w = jnp.exp(s_w - jnp.max(s_w, axis=0, keepdims=True))
        acc_w = _dot(with_ones(v_win, h), p_w.astype(BF16))
        o_w = acc_w / acc_w[sum_row:sum_row + 1, :]

        mixed = []
        for g in range(NSA_GROUP):
            hd = NSA_GROUP * h + g
            rs = slice(g * tq, (g + 1) * tq)
            key_major = (gates_t[NSA_HEADS + hd:NSA_HEADS + hd + 1, :] * o_s[:, rs]
                         + gates_t[2 * NSA_HEADS + hd:2 * NSA_HEADS + hd + 1, :] * o_w[:, rs])
            mixed.append(gates[:, hd:hd + 1] * o_c[rs] + key_major.T)
        for pair in range(NSA_GROUP // 2):
            a, c = mixed[2 * pair], mixed[2 * pair + 1]
            if h == 0:
                both = jnp.where(lane < HEAD_DIM, a, pltpu.roll(c, HEAD_DIM, 1))
            else:
                both = jnp.where(lane < HEAD_DIM, pltpu.roll(a, HEAD_DIM, 1), c)
            col = (NSA_GROUP * h + 2 * pair) * HEAD_DIM
            y_ref[0, :, col:col + LANES] = (both * _silu(z_ref[0, :, col:col + LANES])).astype(BF16)


def _nsa_prompt(rel_bias, q, ck, cv, ks, vs_t, kw, vw_t, e_mat, gates, z):
    b, t, _ = q.shape
    tq = Q_TILE
    n_cmp = ck.shape[1]
    assert t % SLC_KEYS == 0 and SLC_KEYS % tq == 0 and WINDOW % tq == 0 and t >= WINDOW + tq
    assert t // SLC_BLOCK <= LANES
    _, _, n_band = _band_geometry(tq)
    tile = lambda width: pl.BlockSpec((1, tq, width), lambda i, j: (j, i, 0))
    seq = lambda rows: pl.BlockSpec((1, rows, LANES), lambda i, j: (j, 0, 0))
    seq_t = pl.BlockSpec((1, t // LANES, LANES, LANES), lambda i, j: (j, 0, 0, 0))
    assert tq == LANES
    return pl.pallas_call(
        _nsa_prompt_kernel,
        grid=(t // tq, b),
        in_specs=[pl.BlockSpec(memory_space=pltpu.SMEM), tile(NSA_WIDTH), seq(n_cmp), seq(n_cmp),
                  seq(t), seq_t, seq(t), seq_t, pl.BlockSpec((t, LANES), lambda i, j: (0, 0)),
                  tile(LANES), tile(NSA_WIDTH)],
        out_specs=tile(NSA_WIDTH),
        out_shape=jax.ShapeDtypeStruct((b, t, NSA_WIDTH), BF16),
        scratch_shapes=[pltpu.VMEM((NSA_HEADS, tq, n_cmp), F32),
                        pltpu.VMEM((n_band, NSA_HEADS, tq, tq), F32)],
        compiler_params=pltpu.CompilerParams(dimension_semantics=("arbitrary", "arbitrary"),
                                             vmem_limit_bytes=VMEM_LIMIT),
        name="nsa_prompt",
    )(rel_bias, q, ck, cv, ks, vs_t, kw, vw_t, e_mat, gates, z)


def _nsa_sample_kernel(pt_ref, rel_ref, q_ref, ck_ref, cv_ref, pk_ref, pv_ref, e_ref, kwc_ref, vwc_ref, new_ref,
                       g_ref, z_ref, y_ref, kbuf, vbuf, kaug_ref, biasc_ref, biass_ref, biasw_ref, ksem, vsem):
    page_rows = pk_ref.shape[2]
    past = kbuf.shape[2]
    n_cmp = ck_ref.shape[1]
    n_slc = n_cmp // (SLC_BLOCK // CMP_BLOCK)
    win_rows = kwc_ref.shape[2]
    heads = tuple(range(NSA_HEADS))

    @pl.when(pl.program_id(0) == 0)
    def _():
        ccol = lax.broadcasted_iota(jnp.int32, (1, n_cmp), 1)
        cblock = (SLC_BLOCK // CMP_BLOCK) * (ccol % n_slc) + ccol // n_slc
        cdist = past - (CMP_BLOCK * cblock + CMP_BLOCK - 1)
        biasc_ref[...] = jnp.concatenate(_rel_bias_rows(cdist, rel_ref, heads), axis=0)
        sdist = past - lax.broadcasted_iota(jnp.int32, (1, past), 1)
        biass_ref[...] = jnp.concatenate(_rel_bias_rows(sdist, rel_ref, heads), axis=0)
        wdist = win_rows - lax.broadcasted_iota(jnp.int32, (1, win_rows), 1)
        biasw_ref[...] = jnp.concatenate(_rel_bias_rows(wdist, rel_ref, heads), axis=0)
        kaug_ref[KV_WIDTH:, :] = e_ref[...]

    slot = _double_buffered_gather(pt_ref, (pk_ref, pv_ref), (kbuf, vbuf), (ksem, vsem), past // page_rows)

    q8 = q_ref[0]
    q8f = q8.astype(F32)
    bias0 = jnp.concatenate(
        [jnp.full((1, 1), rel_ref[0, hd] - rel_ref[REL_BUCKETS - 1, hd], F32) for hd in heads], axis=0)
    lane = lax.broadcasted_iota(jnp.int32, (NSA_KV_HEADS, LANES), 1)

    s_c = _dot_nt(q8, ck_ref[0]) + biasc_ref[...]
    e_c = jnp.exp(s_c - jnp.max(s_c, axis=1, keepdims=True))
    p_c = e_c / jnp.sum(e_c, axis=1, keepdims=True)
    o_c = _dot(p_c.astype(BF16), cv_ref[0])

    imps = []
    for h in range(NSA_KV_HEADS):
        acc = p_c[NSA_GROUP * h:NSA_GROUP * h + 1]
        for g in range(1, NSA_GROUP):
            acc = acc + p_c[NSA_GROUP * h + g:NSA_GROUP * h + g + 1]
        imps.append(acc)
    imp = jnp.concatenate(imps, axis=0)
    imp = imp[:, :n_slc] + imp[:, n_slc:]
    if n_slc < LANES:
        imp = jnp.concatenate([imp, jnp.zeros((NSA_KV_HEADS, LANES - n_slc), F32)], axis=1)
    forced = (lane == 0) | (lane == n_slc - 1)
    score = jnp.where(lane < n_slc, jnp.where(forced, jnp.inf, imp), -jnp.inf)
    sel = _top_blocks(score, SLC_TOPN - 1)
    neg_mask = (sel - 1.0).astype(BF16)
    neg_rows = jnp.concatenate([neg_mask[h:h + 1] for h in range(NSA_KV_HEADS) for _ in range(NSA_GROUP)], axis=0)
    q_aug = jnp.concatenate([q8, neg_rows], axis=1)

    def with_new_token(s, s_new, v_t, v_new):
        m = jnp.maximum(jnp.max(s, axis=1, keepdims=True), s_new)
        p = jnp.exp(s - m)
        p_new = jnp.exp(s_new - m)
        total = jnp.sum(p, axis=1, keepdims=True) + p_new
        return (_dot_nt(p.astype(BF16), v_t) + p_new.astype(BF16).astype(F32) * v_new.astype(F32)) / total

    kaug_ref[:KV_WIDTH, :] = kbuf[slot].astype(BF16)
    s_s = _dot(q_aug, kaug_ref[...]) + biass_ref[...]
    s_new = jnp.sum(q8f * new_ref[0, 0:1, :].astype(F32), axis=1, keepdims=True) + bias0
    o_s = with_new_token(s_s, s_new, vbuf[slot].astype(BF16), new_ref[0, 1:2, :])

    wcol = lax.broadcasted_iota(jnp.int32, (1, win_rows), 1)
    s_w = _dot(q8, kwc_ref[0].astype(BF16)) + biasw_ref[...]
    s_w = jnp.where(win_rows - wcol < WINDOW, s_w, MASK_VALUE)
    s_new = jnp.sum(q8f * new_ref[0, 2:3, :].astype(F32), axis=1, keepdims=True) + bias0
    o_w = with_new_token(s_w, s_new, vwc_ref[0].astype(BF16), new_ref[0, 3:4, :])

    gates = g_ref[0]
    low = []
    for hd in heads:
        r = (gates[:, hd:hd + 1] * o_c[hd:hd + 1]
             + gates[:, NSA_HEADS + hd:NSA_HEADS + hd + 1] * o_s[hd:hd + 1]
             + gates[:, 2 * NSA_HEADS + hd:2 * NSA_HEADS + hd + 1] * o_w[hd:hd + 1])
        low.append(r if hd < NSA_GROUP else pltpu.roll(r, HEAD_DIM, 1))
    lane1 = lax.broadcasted_iota(jnp.int32, (1, LANES), 1)
    for pair in range(NSA_HEADS // 2):
        both = jnp.where(lane1 < HEAD_DIM, low[2 * pair], pltpu.roll(low[2 * pair + 1], HEAD_DIM, 1))
        cols = slice(LANES * pair, LANES * (pair + 1))
        y_ref[0, :, cols] = (both * _silu(z_ref[0, :, cols])).astype(BF16)


def _nsa_sample(page_table, rel_bias, q8, ck, cv, pool_k, pool_v, e_mat, win_k, win_v, new_rows, gates, z):
    n_seq, n_pages = page_table.shape
    page_rows = pool_k.shape[2]
    past = n_pages * page_rows
    n_cmp = ck.shape[1]
    win_rows = win_k.shape[2]
    assert past % SLC_BLOCK == 0 and past // SLC_BLOCK <= LANES and page_rows % LANES == 0
    hbm = pl.BlockSpec(memory_space=pl.ANY)
    per_seq = lambda r, w: pl.BlockSpec((1, r, w), lambda i, pt: (i, 0, 0))
    return pl.pallas_call(
        _nsa_sample_kernel,
        grid_spec=pltpu.PrefetchScalarGridSpec(
            num_scalar_prefetch=1,
            grid=(n_seq,),
            in_specs=[pl.BlockSpec(memory_space=pltpu.SMEM), per_seq(NSA_HEADS, LANES), per_seq(n_cmp, LANES),
                      per_seq(n_cmp, LANES), hbm, hbm,
                      pl.BlockSpec((LANES, past), lambda i, pt: (0, 0)),
                      per_seq(KV_WIDTH, win_rows), per_seq(KV_WIDTH, win_rows), per_seq(4, LANES),
                      per_seq(1, LANES), per_seq(1, NSA_WIDTH)],
            out_specs=per_seq(1, NSA_WIDTH),
            scratch_shapes=[pltpu.VMEM((2, KV_WIDTH, past), F32), pltpu.VMEM((2, KV_WIDTH, past), F32),
                            pltpu.VMEM((KV_WIDTH + LANES, past), BF16),
                            pltpu.VMEM((NSA_HEADS, n_cmp), F32), pltpu.VMEM((NSA_HEADS, past), F32),
                            pltpu.VMEM((NSA_HEADS, win_rows), F32),
                            pltpu.SemaphoreType.DMA((2,)), pltpu.SemaphoreType.DMA((2,))]),
        out_shape=jax.ShapeDtypeStruct((n_seq, 1, NSA_WIDTH), BF16),
        compiler_params=pltpu.CompilerParams(dimension_semantics=("arbitrary",), vmem_limit_bytes=VMEM_LIMIT),
        name="nsa_sample",
    )(page_table.reshape(-1), rel_bias, q8, ck, cv, pool_k, pool_v, e_mat, win_k, win_v, new_rows, gates, z)


def _group_norm_gate(o, z, gn, seg):
    mu = _segment_sum(o, seg) * (1.0 / RET_VDIM)
    d = o - mu
    var = _segment_sum(d * d, seg) * (1.0 / RET_VDIM)
    return (d * lax.rsqrt(var + EPS) * gn * _silu(z)).astype(BF16)


def _retention_prompt_kernel(q_ref, k_ref, v_ref, z_ref, gn_ref, seg_ref, dec_ref, qdec_ref, kdec_ref, cdec_ref,
                             y_ref, st_ref, state):
    chunk = pl.program_id(1)
    tc = q_ref.shape[1]
    lane = lax.broadcasted_iota(jnp.int32, (tc, LANES), 1)
    srow = lax.broadcasted_iota(jnp.int32, (LANES, LANES), 0)
    scol = lax.broadcasted_iota(jnp.int32, (LANES, LANES), 1)
    same_head = (srow // RET_KDIM) == (scol // RET_VDIM)

    @pl.when(chunk == 0)
    def _():
        state[...] = jnp.zeros_like(state)

    for pair in range(RET_HEADS // 2):
        cols = slice(LANES * pair, LANES * (pair + 1))
        q = q_ref[0, :, cols]
        k = k_ref[0, :, cols]
        v = v_ref[0, :, cols]
        s_old = state[pair]
        cross = _dot(q, s_old.astype(BF16)) * qdec_ref[:, cols]
        halves = []
        for e in range(2):
            qe = jnp.where((lane // RET_KDIM) == e, q, jnp.zeros((), BF16))
            scores = _dot_nt(qe, k) * dec_ref[2 * pair + e]
            halves.append(_dot(scores.astype(BF16), v))
        o = jnp.where(lane < RET_VDIM, halves[0], halves[1]) + cross
        y_ref[0, :, cols] = _group_norm_gate(o, z_ref[0, :, cols], gn_ref[:, cols], seg_ref[...])
        kd_t = (k.astype(F32) * kdec_ref[:, cols]).T.astype(BF16)
        s_new = s_old * cdec_ref[:, cols] + jnp.where(same_head, _dot(kd_t, v), 0.0)
        state[pair] = s_new

    @pl.when(chunk == pl.num_programs(1) - 1)
    def _():
        for head in range(RET_HEADS):
            e = head % 2
            st_ref[0, head] = state[head // 2, RET_KDIM * e:RET_KDIM * (e + 1), RET_VDIM * e:RET_VDIM * (e + 1)]


def _retention_tables(tc):
    log_g = jnp.asarray(RET_GAMMA_LOG, F32)
    i = jnp.arange(tc, dtype=F32)
    diff = i[:, None] - i[None, :]
    decay = jnp.where(diff >= 0, jnp.exp(jnp.maximum(diff, 0.0)[None] * log_g[:, None, None]), 0.0)
    widen = lambda a: jnp.repeat(a, RET_VDIM, axis=-1)
    q_decay = widen(jnp.exp((i[:, None] + 1.0) * log_g[None, :]))
    k_decay = widen(jnp.exp((tc - 1.0 - i)[:, None] * log_g[None, :]))
    c_decay = widen(jnp.exp(tc * log_g)[None, :])
    return decay, q_decay, k_decay, c_decay


def _retention_prompt(q, k, v, z, gn, seg):
    b, t, _ = q.shape
    tc = RET_TILE
    assert t % tc == 0
    decay, q_decay, k_decay, c_decay = _retention_tables(tc)
    tile = pl.BlockSpec((1, tc, RET_WIDTH), lambda i, j: (i, j, 0))
    const = lambda a: pl.BlockSpec(a.shape, lambda i, j: (0,) * a.ndim)
    return pl.pallas_call(
        _retention_prompt_kernel,
        grid=(b, t // tc),
        in_specs=[tile, tile, tile, tile, const(gn), const(seg), const(decay), const(q_decay), const(k_decay),
                  const(c_decay)],
        out_specs=[tile, pl.BlockSpec((1, RET_HEADS, RET_KDIM, RET_VDIM), lambda i, j: (i, 0, 0, 0))],
        out_shape=[jax.ShapeDtypeStruct((b, t, RET_WIDTH), BF16),
                   jax.ShapeDtypeStruct((b, RET_HEADS, RET_KDIM, RET_VDIM), F32)],
        scratch_shapes=[pltpu.VMEM((RET_HEADS // 2, LANES, LANES), F32)],
        compiler_params=pltpu.CompilerParams(dimension_semantics=("arbitrary", "arbitrary"),
                                             vmem_limit_bytes=VMEM_LIMIT),
        name="retention_prompt",
    )(q, k, v, z, gn, seg, decay, q_decay, k_decay, c_decay)


def _retention_sample_kernel(qt_ref, kt_ref, v_ref, z_ref, gn_ref, gam_ref, st_ref, y_ref, so_ref):
    n = v_ref.shape[1]
    qt = qt_ref[0]
    kt = kt_ref[0]
    gam = gam_ref[...]
    qk = jnp.sum(qt * kt, axis=1)
    for s in range(n):
        st = st_ref[s]
        vs = v_ref[0, s]
        cross = jnp.sum(qt[:, :, s:s + 1] * st, axis=1)
        so_ref[s] = st * gam[:, :, None] + kt[:, :, s:s + 1] * vs[:, None, :]
        o = qk[:, s:s + 1] * vs + gam * cross
        mu = jnp.mean(o, axis=-1, keepdims=True)
        d = o - mu
        var = jnp.mean(d * d, axis=-1, keepdims=True)
        y_ref[0, s] = (d * lax.rsqrt(var + EPS) * gn_ref[...] * _silu(z_ref[0, s])).astype(BF16)


def _retention_sample(q, k, v, z, gn, state):
    n_seq = q.shape[0]
    n = min(RET_SAMPLE_SEQS, n_seq)
    assert n_seq % n == 0
    steps = n_seq // n
    to_cols = lambda a: a.astype(F32).reshape(steps, n, RET_HEADS, RET_KDIM).transpose(0, 2, 3, 1)
    to_rows = lambda a: a.astype(F32).reshape(steps, n, RET_HEADS, RET_VDIM)
    gam = jnp.exp(jnp.asarray(RET_GAMMA_LOG, F32)).reshape(RET_HEADS, 1)
    cols = pl.BlockSpec((1, RET_HEADS, RET_KDIM, n), lambda i: (i, 0, 0, 0))
    rws = pl.BlockSpec((1, n, RET_HEADS, RET_VDIM), lambda i: (i, 0, 0, 0))
    st = pl.BlockSpec((n, RET_HEADS, RET_KDIM, RET_VDIM), lambda i: (i, 0, 0, 0))
    y, new_state = pl.pallas_call(
        _retention_sample_kernel,
        grid=(steps,),
        in_specs=[cols, cols, rws, rws, pl.BlockSpec((RET_HEADS, RET_VDIM), lambda i: (0, 0)),
                  pl.BlockSpec((RET_HEADS, 1), lambda i: (0, 0)), st],
        out_specs=[rws, st],
        out_shape=[jax.ShapeDtypeStruct((steps, n, RET_HEADS, RET_VDIM), BF16),
                   jax.ShapeDtypeStruct(state.shape, F32)],
        compiler_params=pltpu.CompilerParams(dimension_semantics=("parallel",), vmem_limit_bytes=VMEM_LIMIT),
        name="retention_sample",
    )(to_cols(q), to_cols(k), to_rows(v), to_rows(z), gn.reshape(RET_HEADS, RET_VDIM), gam, state)
    return y.reshape(n_seq, RET_WIDTH), new_state


def _out_kernel(x_ref, yn_ref, yr_ref, w_ref, o_ref):
    y = jnp.concatenate([yn_ref[0], yr_ref[0]], axis=1)
    o_ref[0] = x_ref[0] + _dot(y, w_ref[...])


def _out_project(x, y_nsa, y_ret, w):
    nb, rows, d = x.shape
    tm = min(PROJ_ROWS, rows)
    spec = lambda width: pl.BlockSpec((1, tm, width), lambda i, j: (i, j, 0))
    return pl.pallas_call(
        _out_kernel,
        grid=(nb, rows // tm),
        in_specs=[spec(d), spec(NSA_WIDTH), spec(RET_WIDTH), pl.BlockSpec(w.shape, lambda i, j: (0, 0))],
        out_specs=spec(d),
        out_shape=jax.ShapeDtypeStruct(x.shape, F32),
        compiler_params=pltpu.CompilerParams(dimension_semantics=("parallel", "parallel"),
                                             vmem_limit_bytes=VMEM_LIMIT),
        name="output_projection",
    )(x, y_nsa, y_ret, w)


def _rotary_tables(pos):
    half = RET_KDIM // 2
    inv = ROPE_BASE ** (-jnp.arange(half, dtype=F32) / half)
    ang = pos.astype(F32)[:, None] * inv[None, :]
    cos, sin = jnp.cos(ang), jnp.sin(ang)
    reps = LANES // RET_KDIM
    return jnp.tile(jnp.concatenate([cos, cos], axis=1), (1, reps)), jnp.tile(jnp.concatenate([-sin, sin], axis=1), (1, reps))


def _arrange_w_in(w_in):
    d = w_in.shape[0]
    parts = jnp.split(w_in, np.cumsum(SPLIT_WIDTHS)[:-1].tolist(), axis=1)
    order = [hd for g in range(NSA_GROUP) for hd in (g, NSA_GROUP + g)]
    q = parts[0].reshape(d, NSA_HEADS, HEAD_DIM)[:, order].reshape(d, NSA_WIDTH)
    gates = parts[7].reshape(d, NSA_HEADS, 3).transpose(0, 2, 1).reshape(d, 3 * NSA_HEADS)
    gates = jnp.pad(gates, ((0, 0), (0, LANES - 3 * NSA_HEADS)))
    w = jnp.concatenate([q] + parts[1:7] + parts[8:] + [gates], axis=1)
    assert w.shape[1] == COL_END
    return w.astype(BF16)


def _compress_weights(pe, w1, w2):
    w1r = w1.reshape(CMP_BLOCK, HEAD_DIM, HEAD_DIM)
    w1_both = jnp.zeros((CMP_BLOCK, NSA_KV_HEADS, HEAD_DIM, NSA_KV_HEADS, HEAD_DIM), F32)
    w2_both = jnp.zeros((NSA_KV_HEADS, HEAD_DIM, NSA_KV_HEADS, HEAD_DIM), F32)
    for h in range(NSA_KV_HEADS):
        w1_both = w1_both.at[:, h, :, h, :].set(w1r)
        w2_both = w2_both.at[h, :, h, :].set(w2)
    return (jnp.tile(pe, (1, NSA_KV_HEADS)), w1_both.reshape(CMP_BLOCK * KV_WIDTH, KV_WIDTH).astype(BF16),
            w2_both.reshape(KV_WIDTH, KV_WIDTH).astype(BF16))


def _even_odd(c):
    n, nc, w = c.shape
    ratio = SLC_BLOCK // CMP_BLOCK
    return c.reshape(n, nc // ratio, ratio, w).transpose(0, 2, 1, 3).reshape(n, nc, w)


def _block_membership(n_keys):
    k = jnp.arange(n_keys)[:, None] // SLC_BLOCK
    return jnp.where(k == jnp.arange(LANES)[None, :], -MASK_VALUE, 0.0).astype(BF16)


def kernel(x_prompt, x_sample, cache_cmp_k, cache_cmp_v, cache_slc_k, cache_slc_v, cache_win_k, cache_win_v,
           state_ret, page_table, norm_g, w_in, q_norm_g, k_norm_g, cmp_pe_k, cmp_w1_k, cmp_w2_k, cmp_pe_v,
           cmp_w1_v, cmp_w2_v, rel_bias, ret_gn_g, w_out):
    b, t, d = x_prompt.shape
    n_seq, dec_len, _ = x_sample.shape
    n_pages = page_table.shape[1]
    page_rows = cache_cmp_k.shape[1]
    past = n_pages * page_rows
    assert dec_len == 1 and past % CMP_BLOCK == 0
    kv4 = lambda a: a.reshape(a.shape[0], a.shape[1], NSA_KV_HEADS, HEAD_DIM)

    w = _arrange_w_in(w_in)
    w_o = w_out.astype(BF16)
    ng = norm_g.reshape(1, d)
    qg = (jnp.tile(q_norm_g, LANES // HEAD_DIM) * ATTN_SCALE).reshape(1, LANES)
    kg = jnp.tile(k_norm_g, LANES // HEAD_DIM).reshape(1, LANES)
    gn = ret_gn_g.reshape(1, RET_WIDTH)
    lane = np.arange(LANES)
    seg = jnp.asarray(lane[:, None] // HEAD_DIM == lane[None, :] // HEAD_DIM, BF16)
    cw = _compress_weights(cmp_pe_k, cmp_w1_k, cmp_w2_k) + _compress_weights(cmp_pe_v, cmp_w1_v, cmp_w2_v)

    cos, sin = _rotary_tables(jnp.arange(t))
    (q, kc, vc, ks, vs, kw, vw, ksb, _, kwb, _, gates, zn, qr, kr, vr, zr, vs_t, vw_t) = _project(
        x_prompt, cos, sin, ng, w, qg, kg, seg)
    ck, cv = _compress_prompt(kc, vc, cw, kg, seg)
    y_nsa = _nsa_prompt(rel_bias, q, _even_odd(ck), _even_odd(cv), ksb, vs_t, kwb, vw_t, _block_membership(t), gates, zn)
    y_ret, p_ret = _retention_prompt(qr, kr, vr, zr, gn, seg)
    y_prompt = _out_project(x_prompt, y_nsa, y_ret, w_o)
    keep = min(WINDOW, t)
    prompt_out = (y_prompt, kv4(kc), kv4(vc), kv4(ks), kv4(vs), kv4(kw[:, t - keep:]), kv4(vw[:, t - keep:]), p_ret)

    cos, sin = _rotary_tables(jnp.full((n_seq,), past))
    xs = x_sample.reshape(1, n_seq, d)
    (q, kc, vc, ks, vs, kw, vw, ksb, vsb, kwb, vwb, gates, zn, qr, kr, vr, zr) = [
        a[0] for a in _project(xs, cos, sin, ng, w, qg, kg, seg)[:17]]
    chan = lambda a: jnp.transpose(a, (0, 2, 3, 1)).reshape(a.shape[0], KV_WIDTH, a.shape[1])
    ck, cv = _compress_sample(page_table, chan(cache_cmp_k), chan(cache_cmp_v), cw, kg, seg)
    half = jnp.asarray(lane[None, :] // HEAD_DIM == (np.arange(NSA_HEADS) // NSA_GROUP)[:, None])
    q8 = jnp.where(half[None], jnp.tile(q.reshape(n_seq, NSA_GROUP, LANES), (1, NSA_KV_HEADS, 1)), jnp.zeros((), BF16))
    new_rows = jnp.stack([ksb, vsb, kwb, vwb], axis=1)
    y_nsa = _nsa_sample(page_table, rel_bias, q8, _even_odd(ck), _even_odd(cv), chan(cache_slc_k), chan(cache_slc_v),
                        _block_membership(past).T, chan(cache_win_k), chan(cache_win_v), new_rows,
                        gates.reshape(n_seq, 1, LANES), zn.reshape(n_seq, 1, NSA_WIDTH))
    y_ret, s_ret = _retention_sample(qr, kr, vr, zr, gn, state_ret)
    y_sample = _out_project(xs, y_nsa.reshape(1, n_seq, NSA_WIDTH), y_ret.reshape(1, n_seq, RET_WIDTH), w_o)
    keep = min(WINDOW, cache_win_k.shape[1] + 1)
    new4 = lambda a: a.reshape(n_seq, 1, NSA_KV_HEADS, HEAD_DIM)
    s_win_k = jnp.concatenate([cache_win_k, new4(kw)], axis=1)[:, -keep:]
    s_win_v = jnp.concatenate([cache_win_v, new4(vw)], axis=1)[:, -keep:]
    sample_out = (y_sample.reshape(n_seq, 1, d), new4(kc), new4(vc), new4(ks), new4(vs), s_win_k, s_win_v, s_ret)

    return (prompt_out[0], sample_out[0]) + prompt_out[1:] + sample_out[1:]
```

```python
import functools

import numpy as np
import jax
import jax.numpy as jnp
from jax import lax
from jax.experimental import pallas as pl
from jax.experimental.pallas import tpu as pltpu

F32, BF16 = jnp.float32, jnp.bfloat16

NSA_HEADS = 8
NSA_KV_HEADS = 2
HEAD_DIM = 64
NSA_GROUP = NSA_HEADS // NSA_KV_HEADS
NSA_WIDTH = NSA_HEADS * HEAD_DIM
KV_WIDTH = NSA_KV_HEADS * HEAD_DIM
CMP_BLOCK = 32
SLC_BLOCK = 64
SLC_TOPN = 16
WINDOW = 512
ATTN_SCALE = HEAD_DIM ** -0.5
RET_HEADS = 8
RET_KDIM = 64
RET_VDIM = 64
RET_WIDTH = RET_HEADS * RET_VDIM
ROPE_BASE = 10000.0
REL_BUCKETS = 32
REL_MAX_DIST = 1024
EPS = 1e-6
SPLIT_WIDTHS = (NSA_WIDTH, KV_WIDTH, KV_WIDTH, KV_WIDTH, KV_WIDTH, KV_WIDTH, KV_WIDTH,
                3 * NSA_HEADS, NSA_WIDTH, RET_HEADS * RET_KDIM, RET_HEADS * RET_KDIM, RET_WIDTH, RET_WIDTH)

LANES = 128
VMEM_LIMIT = 56 * 1024 * 1024

PROJ_ROWS = 256
Q_TILE = 128
SLC_KEYS = 512
RET_TILE = 256
RET_SAMPLE_SEQS = 16
CMP_PITCH = CMP_BLOCK + 4

MASK_VALUE = -float(2 ** 30)

MXU_COLS = 256
COL_Q = 0
COL_KC = COL_Q + NSA_WIDTH
COL_VC = COL_KC + LANES
COL_KS = COL_VC + LANES
COL_VS = COL_KS + LANES
COL_KW = COL_VS + LANES
COL_VW = COL_KW + LANES
COL_ZN = COL_VW + LANES
COL_QR = COL_ZN + NSA_WIDTH
COL_KR = COL_QR + RET_WIDTH
COL_VR = COL_KR + RET_WIDTH
COL_ZR = COL_VR + RET_WIDTH
COL_G = COL_ZR + RET_WIDTH
COL_END = COL_G + LANES


def _bucket_lower_bounds():
    exact = REL_BUCKETS // 2
    ratio = REL_MAX_DIST // exact
    lows = list(range(exact))
    n = exact
    for k in range(REL_BUCKETS - exact):
        while n ** (REL_BUCKETS - exact) < exact ** (REL_BUCKETS - exact) * ratio ** k:
            n += 1
        lows.append(n)
    return tuple(lows)


BUCKET_LOW = _bucket_lower_bounds()
FAR_DIST = BUCKET_LOW[-1]
RET_GAMMA_LOG = tuple(float(np.log1p(-np.exp2(-5.0 - h))) for h in range(RET_HEADS))


def _dot(a, b):
    return jnp.dot(a, b, preferred_element_type=F32)


def _dot_nt(a, b):
    return lax.dot_general(a, b, (((1,), (1,)), ((), ())), preferred_element_type=F32)


def _segment_sum(v, seg):
    hi = v.astype(BF16)
    lo = (v - hi.astype(F32)).astype(BF16)
    return _dot(hi, seg) + _dot(lo, seg)


def _head_rms(y, g, seg):
    ms = _segment_sum(y * y, seg) * (1.0 / HEAD_DIM)
    return y * lax.rsqrt(ms + EPS) * g


def _silu(x):
    return x * jax.nn.sigmoid(x)


def _rel_bias_rows(dist, rel_ref, heads):
    out = [jnp.full(dist.shape, rel_ref[0, hd], F32) for hd in heads]
    for bkt in range(1, REL_BUCKETS):
        hit = dist >= BUCKET_LOW[bkt]
        out = [jnp.where(hit, rel_ref[bkt, hd], o) for hd, o in zip(heads, out)]
    return [o - rel_ref[REL_BUCKETS - 1, hd] for hd, o in zip(heads, out)]


def _top_blocks(score, n_pick):
    col = lax.broadcasted_iota(jnp.int32, score.shape, 1).astype(F32)
    sel = jnp.zeros(score.shape, F32)
    neg_inf = -jnp.inf
    for _ in range(n_pick):
        m = jnp.max(score, axis=1, keepdims=True)
        first = jnp.min(jnp.where(score == m, col, float(score.shape[1])), axis=1, keepdims=True)
        hit = col == first
        sel = jnp.where(hit & (m > neg_inf), 1.0, sel)
        score = jnp.where(hit, neg_inf, score)
    return sel


def _top_blocks_cols(score, n_pick):
    n_rows = score.shape[0]
    rowi = lax.broadcasted_iota(jnp.int32, score.shape, 0).astype(F32)
    sel = jnp.zeros(score.shape, F32)
    neg_inf = -jnp.inf
    for _ in range(n_pick):
        m = jnp.max(score, axis=0, keepdims=True)
        first = jnp.min(jnp.where(score == m, rowi, float(n_rows)), axis=0, keepdims=True)
        hit = rowi == first
        sel = sel + jnp.where(hit, jnp.where(m > neg_inf, 1.0, 0.0), 0.0)
        score = jnp.where(hit, neg_inf, score)
    return sel


def _proj_kernel(x_ref, cos_ref, sin_ref, ng_ref, w_ref, qg_ref, kg_ref, seg_ref,
                 q_o, kc_o, vc_o, ks_o, vs_o, kw_o, vw_o, ksb_o, vsb_o, kwb_o, vwb_o,
                 g_o, zn_o, qr_o, kr_o, vr_o, zr_o, vst_o, vwt_o):
    x = x_ref[0]
    inv = lax.rsqrt(jnp.mean(x * x, axis=-1, keepdims=True) + EPS)
    xn = (x * inv * ng_ref[...]).astype(BF16)
    seg = seg_ref[...]

    def mm(col, width):
        return _dot(xn, w_ref[:, col:col + width])

    def chunk(y, c):
        return y[:, LANES * c:LANES * (c + 1)]

    y = mm(COL_Q, NSA_WIDTH)
    for c in range(NSA_WIDTH // LANES):
        q_o[0, :, LANES * c:LANES * (c + 1)] = _head_rms(chunk(y, c), qg_ref[...], seg).astype(BF16)
    y = mm(COL_KC, 2 * LANES)
    kc_o[0] = chunk(y, 0)
    vc_o[0] = chunk(y, 1)
    for col, k_o, v_o, kb_o, vb_o, vt_o in ((COL_KS, ks_o, vs_o, ksb_o, vsb_o, vst_o),
                                            (COL_KW, kw_o, vw_o, kwb_o, vwb_o, vwt_o)):
        y = mm(col, 2 * LANES)
        k = _head_rms(chunk(y, 0), kg_ref[...], seg)
        k_o[0] = k
        kb_o[0] = k.astype(BF16)
        v = chunk(y, 1)
        v_o[0] = v
        vb_o[0] = v.astype(BF16)
        for c in range(v.shape[0] // LANES):
            vt_o[0, c] = v[LANES * c:LANES * (c + 1)].T.astype(BF16)
    zn_o[0] = mm(COL_ZN, NSA_WIDTH)

    cos = cos_ref[...]
    sin = sin_ref[...]
    lane = lax.broadcasted_iota(jnp.int32, cos.shape, 1)
    first_half = (lane % HEAD_DIM) < (HEAD_DIM // 2)
    for col, o_ref, scale in ((COL_QR, qr_o, None), (COL_KR, kr_o, RET_KDIM ** -0.5)):
        y = mm(col, RET_WIDTH)
        for c in range(RET_WIDTH // LANES):
            yc = chunk(y, c)
            partner = jnp.where(first_half, pltpu.roll(yc, LANES - HEAD_DIM // 2, 1),
                                pltpu.roll(yc, HEAD_DIM // 2, 1))
            r = yc * cos + partner * sin
            if scale is not None:
                r = r * scale
            o_ref[0, :, LANES * c:LANES * (c + 1)] = r.astype(BF16)
    vr_o[0] = mm(COL_VR, RET_WIDTH).astype(BF16)
    zr_o[0] = mm(COL_ZR, RET_WIDTH)
    g_o[0] = jax.nn.sigmoid(mm(COL_G, LANES))


def _project(x, cos, sin, norm_g, w, qg, kg, seg):
    nb, rows, d = x.shape
    tm = min(PROJ_ROWS, rows)
    assert rows % tm == 0 and tm % LANES == 0
    row_spec = lambda width: pl.BlockSpec((1, tm, width), lambda i, j: (i, j, 0))
    const = lambda shape: pl.BlockSpec(shape, lambda i, j: (0,) * len(shape))
    outs = [(NSA_WIDTH, BF16)] + [(LANES, F32)] * 6 + [(LANES, BF16)] * 4 + [(LANES, F32), (NSA_WIDTH, F32),
            (RET_WIDTH, BF16), (RET_WIDTH, BF16), (RET_WIDTH, BF16), (RET_WIDTH, F32)]
    blocks_spec = pl.BlockSpec((1, tm // LANES, LANES, LANES), lambda i, j: (i, j, 0, 0))
    blocks_shape = jax.ShapeDtypeStruct((nb, rows // LANES, LANES, LANES), BF16)
    return pl.pallas_call(
        _proj_kernel,
        grid=(nb, rows // tm),
        in_specs=[row_spec(d), pl.BlockSpec((tm, LANES), lambda i, j: (j, 0)),
                  pl.BlockSpec((tm, LANES), lambda i, j: (j, 0)), const((1, d)), const(w.shape),
                  const((1, LANES)), const((1, LANES)), const((LANES, LANES))],
        out_specs=[row_spec(wd) for wd, _ in outs] + [blocks_spec] * 2,
        out_shape=[jax.ShapeDtypeStruct((nb, rows, wd), dt) for wd, dt in outs] + [blocks_shape] * 2,
        compiler_params=pltpu.CompilerParams(dimension_semantics=("parallel", "parallel"),
                                             vmem_limit_bytes=VMEM_LIMIT),
        name="input_projection",
    )(x, cos, sin, norm_g, w, qg, kg, seg)


def _compress_rows(buf_ref, pe_ref, w1_ref, w2_ref, n_blk, pitch=CMP_BLOCK):
    parts = []
    for i in range(CMP_BLOCK):
        rows = buf_ref[pl.ds(i, n_blk, stride=pitch), :] + pe_ref[i:i + 1, :]
        parts.append(rows.astype(BF16))
    flat = jnp.concatenate(parts, axis=1)
    hidden = _silu(_dot(flat, w1_ref[...]))
    return _dot(hidden.astype(BF16), w2_ref[...])


def _compress_kernel(kc_ref, vc_ref, pek_ref, w1k_ref, w2k_ref, pev_ref, w1v_ref, w2v_ref, kg_ref, seg_ref,
                     ck_o, cv_o):
    n_blk = ck_o.shape[1]
    ck = _compress_rows(kc_ref.at[0], pek_ref, w1k_ref, w2k_ref, n_blk)
    ck_o[0] = _head_rms(ck, kg_ref[...], seg_ref[...]).astype(BF16)
    cv_o[0] = _compress_rows(vc_ref.at[0], pev_ref, w1v_ref, w2v_ref, n_blk).astype(BF16)


def _compress_prompt(kc, vc, cw, kg, seg):
    b, t, _ = kc.shape
    n_blk = t // CMP_BLOCK
    seq = pl.BlockSpec((1, t, LANES), lambda i: (i, 0, 0))
    const = lambda a: pl.BlockSpec(a.shape, lambda i: (0,) * a.ndim)
    out = pl.BlockSpec((1, n_blk, LANES), lambda i: (i, 0, 0))
    return pl.pallas_call(
        _compress_kernel,
        grid=(b,),
        in_specs=[seq, seq] + [const(a) for a in cw] + [const(kg), const(seg)],
        out_specs=[out, out],
        out_shape=[jax.ShapeDtypeStruct((b, n_blk, LANES), BF16)] * 2,
        compiler_params=pltpu.CompilerParams(dimension_semantics=("parallel",), vmem_limit_bytes=VMEM_LIMIT),
        name="compress_prompt",
    )(kc, vc, *cw, kg, seg)


def _page_copy(pt_ref, pool_ref, buf_ref, sem, seq, page, n_pages):
    rows = pool_ref.shape[2]
    return pltpu.make_async_copy(pool_ref.at[pt_ref[seq * n_pages + page]],
                                 buf_ref.at[:, pl.ds(page * rows, rows)], sem)


def _gather_pages(pt_ref, pools, bufs, sems, seq, slot, n_pages, wait):
    for pool_ref, buf_ref, sem in zip(pools, bufs, sems):
        for page in range(n_pages):
            copy = _page_copy(pt_ref, pool_ref, buf_ref.at[slot], sem.at[slot], seq, page, n_pages)
            if wait:
                copy.wait()
            else:
                copy.start()


def _double_buffered_gather(pt_ref, pools, bufs, sems, n_pages):
    seq = pl.program_id(0)
    slot = seq % 2

    @pl.when(seq == 0)
    def _():
        _gather_pages(pt_ref, pools, bufs, sems, seq, slot, n_pages, wait=False)

    @pl.when(seq + 1 < pl.num_programs(0))
    def _():
        _gather_pages(pt_ref, pools, bufs, sems, seq + 1, 1 - slot, n_pages, wait=False)

    _gather_pages(pt_ref, pools, bufs, sems, seq, slot, n_pages, wait=True)
    return slot


def _compress_sample_kernel(pt_ref, pk_ref, pv_ref, pek_ref, w1k_ref, w2k_ref, pev_ref, w1v_ref, w2v_ref,
                            kg_ref, seg_ref, ck_o, cv_o, kbuf, vbuf, rows_ref, ksem, vsem):
    page_rows = pk_ref.shape[2]
    n_pages = kbuf.shape[2] // page_rows
    n_blk = ck_o.shape[1]
    slot = _double_buffered_gather(pt_ref, (pk_ref, pv_ref), (kbuf, vbuf), (ksem, vsem), n_pages)

    blocks_per_page = page_rows // CMP_BLOCK

    def token_major(buf):
        for page in range(n_pages):
            tokens = buf[slot, :, page * page_rows:(page + 1) * page_rows].T
            for c in range(blocks_per_page):
                r0 = (page * blocks_per_page + c) * CMP_PITCH
                rows_ref[r0:r0 + CMP_BLOCK, :] = tokens[c * CMP_BLOCK:(c + 1) * CMP_BLOCK]

    token_major(kbuf)
    ck = _compress_rows(rows_ref, pek_ref, w1k_ref, w2k_ref, n_blk, CMP_PITCH)
    ck_o[0] = _head_rms(ck, kg_ref[...], seg_ref[...]).astype(BF16)
    token_major(vbuf)
    cv_o[0] = _compress_rows(rows_ref, pev_ref, w1v_ref, w2v_ref, n_blk, CMP_PITCH).astype(BF16)


def _compress_sample(page_table, pool_k, pool_v, cw, kg, seg):
    n_seq, n_pages = page_table.shape
    page_rows = pool_k.shape[2]
    assert page_rows == LANES
    past = n_pages * page_rows
    n_blk = past // CMP_BLOCK
    hbm = pl.BlockSpec(memory_space=pl.ANY)
    const = lambda a: pl.BlockSpec(a.shape, lambda i, pt: (0,) * a.ndim)
    out = pl.BlockSpec((1, n_blk, LANES), lambda i, pt: (i, 0, 0))
    return pl.pallas_call(
        _compress_sample_kernel,
        grid_spec=pltpu.PrefetchScalarGridSpec(
            num_scalar_prefetch=1,
            grid=(n_seq,),
            in_specs=[hbm, hbm] + [const(a) for a in cw] + [const(kg), const(seg)],
            out_specs=[out, out],
            scratch_shapes=[pltpu.VMEM((2, KV_WIDTH, past), F32), pltpu.VMEM((2, KV_WIDTH, past), F32),
                            pltpu.VMEM((n_blk * CMP_PITCH, KV_WIDTH), F32),
                            pltpu.SemaphoreType.DMA((2,)), pltpu.SemaphoreType.DMA((2,))]),
        out_shape=[jax.ShapeDtypeStruct((n_seq, n_blk, LANES), BF16)] * 2,
        compiler_params=pltpu.CompilerParams(dimension_semantics=("arbitrary",), vmem_limit_bytes=VMEM_LIMIT),
        name="compress_sample",
    )(page_table.reshape(-1), pool_k, pool_v, *cw, kg, seg)


def _band_geometry(tq):
    near = -(-(FAR_DIST + tq - 1) // tq)
    rel_max = near + SLC_KEYS // tq - 2
    rel_min = -max(WINDOW // tq, SLC_KEYS // tq - 1)
    return near, rel_max, rel_max - rel_min + 1


def _nsa_prompt_kernel(rel_ref, q_ref, ck_ref, cv_ref, ks_ref, vst_ref, kw_ref, vwt_ref, e_ref, g_ref, z_ref,
                       y_ref, biasc_ref, band_ref, logit_ref, prob_ref, acc_ref, winb_ref):
    tq = q_ref.shape[1]
    n_cmp = ck_ref.shape[1]
    n_slc = n_cmp // (SLC_BLOCK // CMP_BLOCK)
    rows = NSA_GROUP * tq
    kt = SLC_KEYS
    chunks = kt // tq
    near, rel_max, n_band = _band_geometry(tq)
    win_tiles = WINDOW // tq
    win_keys = WINDOW + tq
    qb = pl.program_id(0)
    b = pl.program_id(1)
    t0 = qb * tq
    all_heads = tuple(range(NSA_HEADS))

    lane = lax.broadcasted_iota(jnp.int32, (tq, LANES), 1)
    row = lax.broadcasted_iota(jnp.int32, (tq, LANES), 0)

    @pl.when((qb == 0) & (b == 0))
    def _():
        for c in range(n_band):
            dist = lane - row + tq * (rel_max - c)
            for hd, bias in zip(all_heads, _rel_bias_rows(dist, rel_ref, all_heads)):
                band_ref[c, hd] = bias

    def band_bias(first_chunk, n_chunks, h):
        base = rel_max - qb + first_chunk
        return jnp.concatenate(
            [jnp.concatenate([band_ref[base + u, NSA_GROUP * h + g] for g in range(NSA_GROUP)], axis=1)
             for u in range(n_chunks)], axis=0)

    ccol = lax.broadcasted_iota(jnp.int32, (tq, n_cmp), 1)
    crow = lax.broadcasted_iota(jnp.int32, (tq, n_cmp), 0)
    cblock = (SLC_BLOCK // CMP_BLOCK) * (ccol % n_slc) + ccol // n_slc
    cdist = t0 + crow - (CMP_BLOCK * cblock + CMP_BLOCK - 1)

    @pl.when(b == 0)
    def _():
        for hd, bias in zip(all_heads, _rel_bias_rows(cdist, rel_ref, all_heads)):
            biasc_ref[hd] = bias

    q_all = q_ref[0]
    gates = g_ref[0]
    ck = ck_ref[0]
    cv = cv_ref[0]
    neg_inf = -jnp.inf
    def key_minus_query(n_keys):
        return (lax.broadcasted_iota(jnp.int32, (n_keys, rows), 0)
                - lax.broadcasted_iota(jnp.int32, (n_keys, rows), 1) % tq)

    ones = jnp.ones((), BF16)

    def with_ones(v_t, h):
        own = (lax.broadcasted_iota(jnp.int32, v_t.shape, 0) // HEAD_DIM) == h
        return jnp.where(own, v_t, ones)

    def softmax_step(carry, s, v_t):
        m, acc = carry
        m_new = jnp.maximum(m, jnp.max(s, axis=0, keepdims=True))
        p = jnp.exp(s - m_new)
        return m_new, jnp.exp(m - m_new) * acc + _dot(v_t, p.astype(BF16))

    def value_blocks(ref, first_chunk, n_chunks):
        return jnp.concatenate([ref[0, first_chunk + u] for u in range(n_chunks)], axis=1)

    q_pads, q_augs, o_cs = [], [], []
    for h in range(NSA_KV_HEADS):
        own_half = (lane // HEAD_DIM) == h
        q_pad = jnp.concatenate(
            [jnp.where(own_half, q_all[:, LANES * g:LANES * (g + 1)], jnp.zeros((), BF16)) for g in range(NSA_GROUP)],
            axis=0)

        s_c = _dot_nt(q_pad, ck).reshape(NSA_GROUP, tq, n_cmp) + biasc_ref[NSA_GROUP * h:NSA_GROUP * (h + 1)]
        s_c = jnp.where((cdist >= 0)[None], s_c, neg_inf)
        m_c = jnp.max(s_c, axis=-1, keepdims=True)
        m_c = jnp.where(m_c == neg_inf, 0.0, m_c)
        e_c = jnp.exp(s_c - m_c)
        sum_c = jnp.sum(e_c, axis=-1, keepdims=True)
        p_c = e_c / jnp.where(sum_c > 0.0, sum_c, 1.0)
        o_cs.append(_dot(p_c.reshape(rows, n_cmp).astype(BF16), cv))

        imp = p_c[0]
        for g in range(1, NSA_GROUP):
            imp = imp + p_c[g]
        imp = imp[:, :n_slc] + imp[:, n_slc:]
        if n_slc < LANES:
            imp = jnp.concatenate([imp, jnp.zeros((tq, LANES - n_slc), F32)], axis=1)
        cur = (t0 + row) // SLC_BLOCK
        forced = (lane == 0) | (lane == cur) | (lane == cur - 1)
        score = jnp.where(lane <= cur, jnp.where(forced, jnp.inf, imp), neg_inf)
        sel = _top_blocks_cols(score.T, SLC_TOPN).T
        neg_mask = (sel - 1.0).astype(BF16)
        q_pads.append(q_pad)
        q_augs.append(jnp.concatenate([q_pad, jnp.concatenate([neg_mask] * NSA_GROUP, axis=0)], axis=1))

    kv_heads = range(NSA_KV_HEADS)
    last = qb // chunks

    def tile_logits(j):
        k0 = pl.multiple_of(j * kt, kt)
        k_aug = jnp.concatenate([ks_ref[0, pl.ds(k0, kt), :], e_ref[pl.ds(k0, kt), :]], axis=1)
        return [_dot_nt(k_aug, q_augs[h]) for h in kv_heads]

    def tile_pv(j):
        v_t = value_blocks(vst_ref, j * chunks, chunks)
        return [_dot(with_ones(v_t, h), prob_ref[h]) for h in kv_heads]

    for h, s in enumerate(tile_logits(0)):
        logit_ref[h] = s
        prob_ref[h] = jnp.zeros((kt, rows), BF16)
        acc_ref[h] = jnp.zeros((KV_WIDTH, rows), F32)

    def slc_tile(j, m_old, biased, final=False):
        pv = tile_pv(jnp.maximum(j - 1, 0))
        logits = [logit_ref[h] for h in kv_heads]
        if biased:
            logits = [s + band_bias(j * chunks, chunks, h) for h, s in enumerate(logits)]
        if final:
            causal = key_minus_query(kt) <= t0 - j * kt
            logits = [jnp.where(causal, s, MASK_VALUE) for s in logits]
        m_new = [jnp.maximum(m_old[h], jnp.max(logits[h], axis=0, keepdims=True)) for h in kv_heads]
        probs = [jnp.exp(logits[h] - m_new[h]).astype(BF16) for h in kv_heads]
        nxt = None if final else tile_logits(j + 1)
        for h in kv_heads:
            acc_ref[h] = jnp.exp(m_old[h] - m_new[h]) * (acc_ref[h] + pv[h])
            prob_ref[h] = probs[h]
            if not final:
                logit_ref[h] = nxt[h]
        return tuple(m_new)

    n_far = jnp.maximum(qb - (near - 1), 0) // chunks
    m_run = (jnp.full((1, rows), neg_inf, F32),) * NSA_KV_HEADS
    m_run = lax.fori_loop(0, n_far, functools.partial(slc_tile, biased=False), m_run)
    m_run = lax.fori_loop(n_far, last, functools.partial(slc_tile, biased=True), m_run)
    slc_tile(last, m_run, biased=True, final=True)
    acc_slc = [acc_ref[h] + pv for h, pv in enumerate(tile_pv(last))]

    w0 = pl.multiple_of(jnp.maximum(t0 - WINDOW, 0), tq)
    k_win = kw_ref[0, pl.ds(w0, win_keys), :]
    v_win = value_blocks(vwt_ref, w0 // tq, win_tiles + 1)
    gates_t = gates.T

    @pl.when((b == 0) & (qb <= win_tiles))
    def _():
        w_dist = (t0 - w0) - key_minus_query(win_keys)
        in_window = (w_dist >= 0) & (w_dist < WINDOW)
        for h in kv_heads:
            winb_ref[h] = jnp.where(in_window, band_bias(w0 // tq, win_tiles + 1, h), MASK_VALUE)

    for h in range(NSA_KV_HEADS):
        sum_row = HEAD_DIM * (1 - h)
        o_c = o_cs[h]
        acc_s = acc_slc[h]
        o_s = acc_s / acc_s[sum_row:sum_row + 1, :]

        s_w = _dot_nt(k_win, q_pads[h]) + winb_ref[h]
        p_w = jnp.exp(s_w - jnp.max(s_w, axis=0, keepdims=True))
        acc_w = _dot(with_ones(v_win, h), p_w.astype(BF16))
        o_w = acc_w / acc_w[sum_row:sum_row + 1, :]

        mixed = []
        for g in range(NSA_GROUP):
            hd = NSA_GROUP * h + g
            rs = slice(g * tq, (g + 1) * tq)
            key_major = (gates_t[NSA_HEADS + hd:NSA_HEADS + hd + 1, :] * o_s[:, rs]
                         + gates_t[2 * NSA_HEADS + hd:2 * NSA_HEADS + hd + 1, :] * o_w[:, rs])
            mixed.append(gates[:, hd:hd + 1] * o_c[rs] + key_major.T)
        for pair in range(NSA_GROUP // 2):
            a, c = mixed[2 * pair], mixed[2 * pair + 1]
            if h == 0:
                both = jnp.where(lane < HEAD_DIM, a, pltpu.roll(c, HEAD_DIM, 1))
            else:
                both = jnp.where(lane < HEAD_DIM, pltpu.roll(a, HEAD_DIM, 1), c)
            col = (NSA_GROUP * h + 2 * pair) * HEAD_DIM
            y_ref[0, :, col:col + LANES] = (both * _silu(z_ref[0, :, col:col + LANES])).astype(BF16)


def _nsa_prompt(rel_bias, q, ck, cv, ks, vs_t, kw, vw_t, e_mat, gates, z):
    b, t, _ = q.shape
    tq = Q_TILE
    n_cmp = ck.shape[1]
    assert t % SLC_KEYS == 0 and SLC_KEYS % tq == 0 and WINDOW % tq == 0 and t >= WINDOW + tq
    assert t // SLC_BLOCK <= LANES
    _, _, n_band = _band_geometry(tq)
    tile = lambda width: pl.BlockSpec((1, tq, width), lambda i, j: (j, i, 0))
    seq = lambda rows: pl.BlockSpec((1, rows, LANES), lambda i, j: (j, 0, 0))
    seq_t = pl.BlockSpec((1, t // LANES, LANES, LANES), lambda i, j: (j, 0, 0, 0))
    assert tq == LANES
    return pl.pallas_call(
        _nsa_prompt_kernel,
        grid=(t // tq, b),
        in_specs=[pl.BlockSpec(memory_space=pltpu.SMEM), tile(NSA_WIDTH), seq(n_cmp), seq(n_cmp),
                  seq(t), seq_t, seq(t), seq_t, pl.BlockSpec((t, LANES), lambda i, j: (0, 0)),
                  tile(LANES), tile(NSA_WIDTH)],
        out_specs=tile(NSA_WIDTH),
        out_shape=jax.ShapeDtypeStruct((b, t, NSA_WIDTH), BF16),
        scratch_shapes=[pltpu.VMEM((NSA_HEADS, tq, n_cmp), F32),
                        pltpu.VMEM((n_band, NSA_HEADS, tq, tq), F32),
                        pltpu.VMEM((NSA_KV_HEADS, SLC_KEYS, NSA_GROUP * tq), F32),
                        pltpu.VMEM((NSA_KV_HEADS, SLC_KEYS, NSA_GROUP * tq), BF16),
                        pltpu.VMEM((NSA_KV_HEADS, KV_WIDTH, NSA_GROUP * tq), F32),
                        pltpu.VMEM((NSA_KV_HEADS, WINDOW + tq, NSA_GROUP * tq), F32)],
        compiler_params=pltpu.CompilerParams(dimension_semantics=("arbitrary", "arbitrary"),
                                             vmem_limit_bytes=VMEM_LIMIT),
        name="nsa_prompt",
    )(rel_bias, q, ck, cv, ks, vs_t, kw, vw_t, e_mat, gates, z)


def _nsa_sample_kernel(pt_ref, rel_ref, q_ref, ck_ref, cv_ref, pk_ref, pv_ref, e_ref, kwc_ref, vwc_ref, new_ref,
                       g_ref, z_ref, y_ref, kbuf, vbuf, kaug_ref, biasc_ref, biass_ref, biasw_ref, ksem, vsem):
    page_rows = pk_ref.shape[2]
    past = kbuf.shape[2]
    n_cmp = ck_ref.shape[1]
    n_slc = n_cmp // (SLC_BLOCK // CMP_BLOCK)
    win_rows = kwc_ref.shape[2]
    heads = tuple(range(NSA_HEADS))

    @pl.when(pl.program_id(0) == 0)
    def _():
        ccol = lax.broadcasted_iota(jnp.int32, (1, n_cmp), 1)
        cblock = (SLC_BLOCK // CMP_BLOCK) * (ccol % n_slc) + ccol // n_slc
        cdist = past - (CMP_BLOCK * cblock + CMP_BLOCK - 1)
        biasc_ref[...] = jnp.concatenate(_rel_bias_rows(cdist, rel_ref, heads), axis=0)
        sdist = past - lax.broadcasted_iota(jnp.int32, (1, past), 1)
        biass_ref[...] = jnp.concatenate(_rel_bias_rows(sdist, rel_ref, heads), axis=0)
        wdist = win_rows - lax.broadcasted_iota(jnp.int32, (1, win_rows), 1)
        biasw_ref[...] = jnp.concatenate(_rel_bias_rows(wdist, rel_ref, heads), axis=0)
        kaug_ref[KV_WIDTH:, :] = e_ref[...]

    slot = _double_buffered_gather(pt_ref, (pk_ref, pv_ref), (kbuf, vbuf), (ksem, vsem), past // page_rows)

    q8 = q_ref[0]
    q8f = q8.astype(F32)
    bias0 = jnp.concatenate(
        [jnp.full((1, 1), rel_ref[0, hd] - rel_ref[REL_BUCKETS - 1, hd], F32) for hd in heads], axis=0)
    lane = lax.broadcasted_iota(jnp.int32, (NSA_KV_HEADS, LANES), 1)

    s_c = _dot_nt(q8, ck_ref[0]) + biasc_ref[...]
    e_c = jnp.exp(s_c - jnp.max(s_c, axis=1, keepdims=True))
    p_c = e_c / jnp.sum(e_c, axis=1, keepdims=True)
    o_c = _dot(p_c.astype(BF16), cv_ref[0])

    imps = []
    for h in range(NSA_KV_HEADS):
        acc = p_c[NSA_GROUP * h:NSA_GROUP * h + 1]
        for g in range(1, NSA_GROUP):
            acc = acc + p_c[NSA_GROUP * h + g:NSA_GROUP * h + g + 1]
        imps.append(acc)
    imp = jnp.concatenate(imps, axis=0)
    imp = imp[:, :n_slc] + imp[:, n_slc:]
    if n_slc < LANES:
        imp = jnp.concatenate([imp, jnp.zeros((NSA_KV_HEADS, LANES - n_slc), F32)], axis=1)
    forced = (lane == 0) | (lane == n_slc - 1)
    score = jnp.where(lane < n_slc, jnp.where(forced, jnp.inf, imp), -jnp.inf)
    sel = _top_blocks(score, SLC_TOPN - 1)
    neg_mask = (sel - 1.0).astype(BF16)
    neg_rows = jnp.concatenate([neg_mask[h:h + 1] for h in range(NSA_KV_HEADS) for _ in range(NSA_GROUP)], axis=0)
    q_aug = jnp.concatenate([q8, neg_rows], axis=1)

    def with_new_token(s, s_new, v_t, v_new):
        m = jnp.maximum(jnp.max(s, axis=1, keepdims=True), s_new)
        p = jnp.exp(s - m)
        p_new = jnp.exp(s_new - m)
        total = jnp.sum(p, axis=1, keepdims=True) + p_new
        return (_dot_nt(p.astype(BF16), v_t) + p_new.astype(BF16).astype(F32) * v_new.astype(F32)) / total

    kaug_ref[:KV_WIDTH, :] = kbuf[slot].astype(BF16)
    s_s = _dot(q_aug, kaug_ref[...]) + biass_ref[...]
    s_new = jnp.sum(q8f * new_ref[0, 0:1, :].astype(F32), axis=1, keepdims=True) + bias0
    o_s = with_new_token(s_s, s_new, vbuf[slot].astype(BF16), new_ref[0, 1:2, :])

    wcol = lax.broadcasted_iota(jnp.int32, (1, win_rows), 1)
    s_w = _dot(q8, kwc_ref[0].astype(BF16)) + biasw_ref[...]
    s_w = jnp.where(win_rows - wcol < WINDOW, s_w, MASK_VALUE)
    s_new = jnp.sum(q8f * new_ref[0, 2:3, :].astype(F32), axis=1, keepdims=True) + bias0
    o_w = with_new_token(s_w, s_new, vwc_ref[0].astype(BF16), new_ref[0, 3:4, :])

    gates = g_ref[0]
    low = []
    for hd in heads:
        r = (gates[:, hd:hd + 1] * o_c[hd:hd + 1]
             + gates[:, NSA_HEADS + hd:NSA_HEADS + hd + 1] * o_s[hd:hd + 1]
             + gates[:, 2 * NSA_HEADS + hd:2 * NSA_HEADS + hd + 1] * o_w[hd:hd + 1])
        low.append(r if hd < NSA_GROUP else pltpu.roll(r, HEAD_DIM, 1))
    lane1 = lax.broadcasted_iota(jnp.int32, (1, LANES), 1)
    for pair in range(NSA_HEADS // 2):
        both = jnp.where(lane1 < HEAD_DIM, low[2 * pair], pltpu.roll(low[2 * pair + 1], HEAD_DIM, 1))
        cols = slice(LANES * pair, LANES * (pair + 1))
        y_ref[0, :, cols] = (both * _silu(z_ref[0, :, cols])).astype(BF16)


def _nsa_sample(page_table, rel_bias, q8, ck, cv, pool_k, pool_v, e_mat, win_k, win_v, new_rows, gates, z):
    n_seq, n_pages = page_table.shape
    page_rows = pool_k.shape[2]
    past = n_pages * page_rows
    n_cmp = ck.shape[1]
    win_rows = win_k.shape[2]
    assert past % SLC_BLOCK == 0 and past // SLC_BLOCK <= LANES and page_rows % LANES == 0
    hbm = pl.BlockSpec(memory_space=pl.ANY)
    per_seq = lambda r, w: pl.BlockSpec((1, r, w), lambda i, pt: (i, 0, 0))
    return pl.pallas_call(
        _nsa_sample_kernel,
        grid_spec=pltpu.PrefetchScalarGridSpec(
            num_scalar_prefetch=1,
            grid=(n_seq,),
            in_specs=[pl.BlockSpec(memory_space=pltpu.SMEM), per_seq(NSA_HEADS, LANES), per_seq(n_cmp, LANES),
                      per_seq(n_cmp, LANES), hbm, hbm,
                      pl.BlockSpec((LANES, past), lambda i, pt: (0, 0)),
                      per_seq(KV_WIDTH, win_rows), per_seq(KV_WIDTH, win_rows), per_seq(4, LANES),
                      per_seq(1, LANES), per_seq(1, NSA_WIDTH)],
            out_specs=per_seq(1, NSA_WIDTH),
            scratch_shapes=[pltpu.VMEM((2, KV_WIDTH, past), F32), pltpu.VMEM((2, KV_WIDTH, past), F32),
                            pltpu.VMEM((KV_WIDTH + LANES, past), BF16),
                            pltpu.VMEM((NSA_HEADS, n_cmp), F32), pltpu.VMEM((NSA_HEADS, past), F32),
                            pltpu.VMEM((NSA_HEADS, win_rows), F32),
                            pltpu.SemaphoreType.DMA((2,)), pltpu.SemaphoreType.DMA((2,))]),
        out_shape=jax.ShapeDtypeStruct((n_seq, 1, NSA_WIDTH), BF16),
        compiler_params=pltpu.CompilerParams(dimension_semantics=("arbitrary",), vmem_limit_bytes=VMEM_LIMIT),
        name="nsa_sample",
    )(page_table.reshape(-1), rel_bias, q8, ck, cv, pool_k, pool_v, e_mat, win_k, win_v, new_rows, gates, z)


def _group_norm_gate(o, z, gn, seg):
    mu = _segment_sum(o, seg) * (1.0 / RET_VDIM)
    d = o - mu
    var = _segment_sum(d * d, seg) * (1.0 / RET_VDIM)
    return (d * lax.rsqrt(var + EPS) * gn * _silu(z)).astype(BF16)


def _retention_prompt_kernel(q_ref, k_ref, v_ref, z_ref, gn_ref, seg_ref, dec_ref, qdec_ref, kdec_ref, cdec_ref,
                             y_ref, st_ref, state):
    chunk = pl.program_id(1)
    tc = q_ref.shape[1]
    lane = lax.broadcasted_iota(jnp.int32, (tc, LANES), 1)
    srow = lax.broadcasted_iota(jnp.int32, (LANES, LANES), 0)
    scol = lax.broadcasted_iota(jnp.int32, (LANES, LANES), 1)
    same_head = (srow // RET_KDIM) == (scol // RET_VDIM)

    @pl.when(chunk == 0)
    def _():
        state[...] = jnp.zeros_like(state)

    for pair in range(RET_HEADS // 2):
        cols = slice(LANES * pair, LANES * (pair + 1))
        q = q_ref[0, :, cols]
        k = k_ref[0, :, cols]
        v = v_ref[0, :, cols]
        s_old = state[pair]
        cross = _dot(q, s_old.astype(BF16)) * qdec_ref[:, cols]
        halves = []
        for e in range(2):
            qe = jnp.where((lane // RET_KDIM) == e, q, jnp.zeros((), BF16))
            scores = _dot_nt(qe, k) * dec_ref[2 * pair + e]
            halves.append(_dot(scores.astype(BF16), v))
        o = jnp.where(lane < RET_VDIM, halves[0], halves[1]) + cross
        y_ref[0, :, cols] = _group_norm_gate(o, z_ref[0, :, cols], gn_ref[:, cols], seg_ref[...])
        kd_t = (k.astype(F32) * kdec_ref[:, cols]).T.astype(BF16)
        s_new = s_old * cdec_ref[:, cols] + jnp.where(same_head, _dot(kd_t, v), 0.0)
        state[pair] = s_new

    @pl.when(chunk == pl.num_programs(1) - 1)
    def _():
        for head in range(RET_HEADS):
            e = head % 2
            st_ref[0, head] = state[head // 2, RET_KDIM * e:RET_KDIM * (e + 1), RET_VDIM * e:RET_VDIM * (e + 1)]


def _retention_tables(tc):
    log_g = jnp.asarray(RET_GAMMA_LOG, F32)
    i = jnp.arange(tc, dtype=F32)
    diff = i[:, None] - i[None, :]
    decay = jnp.where(diff >= 0, jnp.exp(jnp.maximum(diff, 0.0)[None] * log_g[:, None, None]), 0.0)
    widen = lambda a: jnp.repeat(a, RET_VDIM, axis=-1)
    q_decay = widen(jnp.exp((i[:, None] + 1.0) * log_g[None, :]))
    k_decay = widen(jnp.exp((tc - 1.0 - i)[:, None] * log_g[None, :]))
    c_decay = widen(jnp.exp(tc * log_g)[None, :])
    return decay, q_decay, k_decay, c_decay


def _retention_prompt(q, k, v, z, gn, seg):
    b, t, _ = q.shape
    tc = RET_TILE
    assert t % tc == 0
    decay, q_decay, k_decay, c_decay = _retention_tables(tc)
    tile = pl.BlockSpec((1, tc, RET_WIDTH), lambda i, j: (i, j, 0))
    const = lambda a: pl.BlockSpec(a.shape, lambda i, j: (0,) * a.ndim)
    return pl.pallas_call(
        _retention_prompt_kernel,
        grid=(b, t // tc),
        in_specs=[tile, tile, tile, tile, const(gn), const(seg), const(decay), const(q_decay), const(k_decay),
                  const(c_decay)],
        out_specs=[tile, pl.BlockSpec((1, RET_HEADS, RET_KDIM, RET_VDIM), lambda i, j: (i, 0, 0, 0))],
        out_shape=[jax.ShapeDtypeStruct((b, t, RET_WIDTH), BF16),
                   jax.ShapeDtypeStruct((b, RET_HEADS, RET_KDIM, RET_VDIM), F32)],
        scratch_shapes=[pltpu.VMEM((RET_HEADS // 2, LANES, LANES), F32)],
        compiler_params=pltpu.CompilerParams(dimension_semantics=("arbitrary", "arbitrary"),
                                             vmem_limit_bytes=VMEM_LIMIT),
        name="retention_prompt",
    )(q, k, v, z, gn, seg, decay, q_decay, k_decay, c_decay)


def _retention_sample_kernel(qt_ref, kt_ref, v_ref, z_ref, gn_ref, gam_ref, st_ref, y_ref, so_ref):
    n = v_ref.shape[1]
    qt = qt_ref[0]
    kt = kt_ref[0]
    gam = gam_ref[...]
    qk = jnp.sum(qt * kt, axis=1)
    for s in range(n):
        st = st_ref[s]
        vs = v_ref[0, s]
        cross = jnp.sum(qt[:, :, s:s + 1] * st, axis=1)
        so_ref[s] = st * gam[:, :, None] + kt[:, :, s:s + 1] * vs[:, None, :]
        o = qk[:, s:s + 1] * vs + gam * cross
        mu = jnp.mean(o, axis=-1, keepdims=True)
        d = o - mu
        var = jnp.mean(d * d, axis=-1, keepdims=True)
        y_ref[0, s] = (d * lax.rsqrt(var + EPS) * gn_ref[...] * _silu(z_ref[0, s])).astype(BF16)


def _retention_sample(q, k, v, z, gn, state):
    n_seq = q.shape[0]
    n = min(RET_SAMPLE_SEQS, n_seq)
    assert n_seq % n == 0
    steps = n_seq // n
    to_cols = lambda a: a.astype(F32).reshape(steps, n, RET_HEADS, RET_KDIM).transpose(0, 2, 3, 1)
    to_rows = lambda a: a.astype(F32).reshape(steps, n, RET_HEADS, RET_VDIM)
    gam = jnp.exp(jnp.asarray(RET_GAMMA_LOG, F32)).reshape(RET_HEADS, 1)
    cols = pl.BlockSpec((1, RET_HEADS, RET_KDIM, n), lambda i: (i, 0, 0, 0))
    rws = pl.BlockSpec((1, n, RET_HEADS, RET_VDIM), lambda i: (i, 0, 0, 0))
    st = pl.BlockSpec((n, RET_HEADS, RET_KDIM, RET_VDIM), lambda i: (i, 0, 0, 0))
    y, new_state = pl.pallas_call(
        _retention_sample_kernel,
        grid=(steps,),
        in_specs=[cols, cols, rws, rws, pl.BlockSpec((RET_HEADS, RET_VDIM), lambda i: (0, 0)),
                  pl.BlockSpec((RET_HEADS, 1), lambda i: (0, 0)), st],
        out_specs=[rws, st],
        out_shape=[jax.ShapeDtypeStruct((steps, n, RET_HEADS, RET_VDIM), BF16),
                   jax.ShapeDtypeStruct(state.shape, F32)],
        compiler_params=pltpu.CompilerParams(dimension_semantics=("parallel",), vmem_limit_bytes=VMEM_LIMIT),
        name="retention_sample",
    )(to_cols(q), to_cols(k), to_rows(v), to_rows(z), gn.reshape(RET_HEADS, RET_VDIM), gam, state)
    return y.reshape(n_seq, RET_WIDTH), new_state


def _out_kernel(x_ref, yn_ref, yr_ref, w_ref, o_ref):
    y = jnp.concatenate([yn_ref[0], yr_ref[0]], axis=1)
    o_ref[0] = x_ref[0] + _dot(y, w_ref[...])


def _out_project(x, y_nsa, y_ret, w):
    nb, rows, d = x.shape
    tm = min(PROJ_ROWS, rows)
    spec = lambda width: pl.BlockSpec((1, tm, width), lambda i, j: (i, j, 0))
    return pl.pallas_call(
        _out_kernel,
        grid=(nb, rows // tm),
        in_specs=[spec(d), spec(NSA_WIDTH), spec(RET_WIDTH), pl.BlockSpec(w.shape, lambda i, j: (0, 0))],
        out_specs=spec(d),
        out_shape=jax.ShapeDtypeStruct(x.shape, F32),
        compiler_params=pltpu.CompilerParams(dimension_semantics=("parallel", "parallel"),
                                             vmem_limit_bytes=VMEM_LIMIT),
        name="output_projection",
    )(x, y_nsa, y_ret, w)


def _rotary_tables(pos):
    half = RET_KDIM // 2
    inv = ROPE_BASE ** (-jnp.arange(half, dtype=F32) / half)
    ang = pos.astype(F32)[:, None] * inv[None, :]
    cos, sin = jnp.cos(ang), jnp.sin(ang)
    reps = LANES // RET_KDIM
    return jnp.tile(jnp.concatenate([cos, cos], axis=1), (1, reps)), jnp.tile(jnp.concatenate([-sin, sin], axis=1), (1, reps))


def _arrange_w_in(w_in):
    d = w_in.shape[0]
    parts = jnp.split(w_in, np.cumsum(SPLIT_WIDTHS)[:-1].tolist(), axis=1)
    order = [hd for g in range(NSA_GROUP) for hd in (g, NSA_GROUP + g)]
    q = parts[0].reshape(d, NSA_HEADS, HEAD_DIM)[:, order].reshape(d, NSA_WIDTH)
    gates = parts[7].reshape(d, NSA_HEADS, 3).transpose(0, 2, 1).reshape(d, 3 * NSA_HEADS)
    gates = jnp.pad(gates, ((0, 0), (0, LANES - 3 * NSA_HEADS)))
    w = jnp.concatenate([q] + parts[1:7] + parts[8:] + [gates], axis=1)
    assert w.shape[1] == COL_END
    return w.astype(BF16)


def _compress_weights(pe, w1, w2):
    w1r = w1.reshape(CMP_BLOCK, HEAD_DIM, HEAD_DIM)
    w1_both = jnp.zeros((CMP_BLOCK, NSA_KV_HEADS, HEAD_DIM, NSA_KV_HEADS, HEAD_DIM), F32)
    w2_both = jnp.zeros((NSA_KV_HEADS, HEAD_DIM, NSA_KV_HEADS, HEAD_DIM), F32)
    for h in range(NSA_KV_HEADS):
        w1_both = w1_both.at[:, h, :, h, :].set(w1r)
        w2_both = w2_both.at[h, :, h, :].set(w2)
    return (jnp.tile(pe, (1, NSA_KV_HEADS)), w1_both.reshape(CMP_BLOCK * KV_WIDTH, KV_WIDTH).astype(BF16),
            w2_both.reshape(KV_WIDTH, KV_WIDTH).astype(BF16))


def _even_odd(c):
    n, nc, w = c.shape
    ratio = SLC_BLOCK // CMP_BLOCK
    return c.reshape(n, nc // ratio, ratio, w).transpose(0, 2, 1, 3).reshape(n, nc, w)


def _block_membership(n_keys):
    k = jnp.arange(n_keys)[:, None] // SLC_BLOCK
    return jnp.where(k == jnp.arange(LANES)[None, :], -MASK_VALUE, 0.0).astype(BF16)


def kernel(x_prompt, x_sample, cache_cmp_k, cache_cmp_v, cache_slc_k, cache_slc_v, cache_win_k, cache_win_v,
           state_ret, page_table, norm_g, w_in, q_norm_g, k_norm_g, cmp_pe_k, cmp_w1_k, cmp_w2_k, cmp_pe_v,
           cmp_w1_v, cmp_w2_v, rel_bias, ret_gn_g, w_out):
    b, t, d = x_prompt.shape
    n_seq, dec_len, _ = x_sample.shape
    n_pages = page_table.shape[1]
    page_rows = cache_cmp_k.shape[1]
    past = n_pages * page_rows
    assert dec_len == 1 and past % CMP_BLOCK == 0
    kv4 = lambda a: a.reshape(a.shape[0], a.shape[1], NSA_KV_HEADS, HEAD_DIM)

    w = _arrange_w_in(w_in)
    w_o = w_out.astype(BF16)
    ng = norm_g.reshape(1, d)
    qg = (jnp.tile(q_norm_g, LANES // HEAD_DIM) * ATTN_SCALE).reshape(1, LANES)
    kg = jnp.tile(k_norm_g, LANES // HEAD_DIM).reshape(1, LANES)
    gn = ret_gn_g.reshape(1, RET_WIDTH)
    lane = np.arange(LANES)
    seg = jnp.asarray(lane[:, None] // HEAD_DIM == lane[None, :] // HEAD_DIM, BF16)
    cw = _compress_weights(cmp_pe_k, cmp_w1_k, cmp_w2_k) + _compress_weights(cmp_pe_v, cmp_w1_v, cmp_w2_v)

    cos, sin = _rotary_tables(jnp.arange(t))
    (q, kc, vc, ks, vs, kw, vw, ksb, _, kwb, _, gates, zn, qr, kr, vr, zr, vs_t, vw_t) = _project(
        x_prompt, cos, sin, ng, w, qg, kg, seg)
    ck, cv = _compress_prompt(kc, vc, cw, kg, seg)
    y_nsa = _nsa_prompt(rel_bias, q, _even_odd(ck), _even_odd(cv), ksb, vs_t, kwb, vw_t, _block_membership(t), gates, zn)
    y_ret, p_ret = _retention_prompt(qr, kr, vr, zr, gn, seg)
    y_prompt = _out_project(x_prompt, y_nsa, y_ret, w_o)
    keep = min(WINDOW, t)
    prompt_out = (y_prompt, kv4(kc), kv4(vc), kv4(ks), kv4(vs), kv4(kw[:, t - keep:]), kv4(vw[:, t - keep:]), p_ret)

    cos, sin = _rotary_tables(jnp.full((n_seq,), past))
    xs = x_sample.reshape(1, n_seq, d)
    (q, kc, vc, ks, vs, kw, vw, ksb, vsb, kwb, vwb, gates, zn, qr, kr, vr, zr) = [
        a[0] for a in _project(xs, cos, sin, ng, w, qg, kg, seg)[:17]]
    chan = lambda a: jnp.transpose(a, (0, 2, 3, 1)).reshape(a.shape[0], KV_WIDTH, a.shape[1])
    ck, cv = _compress_sample(page_table, chan(cache_cmp_k), chan(cache_cmp_v), cw, kg, seg)
    half = jnp.asarray(lane[None, :] // HEAD_DIM == (np.arange(NSA_HEADS) // NSA_GROUP)[:, None])
    q8 = jnp.where(half[None], jnp.tile(q.reshape(n_seq, NSA_GROUP, LANES), (1, NSA_KV_HEADS, 1)), jnp.zeros((), BF16))
    new_rows = jnp.stack([ksb, vsb, kwb, vwb], axis=1)
    y_nsa = _nsa_sample(page_table, rel_bias, q8, _even_odd(ck), _even_odd(cv), chan(cache_slc_k), chan(cache_slc_v),
                        _block_membership(past).T, chan(cache_win_k), chan(cache_win_v), new_rows,
                        gates.reshape(n_seq, 1, LANES), zn.reshape(n_seq, 1, NSA_WIDTH))
    y_ret, s_ret = _retention_sample(qr, kr, vr, zr, gn, state_ret)
    y_sample = _out_project(xs, y_nsa.reshape(1, n_seq, NSA_WIDTH), y_ret.reshape(1, n_seq, RET_WIDTH), w_o)
    keep = min(WINDOW, cache_win_k.shape[1] + 1)
    new4 = lambda a: a.reshape(n_seq, 1, NSA_KV_HEADS, HEAD_DIM)
    s_win_k = jnp.concatenate([cache_win_k, new4(kw)], axis=1)[:, -keep:]
    s_win_v = jnp.concatenate([cache_win_v, new4(vw)], axis=1)[:, -keep:]
    sample_out = (y_sample.reshape(n_seq, 1, d), new4(kc), new4(vc), new4(ks), new4(vs), s_win_k, s_win_v, s_ret)

    return (prompt_out[0], sample_out[0]) + prompt_out[1:] + sample_out[1:]
```

```python
import functools

import numpy as np
import jax
import jax.numpy as jnp
from jax import lax
from jax.experimental import pallas as pl
from jax.experimental.pallas import tpu as pltpu

F32, BF16 = jnp.float32, jnp.bfloat16

NSA_HEADS = 8
NSA_KV_HEADS = 2
HEAD_DIM = 64
NSA_GROUP = NSA_HEADS // NSA_KV_HEADS
NSA_WIDTH = NSA_HEADS * HEAD_DIM
KV_WIDTH = NSA_KV_HEADS * HEAD_DIM
CMP_BLOCK = 32
SLC_BLOCK = 64
SLC_TOPN = 16
WINDOW = 512
ATTN_SCALE = HEAD_DIM ** -0.5
RET_HEADS = 8
RET_KDIM = 64
RET_VDIM = 64
RET_WIDTH = RET_HEADS * RET_VDIM
ROPE_BASE = 10000.0
REL_BUCKETS = 32
REL_MAX_DIST = 1024
EPS = 1e-6
SPLIT_WIDTHS = (NSA_WIDTH, KV_WIDTH, KV_WIDTH, KV_WIDTH, KV_WIDTH, KV_WIDTH, KV_WIDTH,
                3 * NSA_HEADS, NSA_WIDTH, RET_HEADS * RET_KDIM, RET_HEADS * RET_KDIM, RET_WIDTH, RET_WIDTH)

LANES = 128
VMEM_LIMIT = 56 * 1024 * 1024

PROJ_ROWS = 512
Q_TILE = 128
SLC_KEYS = 512
RET_TILE = 256
RET_SAMPLE_SEQS = 16
CMP_PITCH = CMP_BLOCK + 4

MASK_VALUE = -float(2 ** 30)

MXU_COLS = 256
COL_Q = 0
COL_KC = COL_Q + NSA_WIDTH
COL_VC = COL_KC + LANES
COL_KS = COL_VC + LANES
COL_VS = COL_KS + LANES
COL_KW = COL_VS + LANES
COL_VW = COL_KW + LANES
COL_ZN = COL_VW + LANES
COL_QR = COL_ZN + NSA_WIDTH
COL_KR = COL_QR + RET_WIDTH
COL_VR = COL_KR + RET_WIDTH
COL_ZR = COL_VR + RET_WIDTH
COL_G = COL_ZR + RET_WIDTH
COL_END = COL_G + LANES


def _bucket_lower_bounds():
    exact = REL_BUCKETS // 2
    ratio = REL_MAX_DIST // exact
    lows = list(range(exact))
    n = exact
    for k in range(REL_BUCKETS - exact):
        while n ** (REL_BUCKETS - exact) < exact ** (REL_BUCKETS - exact) * ratio ** k:
            n += 1
        lows.append(n)
    return tuple(lows)


BUCKET_LOW = _bucket_lower_bounds()
FAR_DIST = BUCKET_LOW[-1]
RET_GAMMA_LOG = tuple(float(np.log1p(-np.exp2(-5.0 - h))) for h in range(RET_HEADS))


def _dot(a, b):
    return jnp.dot(a, b, preferred_element_type=F32)


def _dot_nt(a, b):
    return lax.dot_general(a, b, (((1,), (1,)), ((), ())), preferred_element_type=F32)


def _segment_sum(v, seg):
    hi = v.astype(BF16)
    lo = (v - hi.astype(F32)).astype(BF16)
    return _dot(hi, seg) + _dot(lo, seg)


def _head_rms(y, g, seg):
    ms = _segment_sum(y * y, seg) * (1.0 / HEAD_DIM)
    return y * lax.rsqrt(ms + EPS) * g


def _silu(x):
    return x * jax.nn.sigmoid(x)


def _rel_bias_rows(dist, rel_ref, heads):
    out = [jnp.full(dist.shape, rel_ref[0, hd], F32) for hd in heads]
    for bkt in range(1, REL_BUCKETS):
        hit = dist >= BUCKET_LOW[bkt]
        out = [jnp.where(hit, rel_ref[bkt, hd], o) for hd, o in zip(heads, out)]
    return [o - rel_ref[REL_BUCKETS - 1, hd] for hd, o in zip(heads, out)]


def _top_blocks(score, n_pick):
    col = lax.broadcasted_iota(jnp.int32, score.shape, 1).astype(F32)
    sel = jnp.zeros(score.shape, F32)
    neg_inf = -jnp.inf
    for _ in range(n_pick):
        m = jnp.max(score, axis=1, keepdims=True)
        first = jnp.min(jnp.where(score == m, col, float(score.shape[1])), axis=1, keepdims=True)
        hit = col == first
        sel = jnp.where(hit & (m > neg_inf), 1.0, sel)
        score = jnp.where(hit, neg_inf, score)
    return sel


def _top_blocks_cols(score, n_pick):
    n_rows = score.shape[0]
    rowi = lax.broadcasted_iota(jnp.int32, score.shape, 0).astype(F32)
    sel = jnp.zeros(score.shape, F32)
    neg_inf = -jnp.inf
    for _ in range(n_pick):
        m = jnp.max(score, axis=0, keepdims=True)
        first = jnp.min(jnp.where(score == m, rowi, float(n_rows)), axis=0, keepdims=True)
        hit = rowi == first
        sel = sel + jnp.where(hit, jnp.where(m > neg_inf, 1.0, 0.0), 0.0)
        score = jnp.where(hit, neg_inf, score)
    return sel


def _proj_kernel(x_ref, cos_ref, sin_ref, ng_ref, w_ref, qg_ref, kg_ref, seg_ref,
                 q_o, kc_o, vc_o, ks_o, vs_o, kw_o, vw_o, ksb_o, vsb_o, kwb_o, vwb_o,
                 g_o, zn_o, qr_o, kr_o, vr_o, zr_o, vst_o, vwt_o):
    x = x_ref[0]
    inv = lax.rsqrt(jnp.mean(x * x, axis=-1, keepdims=True) + EPS)
    xn = (x * inv * ng_ref[...]).astype(BF16)
    seg = seg_ref[...]

    def mm(col, width):
        return _dot(xn, w_ref[:, col:col + width])

    def chunk(y, c):
        return y[:, LANES * c:LANES * (c + 1)]

    y = mm(COL_Q, NSA_WIDTH)
    for c in range(NSA_WIDTH // LANES):
        q_o[0, :, LANES * c:LANES * (c + 1)] = _head_rms(chunk(y, c), qg_ref[...], seg).astype(BF16)
    y = mm(COL_KC, 2 * LANES)
    kc_o[0] = chunk(y, 0)
    vc_o[0] = chunk(y, 1)
    for col, k_o, v_o, kb_o, vb_o, vt_o in ((COL_KS, ks_o, vs_o, ksb_o, vsb_o, vst_o),
                                            (COL_KW, kw_o, vw_o, kwb_o, vwb_o, vwt_o)):
        y = mm(col, 2 * LANES)
        k = _head_rms(chunk(y, 0), kg_ref[...], seg)
        k_o[0] = k
        kb_o[0] = k.astype(BF16)
        v = chunk(y, 1)
        v_o[0] = v
        vb_o[0] = v.astype(BF16)
        for c in range(v.shape[0] // LANES):
            vt_o[0, c] = v[LANES * c:LANES * (c + 1)].T.astype(BF16)
    zn_o[0] = mm(COL_ZN, NSA_WIDTH)

    cos = cos_ref[...]
    sin = sin_ref[...]
    lane = lax.broadcasted_iota(jnp.int32, cos.shape, 1)
    first_half = (lane % HEAD_DIM) < (HEAD_DIM // 2)
    for col, o_ref, scale in ((COL_QR, qr_o, None), (COL_KR, kr_o, RET_KDIM ** -0.5)):
        y = mm(col, RET_WIDTH)
        for c in range(RET_WIDTH // LANES):
            yc = chunk(y, c)
            partner = jnp.where(first_half, pltpu.roll(yc, LANES - HEAD_DIM // 2, 1),
                                pltpu.roll(yc, HEAD_DIM // 2, 1))
            r = yc * cos + partner * sin
            if scale is not None:
                r = r * scale
            o_ref[0, :, LANES * c:LANES * (c + 1)] = r.astype(BF16)
    vr_o[0] = mm(COL_VR, RET_WIDTH).astype(BF16)
    zr_o[0] = mm(COL_ZR, RET_WIDTH)
    g_o[0] = jax.nn.sigmoid(mm(COL_G, LANES))


def _project(x, cos, sin, norm_g, w, qg, kg, seg):
    nb, rows, d = x.shape
    tm = min(PROJ_ROWS, rows)
    assert rows % tm == 0 and tm % LANES == 0
    row_spec = lambda width: pl.BlockSpec((1, tm, width), lambda i, j: (i, j, 0))
    const = lambda shape: pl.BlockSpec(shape, lambda i, j: (0,) * len(shape))
    outs = [(NSA_WIDTH, BF16)] + [(LANES, F32)] * 6 + [(LANES, BF16)] * 4 + [(LANES, F32), (NSA_WIDTH, F32),
            (RET_WIDTH, BF16), (RET_WIDTH, BF16), (RET_WIDTH, BF16), (RET_WIDTH, F32)]
    blocks_spec = pl.BlockSpec((1, tm // LANES, LANES, LANES), lambda i, j: (i, j, 0, 0))
    blocks_shape = jax.ShapeDtypeStruct((nb, rows // LANES, LANES, LANES), BF16)
    return pl.pallas_call(
        _proj_kernel,
        grid=(nb, rows // tm),
        in_specs=[row_spec(d), pl.BlockSpec((tm, LANES), lambda i, j: (j, 0)),
                  pl.BlockSpec((tm, LANES), lambda i, j: (j, 0)), const((1, d)), const(w.shape),
                  const((1, LANES)), const((1, LANES)), const((LANES, LANES))],
        out_specs=[row_spec(wd) for wd, _ in outs] + [blocks_spec] * 2,
        out_shape=[jax.ShapeDtypeStruct((nb, rows, wd), dt) for wd, dt in outs] + [blocks_shape] * 2,
        compiler_params=pltpu.CompilerParams(dimension_semantics=("parallel", "parallel"),
                                             vmem_limit_bytes=VMEM_LIMIT),
        name="input_projection",
    )(x, cos, sin, norm_g, w, qg, kg, seg)


def _compress_rows(buf_ref, pe_ref, w1_ref, w2_ref, n_blk, pitch=CMP_BLOCK):
    parts = []
    for i in range(CMP_BLOCK):
        rows = buf_ref[pl.ds(i, n_blk, stride=pitch), :] + pe_ref[i:i + 1, :]
        parts.append(rows.astype(BF16))
    flat = jnp.concatenate(parts, axis=1)
    hidden = _silu(_dot(flat, w1_ref[...]))
    return _dot(hidden.astype(BF16), w2_ref[...])


def _compress_kernel(kc_ref, vc_ref, pek_ref, w1k_ref, w2k_ref, pev_ref, w1v_ref, w2v_ref, kg_ref, seg_ref,
                     ck_o, cv_o):
    n_blk = ck_o.shape[1]
    ck = _compress_rows(kc_ref.at[0], pek_ref, w1k_ref, w2k_ref, n_blk)
    ck_o[0] = _head_rms(ck, kg_ref[...], seg_ref[...]).astype(BF16)
    cv_o[0] = _compress_rows(vc_ref.at[0], pev_ref, w1v_ref, w2v_ref, n_blk).astype(BF16)


def _compress_prompt(kc, vc, cw, kg, seg):
    b, t, _ = kc.shape
    n_blk = t // CMP_BLOCK
    seq = pl.BlockSpec((1, t, LANES), lambda i: (i, 0, 0))
    const = lambda a: pl.BlockSpec(a.shape, lambda i: (0,) * a.ndim)
    out = pl.BlockSpec((1, n_blk, LANES), lambda i: (i, 0, 0))
    return pl.pallas_call(
        _compress_kernel,
        grid=(b,),
        in_specs=[seq, seq] + [const(a) for a in cw] + [const(kg), const(seg)],
        out_specs=[out, out],
        out_shape=[jax.ShapeDtypeStruct((b, n_blk, LANES), BF16)] * 2,
        compiler_params=pltpu.CompilerParams(dimension_semantics=("parallel",), vmem_limit_bytes=VMEM_LIMIT),
        name="compress_prompt",
    )(kc, vc, *cw, kg, seg)


def _page_copy(pt_ref, pool_ref, buf_ref, sem, seq, page, n_pages):
    rows = pool_ref.shape[2]
    return pltpu.make_async_copy(pool_ref.at[pt_ref[seq * n_pages + page]],
                                 buf_ref.at[:, pl.ds(page * rows, rows)], sem)


def _gather_pages(pt_ref, pools, bufs, sems, seq, slot, n_pages, wait):
    for pool_ref, buf_ref, sem in zip(pools, bufs, sems):
        for page in range(n_pages):
            copy = _page_copy(pt_ref, pool_ref, buf_ref.at[slot], sem.at[slot], seq, page, n_pages)
            if wait:
                copy.wait()
            else:
                copy.start()


def _double_buffered_gather(pt_ref, pools, bufs, sems, n_pages):
    seq = pl.program_id(0)
    slot = seq % 2

    @pl.when(seq == 0)
    def _():
        _gather_pages(pt_ref, pools, bufs, sems, seq, slot, n_pages, wait=False)

    @pl.when(seq + 1 < pl.num_programs(0))
    def _():
        _gather_pages(pt_ref, pools, bufs, sems, seq + 1, 1 - slot, n_pages, wait=False)

    _gather_pages(pt_ref, pools, bufs, sems, seq, slot, n_pages, wait=True)
    return slot


def _compress_sample_kernel(pt_ref, pk_ref, pv_ref, pek_ref, w1k_ref, w2k_ref, pev_ref, w1v_ref, w2v_ref,
                            kg_ref, seg_ref, ck_o, cv_o, kbuf, vbuf, rows_ref, ksem, vsem):
    page_rows = pk_ref.shape[2]
    n_pages = kbuf.shape[2] // page_rows
    n_blk = ck_o.shape[1]
    slot = _double_buffered_gather(pt_ref, (pk_ref, pv_ref), (kbuf, vbuf), (ksem, vsem), n_pages)

    blocks_per_page = page_rows // CMP_BLOCK

    def token_major(buf):
        for page in range(n_pages):
            tokens = buf[slot, :, page * page_rows:(page + 1) * page_rows].T
            for c in range(blocks_per_page):
                r0 = (page * blocks_per_page + c) * CMP_PITCH
                rows_ref[r0:r0 + CMP_BLOCK, :] = tokens[c * CMP_BLOCK:(c + 1) * CMP_BLOCK]

    token_major(kbuf)
    ck = _compress_rows(rows_ref, pek_ref, w1k_ref, w2k_ref, n_blk, CMP_PITCH)
    ck_o[0] = _head_rms(ck, kg_ref[...], seg_ref[...]).astype(BF16)
    token_major(vbuf)
    cv_o[0] = _compress_rows(rows_ref, pev_ref, w1v_ref, w2v_ref, n_blk, CMP_PITCH).astype(BF16)


def _compress_sample(page_table, pool_k, pool_v, cw, kg, seg):
    n_seq, n_pages = page_table.shape
    page_rows = pool_k.shape[2]
    assert page_rows == LANES
    past = n_pages * page_rows
    n_blk = past // CMP_BLOCK
    hbm = pl.BlockSpec(memory_space=pl.ANY)
    const = lambda a: pl.BlockSpec(a.shape, lambda i, pt: (0,) * a.ndim)
    out = pl.BlockSpec((1, n_blk, LANES), lambda i, pt: (i, 0, 0))
    return pl.pallas_call(
        _compress_sample_kernel,
        grid_spec=pltpu.PrefetchScalarGridSpec(
            num_scalar_prefetch=1,
            grid=(n_seq,),
            in_specs=[hbm, hbm] + [const(a) for a in cw] + [const(kg), const(seg)],
            out_specs=[out, out],
            scratch_shapes=[pltpu.VMEM((2, KV_WIDTH, past), F32), pltpu.VMEM((2, KV_WIDTH, past), F32),
                            pltpu.VMEM((n_blk * CMP_PITCH, KV_WIDTH), F32),
                            pltpu.SemaphoreType.DMA((2,)), pltpu.SemaphoreType.DMA((2,))]),
        out_shape=[jax.ShapeDtypeStruct((n_seq, n_blk, LANES), BF16)] * 2,
        compiler_params=pltpu.CompilerParams(dimension_semantics=("arbitrary",), vmem_limit_bytes=VMEM_LIMIT),
        name="compress_sample",
    )(page_table.reshape(-1), pool_k, pool_v, *cw, kg, seg)


def _band_geometry(tq):
    near = -(-(FAR_DIST + tq - 1) // tq)
    rel_max = near + SLC_KEYS // tq - 2
    rel_min = -max(WINDOW // tq, SLC_KEYS // tq - 1)
    return near, rel_max, rel_max - rel_min + 1


def _nsa_prompt_kernel(rel_ref, q_ref, ck_ref, cv_ref, ks_ref, vst_ref, kw_ref, vwt_ref, e_ref, g_ref, z_ref,
                       y_ref, biasc_ref, band_ref, logit_ref, prob_ref, acc_ref, winb_ref):
    tq = q_ref.shape[1]
    n_cmp = ck_ref.shape[1]
    n_slc = n_cmp // (SLC_BLOCK // CMP_BLOCK)
    rows = NSA_GROUP * tq
    kt = SLC_KEYS
    chunks = kt // tq
    near, rel_max, n_band = _band_geometry(tq)
    win_tiles = WINDOW // tq
    win_keys = WINDOW + tq
    qb = pl.program_id(0)
    b = pl.program_id(1)
    t0 = qb * tq
    all_heads = tuple(range(NSA_HEADS))

    lane = lax.broadcasted_iota(jnp.int32, (tq, LANES), 1)
    row = lax.broadcasted_iota(jnp.int32, (tq, LANES), 0)

    @pl.when((qb == 0) & (b == 0))
    def _():
        for c in range(n_band):
            dist = lane - row + tq * (rel_max - c)
            for hd, bias in zip(all_heads, _rel_bias_rows(dist, rel_ref, all_heads)):
                band_ref[c, hd] = bias

    def band_bias(first_chunk, n_chunks, h):
        base = rel_max - qb + first_chunk
        return jnp.concatenate(
            [jnp.concatenate([band_ref[base + u, NSA_GROUP * h + g] for g in range(NSA_GROUP)], axis=1)
             for u in range(n_chunks)], axis=0)

    ccol = lax.broadcasted_iota(jnp.int32, (tq, n_cmp), 1)
    crow = lax.broadcasted_iota(jnp.int32, (tq, n_cmp), 0)
    cblock = (SLC_BLOCK // CMP_BLOCK) * (ccol % n_slc) + ccol // n_slc
    cdist = t0 + crow - (CMP_BLOCK * cblock + CMP_BLOCK - 1)

    @pl.when(b == 0)
    def _():
        for hd, bias in zip(all_heads, _rel_bias_rows(cdist, rel_ref, all_heads)):
            biasc_ref[hd] = bias

    q_all = q_ref[0]
    gates = g_ref[0]
    ck = ck_ref[0]
    cv = cv_ref[0]
    neg_inf = -jnp.inf
    def key_minus_query(n_keys):
        return (lax.broadcasted_iota(jnp.int32, (n_keys, rows), 0)
                - lax.broadcasted_iota(jnp.int32, (n_keys, rows), 1) % tq)

    ones = jnp.ones((), BF16)

    def with_ones(v_t, h):
        own = (lax.broadcasted_iota(jnp.int32, v_t.shape, 0) // HEAD_DIM) == h
        return jnp.where(own, v_t, ones)

    def value_blocks(ref, first_chunk, n_chunks):
        return jnp.concatenate([ref[0, first_chunk + u] for u in range(n_chunks)], axis=1)

    kv_heads = range(NSA_KV_HEADS)
    w0 = pl.multiple_of(jnp.maximum(t0 - WINDOW, 0), tq)

    @pl.when((b == 0) & (qb <= win_tiles))
    def _():
        w_dist = (t0 - w0) - key_minus_query(win_keys)
        in_window = (w_dist >= 0) & (w_dist < WINDOW)
        for h in kv_heads:
            winb_ref[h] = jnp.where(in_window, band_bias(w0 // tq, win_tiles + 1, h), MASK_VALUE)

    q_pads = [jnp.concatenate(
        [jnp.where((lane // HEAD_DIM) == h, q_all[:, LANES * g:LANES * (g + 1)], jnp.zeros((), BF16))
         for g in range(NSA_GROUP)], axis=0) for h in kv_heads]

    k_win = kw_ref[0, pl.ds(w0, win_keys), :]
    v_win = value_blocks(vwt_ref, w0 // tq, win_tiles + 1)
    o_ws = []
    for h in kv_heads:
        s_w = _dot_nt(k_win, q_pads[h]) + winb_ref[h]
        p_w = jnp.exp(s_w - jnp.max(s_w, axis=0, keepdims=True))
        acc_w = _dot(with_ones(v_win, h), p_w.astype(BF16))
        sum_row = HEAD_DIM * (1 - h)
        o_ws.append(acc_w / acc_w[sum_row:sum_row + 1, :])

    q_augs, o_cs = [], []
    for h in kv_heads:
        q_pad = q_pads[h]

        s_c = _dot_nt(q_pad, ck).reshape(NSA_GROUP, tq, n_cmp) + biasc_ref[NSA_GROUP * h:NSA_GROUP * (h + 1)]
        s_c = jnp.where((cdist >= 0)[None], s_c, neg_inf)
        m_c = jnp.max(s_c, axis=-1, keepdims=True)
        m_c = jnp.where(m_c == neg_inf, 0.0, m_c)
        e_c = jnp.exp(s_c - m_c)
        sum_c = jnp.sum(e_c, axis=-1, keepdims=True)
        p_c = e_c / jnp.where(sum_c > 0.0, sum_c, 1.0)
        o_cs.append(_dot(p_c.reshape(rows, n_cmp).astype(BF16), cv))

        imp = p_c[0]
        for g in range(1, NSA_GROUP):
            imp = imp + p_c[g]
        imp = imp[:, :n_slc] + imp[:, n_slc:]
        if n_slc < LANES:
            imp = jnp.concatenate([imp, jnp.zeros((tq, LANES - n_slc), F32)], axis=1)
        cur = (t0 + row) // SLC_BLOCK
        forced = (lane == 0) | (lane == cur) | (lane == cur - 1)
        score = jnp.where(lane <= cur, jnp.where(forced, jnp.inf, imp), neg_inf)
        sel = _top_blocks_cols(score.T, SLC_TOPN).T
        neg_mask = (sel - 1.0).astype(BF16)
        q_augs.append(jnp.concatenate([q_pad, jnp.concatenate([neg_mask] * NSA_GROUP, axis=0)], axis=1))

    last = qb // chunks

    def tile_logits(j):
        k0 = pl.multiple_of(j * kt, kt)
        k_aug = jnp.concatenate([ks_ref[0, pl.ds(k0, kt), :], e_ref[pl.ds(k0, kt), :]], axis=1)
        return [_dot_nt(k_aug, q_augs[h]) for h in kv_heads]

    def tile_pv(j):
        v_t = value_blocks(vst_ref, j * chunks, chunks)
        return [_dot(with_ones(v_t, h), prob_ref[h]) for h in kv_heads]

    for h, s in enumerate(tile_logits(0)):
        logit_ref[h] = s
        prob_ref[h] = jnp.zeros((kt, rows), BF16)
        acc_ref[h] = jnp.zeros((KV_WIDTH, rows), F32)

    def slc_tile(j, m_old, biased, final=False):
        pv = tile_pv(jnp.maximum(j - 1, 0))
        logits = [logit_ref[h] for h in kv_heads]
        if biased:
            logits = [s + band_bias(j * chunks, chunks, h) for h, s in enumerate(logits)]
        if final:
            causal = key_minus_query(kt) <= t0 - j * kt
            logits = [jnp.where(causal, s, MASK_VALUE) for s in logits]
        m_new = [jnp.maximum(m_old[h], jnp.max(logits[h], axis=0, keepdims=True)) for h in kv_heads]
        probs = [jnp.exp(logits[h] - m_new[h]).astype(BF16) for h in kv_heads]
        nxt = None if final else tile_logits(j + 1)
        for h in kv_heads:
            acc_ref[h] = jnp.exp(m_old[h] - m_new[h]) * (acc_ref[h] + pv[h])
            prob_ref[h] = probs[h]
            if not final:
                logit_ref[h] = nxt[h]
        return tuple(m_new)

    n_far = jnp.maximum(qb - (near - 1), 0) // chunks
    m_run = (jnp.full((1, rows), neg_inf, F32),) * NSA_KV_HEADS
    m_run = lax.fori_loop(0, n_far, functools.partial(slc_tile, biased=False), m_run)
    m_run = lax.fori_loop(n_far, last, functools.partial(slc_tile, biased=True), m_run)
    slc_tile(last, m_run, biased=True, final=True)
    acc_slc = [acc_ref[h] + pv for h, pv in enumerate(tile_pv(last))]

    gates_t = gates.T
    for h in range(NSA_KV_HEADS):
        sum_row = HEAD_DIM * (1 - h)
        o_c = o_cs[h]
        acc_s = acc_slc[h]
        o_s = acc_s / acc_s[sum_row:sum_row + 1, :]
        o_w = o_ws[h]

        mixed = []
        for g in range(NSA_GROUP):
            hd = NSA_GROUP * h + g
            rs = slice(g * tq, (g + 1) * tq)
            key_major = (gates_t[NSA_HEADS + hd:NSA_HEADS + hd + 1, :] * o_s[:, rs]
                         + gates_t[2 * NSA_HEADS + hd:2 * NSA_HEADS + hd + 1, :] * o_w[:, rs])
            mixed.append(gates[:, hd:hd + 1] * o_c[rs] + key_major.T)
        for pair in range(NSA_GROUP // 2):
            a, c = mixed[2 * pair], mixed[2 * pair + 1]
            if h == 0:
                both = jnp.where(lane < HEAD_DIM, a, pltpu.roll(c, HEAD_DIM, 1))
            else:
                both = jnp.where(lane < HEAD_DIM, pltpu.roll(a, HEAD_DIM, 1), c)
            col = (NSA_GROUP * h + 2 * pair) * HEAD_DIM
            y_ref[0, :, col:col + LANES] = (both * _silu(z_ref[0, :, col:col + LANES])).astype(BF16)


def _nsa_prompt(rel_bias, q, ck, cv, ks, vs_t, kw, vw_t, e_mat, gates, z):
    b, t, _ = q.shape
    tq = Q_TILE
    n_cmp = ck.shape[1]
    assert t % SLC_KEYS == 0 and SLC_KEYS % tq == 0 and WINDOW % tq == 0 and t >= WINDOW + tq
    assert t // SLC_BLOCK <= LANES
    _, _, n_band = _band_geometry(tq)
    tile = lambda width: pl.BlockSpec((1, tq, width), lambda i, j: (j, i, 0))
    seq = lambda rows: pl.BlockSpec((1, rows, LANES), lambda i, j: (j, 0, 0))
    seq_t = pl.BlockSpec((1, t // LANES, LANES, LANES), lambda i, j: (j, 0, 0, 0))
    assert tq == LANES
    return pl.pallas_call(
        _nsa_prompt_kernel,
        grid=(t // tq, b),
        in_specs=[pl.BlockSpec(memory_space=pltpu.SMEM), tile(NSA_WIDTH), seq(n_cmp), seq(n_cmp),
                  seq(t), seq_t, seq(t), seq_t, pl.BlockSpec((t, LANES), lambda i, j: (0, 0)),
                  tile(LANES), tile(NSA_WIDTH)],
        out_specs=tile(NSA_WIDTH),
        out_shape=jax.ShapeDtypeStruct((b, t, NSA_WIDTH), BF16),
        scratch_shapes=[pltpu.VMEM((NSA_HEADS, tq, n_cmp), F32),
                        pltpu.VMEM((n_band, NSA_HEADS, tq, tq), F32),
                        pltpu.VMEM((NSA_KV_HEADS, SLC_KEYS, NSA_GROUP * tq), F32),
                        pltpu.VMEM((NSA_KV_HEADS, SLC_KEYS, NSA_GROUP * tq), BF16),
                        pltpu.VMEM((NSA_KV_HEADS, KV_WIDTH, NSA_GROUP * tq), F32),
                        pltpu.VMEM((NSA_KV_HEADS, WINDOW + tq, NSA_GROUP * tq), F32)],
        compiler_params=pltpu.CompilerParams(dimension_semantics=("arbitrary", "arbitrary"),
                                             vmem_limit_bytes=VMEM_LIMIT),
        name="nsa_prompt",
    )(rel_bias, q, ck, cv, ks, vs_t, kw, vw_t, e_mat, gates, z)


def _nsa_sample_kernel(pt_ref, rel_ref, q_ref, ck_ref, cv_ref, pk_ref, pv_ref, e_ref, kwc_ref, vwc_ref, new_ref,
                       g_ref, z_ref, y_ref, kbuf, vbuf, kaug_ref, biasc_ref, biass_ref, biasw_ref, ksem, vsem):
    page_rows = pk_ref.shape[2]
    past = kbuf.shape[2]
    n_cmp = ck_ref.shape[1]
    n_slc = n_cmp // (SLC_BLOCK // CMP_BLOCK)
    win_rows = kwc_ref.shape[2]
    heads = tuple(range(NSA_HEADS))

    @pl.when(pl.program_id(0) == 0)
    def _():
        ccol = lax.broadcasted_iota(jnp.int32, (1, n_cmp), 1)
        cblock = (SLC_BLOCK // CMP_BLOCK) * (ccol % n_slc) + ccol // n_slc
        cdist = past - (CMP_BLOCK * cblock + CMP_BLOCK - 1)
        biasc_ref[...] = jnp.concatenate(_rel_bias_rows(cdist, rel_ref, heads), axis=0)
        sdist = past - lax.broadcasted_iota(jnp.int32, (1, past), 1)
        biass_ref[...] = jnp.concatenate(_rel_bias_rows(sdist, rel_ref, heads), axis=0)
        wdist = win_rows - lax.broadcasted_iota(jnp.int32, (1, win_rows), 1)
        biasw_ref[...] = jnp.concatenate(_rel_bias_rows(wdist, rel_ref, heads), axis=0)
        kaug_ref[KV_WIDTH:, :] = e_ref[...]

    slot = _double_buffered_gather(pt_ref, (pk_ref, pv_ref), (kbuf, vbuf), (ksem, vsem), past // page_rows)

    q8 = q_ref[0]
    q8f = q8.astype(F32)
    bias0 = jnp.concatenate(
        [jnp.full((1, 1), rel_ref[0, hd] - rel_ref[REL_BUCKETS - 1, hd], F32) for hd in heads], axis=0)
    lane = lax.broadcasted_iota(jnp.int32, (NSA_KV_HEADS, LANES), 1)

    s_c = _dot_nt(q8, ck_ref[0]) + biasc_ref[...]
    e_c = jnp.exp(s_c - jnp.max(s_c, axis=1, keepdims=True))
    p_c = e_c / jnp.sum(e_c, axis=1, keepdims=True)
    o_c = _dot(p_c.astype(BF16), cv_ref[0])

    imps = []
    for h in range(NSA_KV_HEADS):
        acc = p_c[NSA_GROUP * h:NSA_GROUP * h + 1]
        for g in range(1, NSA_GROUP):
            acc = acc + p_c[NSA_GROUP * h + g:NSA_GROUP * h + g + 1]
        imps.append(acc)
    imp = jnp.concatenate(imps, axis=0)
    imp = imp[:, :n_slc] + imp[:, n_slc:]
    if n_slc < LANES:
        imp = jnp.concatenate([imp, jnp.zeros((NSA_KV_HEADS, LANES - n_slc), F32)], axis=1)
    forced = (lane == 0) | (lane == n_slc - 1)
    score = jnp.where(lane < n_slc, jnp.where(forced, jnp.inf, imp), -jnp.inf)
    sel = _top_blocks(score, SLC_TOPN - 1)
    neg_mask = (sel - 1.0).astype(BF16)
    neg_rows = jnp.concatenate([neg_mask[h:h + 1] for h in range(NSA_KV_HEADS) for _ in range(NSA_GROUP)], axis=0)
    q_aug = jnp.concatenate([q8, neg_rows], axis=1)

    def with_new_token(s, s_new, v_t, v_new):
        m = jnp.maximum(jnp.max(s, axis=1, keepdims=True), s_new)
        p = jnp.exp(s - m)
        p_new = jnp.exp(s_new - m)
        total = jnp.sum(p, axis=1, keepdims=True) + p_new
        return (_dot_nt(p.astype(BF16), v_t) + p_new.astype(BF16).astype(F32) * v_new.astype(F32)) / total

    kaug_ref[:KV_WIDTH, :] = kbuf[slot].astype(BF16)
    s_s = _dot(q_aug, kaug_ref[...]) + biass_ref[...]
    s_new = jnp.sum(q8f * new_ref[0, 0:1, :].astype(F32), axis=1, keepdims=True) + bias0
    o_s = with_new_token(s_s, s_new, vbuf[slot].astype(BF16), new_ref[0, 1:2, :])

    wcol = lax.broadcasted_iota(jnp.int32, (1, win_rows), 1)
    s_w = _dot(q8, kwc_ref[0].astype(BF16)) + biasw_ref[...]
    s_w = jnp.where(win_rows - wcol < WINDOW, s_w, MASK_VALUE)
    s_new = jnp.sum(q8f * new_ref[0, 2:3, :].astype(F32), axis=1, keepdims=True) + bias0
    o_w = with_new_token(s_w, s_new, vwc_ref[0].astype(BF16), new_ref[0, 3:4, :])

    gates = g_ref[0]
    low = []
    for hd in heads:
        r = (gates[:, hd:hd + 1] * o_c[hd:hd + 1]
             + gates[:, NSA_HEADS + hd:NSA_HEADS + hd + 1] * o_s[hd:hd + 1]
             + gates[:, 2 * NSA_HEADS + hd:2 * NSA_HEADS + hd + 1] * o_w[hd:hd + 1])
        low.append(r if hd < NSA_GROUP else pltpu.roll(r, HEAD_DIM, 1))
    lane1 = lax.broadcasted_iota(jnp.int32, (1, LANES), 1)
    for pair in range(NSA_HEADS // 2):
        both = jnp.where(lane1 < HEAD_DIM, low[2 * pair], pltpu.roll(low[2 * pair + 1], HEAD_DIM, 1))
        cols = slice(LANES * pair, LANES * (pair + 1))
        y_ref[0, :, cols] = (both * _silu(z_ref[0, :, cols])).astype(BF16)


def _nsa_sample(page_table, rel_bias, q8, ck, cv, pool_k, pool_v, e_mat, win_k, win_v, new_rows, gates, z):
    n_seq, n_pages = page_table.shape
    page_rows = pool_k.shape[2]
    past = n_pages * page_rows
    n_cmp = ck.shape[1]
    win_rows = win_k.shape[2]
    assert past % SLC_BLOCK == 0 and past // SLC_BLOCK <= LANES and page_rows % LANES == 0
    hbm = pl.BlockSpec(memory_space=pl.ANY)
    per_seq = lambda r, w: pl.BlockSpec((1, r, w), lambda i, pt: (i, 0, 0))
    return pl.pallas_call(
        _nsa_sample_kernel,
        grid_spec=pltpu.PrefetchScalarGridSpec(
            num_scalar_prefetch=1,
            grid=(n_seq,),
            in_specs=[pl.BlockSpec(memory_space=pltpu.SMEM), per_seq(NSA_HEADS, LANES), per_seq(n_cmp, LANES),
                      per_seq(n_cmp, LANES), hbm, hbm,
                      pl.BlockSpec((LANES, past), lambda i, pt: (0, 0)),
                      per_seq(KV_WIDTH, win_rows), per_seq(KV_WIDTH, win_rows), per_seq(4, LANES),
                      per_seq(1, LANES), per_seq(1, NSA_WIDTH)],
            out_specs=per_seq(1, NSA_WIDTH),
            scratch_shapes=[pltpu.VMEM((2, KV_WIDTH, past), F32), pltpu.VMEM((2, KV_WIDTH, past), F32),
                            pltpu.VMEM((KV_WIDTH + LANES, past), BF16),
                            pltpu.VMEM((NSA_HEADS, n_cmp), F32), pltpu.VMEM((NSA_HEADS, past), F32),
                            pltpu.VMEM((NSA_HEADS, win_rows), F32),
                            pltpu.SemaphoreType.DMA((2,)), pltpu.SemaphoreType.DMA((2,))]),
        out_shape=jax.ShapeDtypeStruct((n_seq, 1, NSA_WIDTH), BF16),
        compiler_params=pltpu.CompilerParams(dimension_semantics=("arbitrary",), vmem_limit_bytes=VMEM_LIMIT),
        name="nsa_sample",
    )(page_table.reshape(-1), rel_bias, q8, ck, cv, pool_k, pool_v, e_mat, win_k, win_v, new_rows, gates, z)


def _group_norm_gate(o, z, gn, seg):
    mu = _segment_sum(o, seg) * (1.0 / RET_VDIM)
    d = o - mu
    var = _segment_sum(d * d, seg) * (1.0 / RET_VDIM)
    return (d * lax.rsqrt(var + EPS) * gn * _silu(z)).astype(BF16)


def _retention_prompt_kernel(q_ref, k_ref, v_ref, z_ref, gn_ref, seg_ref, dec_ref, qdec_ref, kdec_ref, cdec_ref,
                             y_ref, st_ref, state):
    chunk = pl.program_id(1)
    tc = q_ref.shape[1]
    lane = lax.broadcasted_iota(jnp.int32, (tc, LANES), 1)
    srow = lax.broadcasted_iota(jnp.int32, (LANES, LANES), 0)
    scol = lax.broadcasted_iota(jnp.int32, (LANES, LANES), 1)
    same_head = (srow // RET_KDIM) == (scol // RET_VDIM)

    @pl.when(chunk == 0)
    def _():
        state[...] = jnp.zeros_like(state)

    for pair in range(RET_HEADS // 2):
        cols = slice(LANES * pair, LANES * (pair + 1))
        q = q_ref[0, :, cols]
        k = k_ref[0, :, cols]
        v = v_ref[0, :, cols]
        s_old = state[pair]
        cross = _dot(q, s_old.astype(BF16)) * qdec_ref[:, cols]
        halves = []
        for e in range(2):
            qe = jnp.where((lane // RET_KDIM) == e, q, jnp.zeros((), BF16))
            scores = _dot_nt(qe, k) * dec_ref[2 * pair + e]
            halves.append(_dot(scores.astype(BF16), v))
        o = jnp.where(lane < RET_VDIM, halves[0], halves[1]) + cross
        y_ref[0, :, cols] = _group_norm_gate(o, z_ref[0, :, cols], gn_ref[:, cols], seg_ref[...])
        kd_t = (k.astype(F32) * kdec_ref[:, cols]).T.astype(BF16)
        s_new = s_old * cdec_ref[:, cols] + jnp.where(same_head, _dot(kd_t, v), 0.0)
        state[pair] = s_new

    @pl.when(chunk == pl.num_programs(1) - 1)
    def _():
        for head in range(RET_HEADS):
            e = head % 2
            st_ref[0, head] = state[head // 2, RET_KDIM * e:RET_KDIM * (e + 1), RET_VDIM * e:RET_VDIM * (e + 1)]


def _retention_tables(tc):
    log_g = jnp.asarray(RET_GAMMA_LOG, F32)
    i = jnp.arange(tc, dtype=F32)
    diff = i[:, None] - i[None, :]
    decay = jnp.where(diff >= 0, jnp.exp(jnp.maximum(diff, 0.0)[None] * log_g[:, None, None]), 0.0)
    widen = lambda a: jnp.repeat(a, RET_VDIM, axis=-1)
    q_decay = widen(jnp.exp((i[:, None] + 1.0) * log_g[None, :]))
    k_decay = widen(jnp.exp((tc - 1.0 - i)[:, None] * log_g[None, :]))
    c_decay = widen(jnp.exp(tc * log_g)[None, :])
    return decay, q_decay, k_decay, c_decay


def _retention_prompt(q, k, v, z, gn, seg):
    b, t, _ = q.shape
    tc = RET_TILE
    assert t % tc == 0
    decay, q_decay, k_decay, c_decay = _retention_tables(tc)
    tile = pl.BlockSpec((1, tc, RET_WIDTH), lambda i, j: (i, j, 0))
    const = lambda a: pl.BlockSpec(a.shape, lambda i, j: (0,) * a.ndim)
    return pl.pallas_call(
        _retention_prompt_kernel,
        grid=(b, t // tc),
        in_specs=[tile, tile, tile, tile, const(gn), const(seg), const(decay), const(q_decay), const(k_decay),
                  const(c_decay)],
        out_specs=[tile, pl.BlockSpec((1, RET_HEADS, RET_KDIM, RET_VDIM), lambda i, j: (i, 0, 0, 0))],
        out_shape=[jax.ShapeDtypeStruct((b, t, RET_WIDTH), BF16),
                   jax.ShapeDtypeStruct((b, RET_HEADS, RET_KDIM, RET_VDIM), F32)],
        scratch_shapes=[pltpu.VMEM((RET_HEADS // 2, LANES, LANES), F32)],
        compiler_params=pltpu.CompilerParams(dimension_semantics=("arbitrary", "arbitrary"),
                                             vmem_limit_bytes=VMEM_LIMIT),
        name="retention_prompt",
    )(q, k, v, z, gn, seg, decay, q_decay, k_decay, c_decay)


def _retention_sample_kernel(qt_ref, kt_ref, v_ref, z_ref, gn_ref, gam_ref, st_ref, y_ref, so_ref):
    n = v_ref.shape[1]
    qt = qt_ref[0]
    kt = kt_ref[0]
    gam = gam_ref[...]
    qk = jnp.sum(qt * kt, axis=1)
    for s in range(n):
        st = st_ref[s]
        vs = v_ref[0, s]
        cross = jnp.sum(qt[:, :, s:s + 1] * st, axis=1)
        so_ref[s] = st * gam[:, :, None] + kt[:, :, s:s + 1] * vs[:, None, :]
        o = qk[:, s:s + 1] * vs + gam * cross
        mu = jnp.mean(o, axis=-1, keepdims=True)
        d = o - mu
        var = jnp.mean(d * d, axis=-1, keepdims=True)
        y_ref[0, s] = (d * lax.rsqrt(var + EPS) * gn_ref[...] * _silu(z_ref[0, s])).astype(BF16)


def _retention_sample(q, k, v, z, gn, state):
    n_seq = q.shape[0]
    n = min(RET_SAMPLE_SEQS, n_seq)
    assert n_seq % n == 0
    steps = n_seq // n
    to_cols = lambda a: a.astype(F32).reshape(steps, n, RET_HEADS, RET_KDIM).transpose(0, 2, 3, 1)
    to_rows = lambda a: a.astype(F32).reshape(steps, n, RET_HEADS, RET_VDIM)
    gam = jnp.exp(jnp.asarray(RET_GAMMA_LOG, F32)).reshape(RET_HEADS, 1)
    cols = pl.BlockSpec((1, RET_HEADS, RET_KDIM, n), lambda i: (i, 0, 0, 0))
    rws = pl.BlockSpec((1, n, RET_HEADS, RET_VDIM), lambda i: (i, 0, 0, 0))
    st = pl.BlockSpec((n, RET_HEADS, RET_KDIM, RET_VDIM), lambda i: (i, 0, 0, 0))
    y, new_state = pl.pallas_call(
        _retention_sample_kernel,
        grid=(steps,),
        in_specs=[cols, cols, rws, rws, pl.BlockSpec((RET_HEADS, RET_VDIM), lambda i: (0, 0)),
                  pl.BlockSpec((RET_HEADS, 1), lambda i: (0, 0)), st],
        out_specs=[rws, st],
        out_shape=[jax.ShapeDtypeStruct((steps, n, RET_HEADS, RET_VDIM), BF16),
                   jax.ShapeDtypeStruct(state.shape, F32)],
        compiler_params=pltpu.CompilerParams(dimension_semantics=("parallel",), vmem_limit_bytes=VMEM_LIMIT),
        name="retention_sample",
    )(to_cols(q), to_cols(k), to_rows(v), to_rows(z), gn.reshape(RET_HEADS, RET_VDIM), gam, state)
    return y.reshape(n_seq, RET_WIDTH), new_state


def _out_kernel(x_ref, yn_ref, yr_ref, w_ref, o_ref):
    y = jnp.concatenate([yn_ref[0], yr_ref[0]], axis=1)
    o_ref[0] = x_ref[0] + _dot(y, w_ref[...])


def _out_project(x, y_nsa, y_ret, w):
    nb, rows, d = x.shape
    tm = min(PROJ_ROWS, rows)
    spec = lambda width: pl.BlockSpec((1, tm, width), lambda i, j: (i, j, 0))
    return pl.pallas_call(
        _out_kernel,
        grid=(nb, rows // tm),
        in_specs=[spec(d), spec(NSA_WIDTH), spec(RET_WIDTH), pl.BlockSpec(w.shape, lambda i, j: (0, 0))],
        out_specs=spec(d),
        out_shape=jax.ShapeDtypeStruct(x.shape, F32),
        compiler_params=pltpu.CompilerParams(dimension_semantics=("parallel", "parallel"),
                                             vmem_limit_bytes=VMEM_LIMIT),
        name="output_projection",
    )(x, y_nsa, y_ret, w)


def _rotary_tables(pos):
    half = RET_KDIM // 2
    inv = ROPE_BASE ** (-jnp.arange(half, dtype=F32) / half)
    ang = pos.astype(F32)[:, None] * inv[None, :]
    cos, sin = jnp.cos(ang), jnp.sin(ang)
    reps = LANES // RET_KDIM
    return jnp.tile(jnp.concatenate([cos, cos], axis=1), (1, reps)), jnp.tile(jnp.concatenate([-sin, sin], axis=1), (1, reps))


def _arrange_w_in(w_in):
    d = w_in.shape[0]
    parts = jnp.split(w_in, np.cumsum(SPLIT_WIDTHS)[:-1].tolist(), axis=1)
    order = [hd for g in range(NSA_GROUP) for hd in (g, NSA_GROUP + g)]
    q = parts[0].reshape(d, NSA_HEADS, HEAD_DIM)[:, order].reshape(d, NSA_WIDTH)
    gates = parts[7].reshape(d, NSA_HEADS, 3).transpose(0, 2, 1).reshape(d, 3 * NSA_HEADS)
    gates = jnp.pad(gates, ((0, 0), (0, LANES - 3 * NSA_HEADS)))
    w = jnp.concatenate([q] + parts[1:7] + parts[8:] + [gates], axis=1)
    assert w.shape[1] == COL_END
    return w.astype(BF16)


def _compress_weights(pe, w1, w2):
    w1r = w1.reshape(CMP_BLOCK, HEAD_DIM, HEAD_DIM)
    w1_both = jnp.zeros((CMP_BLOCK, NSA_KV_HEADS, HEAD_DIM, NSA_KV_HEADS, HEAD_DIM), F32)
    w2_both = jnp.zeros((NSA_KV_HEADS, HEAD_DIM, NSA_KV_HEADS, HEAD_DIM), F32)
    for h in range(NSA_KV_HEADS):
        w1_both = w1_both.at[:, h, :, h, :].set(w1r)
        w2_both = w2_both.at[h, :, h, :].set(w2)
    return (jnp.tile(pe, (1, NSA_KV_HEADS)), w1_both.reshape(CMP_BLOCK * KV_WIDTH, KV_WIDTH).astype(BF16),
            w2_both.reshape(KV_WIDTH, KV_WIDTH).astype(BF16))


def _even_odd(c):
    n, nc, w = c.shape
    ratio = SLC_BLOCK // CMP_BLOCK
    return c.reshape(n, nc // ratio, ratio, w).transpose(0, 2, 1, 3).reshape(n, nc, w)


def _block_membership(n_keys):
    k = jnp.arange(n_keys)[:, None] // SLC_BLOCK
    return jnp.where(k == jnp.arange(LANES)[None, :], -MASK_VALUE, 0.0).astype(BF16)


def kernel(x_prompt, x_sample, cache_cmp_k, cache_cmp_v, cache_slc_k, cache_slc_v, cache_win_k, cache_win_v,
           state_ret, page_table, norm_g, w_in, q_norm_g, k_norm_g, cmp_pe_k, cmp_w1_k, cmp_w2_k, cmp_pe_v,
           cmp_w1_v, cmp_w2_v, rel_bias, ret_gn_g, w_out):
    b, t, d = x_prompt.shape
    n_seq, dec_len, _ = x_sample.shape
    n_pages = page_table.shape[1]
    page_rows = cache_cmp_k.shape[1]
    past = n_pages * page_rows
    assert dec_len == 1 and past % CMP_BLOCK == 0
    kv4 = lambda a: a.reshape(a.shape[0], a.shape[1], NSA_KV_HEADS, HEAD_DIM)

    w = _arrange_w_in(w_in)
    w_o = w_out.astype(BF16)
    ng = norm_g.reshape(1, d)
    qg = (jnp.tile(q_norm_g, LANES // HEAD_DIM) * ATTN_SCALE).reshape(1, LANES)
    kg = jnp.tile(k_norm_g, LANES // HEAD_DIM).reshape(1, LANES)
    gn = ret_gn_g.reshape(1, RET_WIDTH)
    lane = np.arange(LANES)
    seg = jnp.asarray(lane[:, None] // HEAD_DIM == lane[None, :] // HEAD_DIM, BF16)
    cw = _compress_weights(cmp_pe_k, cmp_w1_k, cmp_w2_k) + _compress_weights(cmp_pe_v, cmp_w1_v, cmp_w2_v)

    cos, sin = _rotary_tables(jnp.arange(t))
    (q, kc, vc, ks, vs, kw, vw, ksb, _, kwb, _, gates, zn, qr, kr, vr, zr, vs_t, vw_t) = _project(
        x_prompt, cos, sin, ng, w, qg, kg, seg)
    ck, cv = _compress_prompt(kc, vc, cw, kg, seg)
    y_nsa = _nsa_prompt(rel_bias, q, _even_odd(ck), _even_odd(cv), ksb, vs_t, kwb, vw_t, _block_membership(t), gates, zn)
    y_ret, p_ret = _retention_prompt(qr, kr, vr, zr, gn, seg)
    y_prompt = _out_project(x_prompt, y_nsa, y_ret, w_o)
    keep = min(WINDOW, t)
    prompt_out = (y_prompt, kv4(kc), kv4(vc), kv4(ks), kv4(vs), kv4(kw[:, t - keep:]), kv4(vw[:, t - keep:]), p_ret)

    cos, sin = _rotary_tables(jnp.full((n_seq,), past))
    xs = x_sample.reshape(1, n_seq, d)
    (q, kc, vc, ks, vs, kw, vw, ksb, vsb, kwb, vwb, gates, zn, qr, kr, vr, zr) = [
        a[0] for a in _project(xs, cos, sin, ng, w, qg, kg, seg)[:17]]
    chan = lambda a: jnp.transpose(a, (0, 2, 3, 1)).reshape(a.shape[0], KV_WIDTH, a.shape[1])
    ck, cv = _compress_sample(page_table, chan(cache_cmp_k), chan(cache_cmp_v), cw, kg, seg)
    half = jnp.asarray(lane[None, :] // HEAD_DIM == (np.arange(NSA_HEADS) // NSA_GROUP)[:, None])
    q8 = jnp.where(half[None], jnp.tile(q.reshape(n_seq, NSA_GROUP, LANES), (1, NSA_KV_HEADS, 1)), jnp.zeros((), BF16))
    new_rows = jnp.stack([ksb, vsb, kwb, vwb], axis=1)
    y_nsa = _nsa_sample(page_table, rel_bias, q8, _even_odd(ck), _even_odd(cv), chan(cache_slc_k), chan(cache_slc_v),
                        _block_membership(past).T, chan(cache_win_k), chan(cache_win_v), new_rows,
                        gates.reshape(n_seq, 1, LANES), zn.reshape(n_seq, 1, NSA_WIDTH))
    y_ret, s_ret = _retention_sample(qr, kr, vr, zr, gn, state_ret)
    y_sample = _out_project(xs, y_nsa.reshape(1, n_seq, NSA_WIDTH), y_ret.reshape(1, n_seq, RET_WIDTH), w_o)
    keep = min(WINDOW, cache_win_k.shape[1] + 1)
    new4 = lambda a: a.reshape(n_seq, 1, NSA_KV_HEADS, HEAD_DIM)
    s_win_k = jnp.concatenate([cache_win_k, new4(kw)], axis=1)[:, -keep:]
    s_win_v = jnp.concatenate([cache_win_v, new4(vw)], axis=1)[:, -keep:]
    sample_out = (y_sample.reshape(n_seq, 1, d), new4(kc), new4(vc), new4(ks), new4(vs), s_win_k, s_win_v, s_ret)

    return (prompt_out[0], sample_out[0]) + prompt_out[1:] + sample_out[1:]
```

```python
import functools

import numpy as np
import jax
import jax.numpy as jnp
from jax import lax
from jax.experimental import pallas as pl
from jax.experimental.pallas import tpu as pltpu

F32, BF16 = jnp.float32, jnp.bfloat16

NSA_HEADS = 8
NSA_KV_HEADS = 2
HEAD_DIM = 64
NSA_GROUP = NSA_HEADS // NSA_KV_HEADS
NSA_WIDTH = NSA_HEADS * HEAD_DIM
KV_WIDTH = NSA_KV_HEADS * HEAD_DIM
CMP_BLOCK = 32
SLC_BLOCK = 64
SLC_TOPN = 16
WINDOW = 512
ATTN_SCALE = HEAD_DIM ** -0.5
RET_HEADS = 8
RET_KDIM = 64
RET_VDIM = 64
RET_WIDTH = RET_HEADS * RET_VDIM
ROPE_BASE = 10000.0
REL_BUCKETS = 32
REL_MAX_DIST = 1024
EPS = 1e-6
SPLIT_WIDTHS = (NSA_WIDTH, KV_WIDTH, KV_WIDTH, KV_WIDTH, KV_WIDTH, KV_WIDTH, KV_WIDTH,
                3 * NSA_HEADS, NSA_WIDTH, RET_HEADS * RET_KDIM, RET_HEADS * RET_KDIM, RET_WIDTH, RET_WIDTH)

LANES = 128
VMEM_LIMIT = 56 * 1024 * 1024

PROJ_ROWS = 512
Q_TILE = 128
SLC_KEYS = 512
RET_TILE = 256
RET_SAMPLE_SEQS = 16
CMP_PITCH = CMP_BLOCK + 4

MASK_VALUE = -float(2 ** 30)

MXU_COLS = 256
COL_Q = 0
COL_KC = COL_Q + NSA_WIDTH
COL_VC = COL_KC + LANES
COL_KS = COL_VC + LANES
COL_VS = COL_KS + LANES
COL_KW = COL_VS + LANES
COL_VW = COL_KW + LANES
COL_ZN = COL_VW + LANES
COL_QR = COL_ZN + NSA_WIDTH
COL_KR = COL_QR + RET_WIDTH
COL_VR = COL_KR + RET_WIDTH
COL_ZR = COL_VR + RET_WIDTH
COL_G = COL_ZR + RET_WIDTH
COL_END = COL_G + LANES


def _bucket_lower_bounds():
    exact = REL_BUCKETS // 2
    ratio = REL_MAX_DIST // exact
    lows = list(range(exact))
    n = exact
    for k in range(REL_BUCKETS - exact):
        while n ** (REL_BUCKETS - exact) < exact ** (REL_BUCKETS - exact) * ratio ** k:
            n += 1
        lows.append(n)
    return tuple(lows)


BUCKET_LOW = _bucket_lower_bounds()
FAR_DIST = BUCKET_LOW[-1]
RET_GAMMA_LOG = tuple(float(np.log1p(-np.exp2(-5.0 - h))) for h in range(RET_HEADS))


def _dot(a, b):
    return jnp.dot(a, b, preferred_element_type=F32)


def _dot_nt(a, b):
    return lax.dot_general(a, b, (((1,), (1,)), ((), ())), preferred_element_type=F32)


def _segment_sum(v, seg):
    hi = v.astype(BF16)
    lo = (v - hi.astype(F32)).astype(BF16)
    return _dot(hi, seg) + _dot(lo, seg)


def _head_rms(y, g, seg):
    ms = _segment_sum(y * y, seg) * (1.0 / HEAD_DIM)
    return y * lax.rsqrt(ms + EPS) * g


def _silu(x):
    return x * jax.nn.sigmoid(x)


def _rel_bias_rows(dist, rel_ref, heads):
    out = [jnp.full(dist.shape, rel_ref[0, hd], F32) for hd in heads]
    for bkt in range(1, REL_BUCKETS):
        hit = dist >= BUCKET_LOW[bkt]
        out = [jnp.where(hit, rel_ref[bkt, hd], o) for hd, o in zip(heads, out)]
    return [o - rel_ref[REL_BUCKETS - 1, hd] for hd, o in zip(heads, out)]


def _top_blocks_cols(score, n_pick):
    n_rows = score.shape[0]
    rowi = lax.broadcasted_iota(jnp.int32, score.shape, 0).astype(F32)
    sel = jnp.zeros(score.shape, F32)
    neg_inf = -jnp.inf
    for _ in range(n_pick):
        m = jnp.max(score, axis=0, keepdims=True)
        first = jnp.min(jnp.where(score == m, rowi, float(n_rows)), axis=0, keepdims=True)
        hit = rowi == first
        sel = sel + jnp.where(hit, jnp.where(m > neg_inf, 1.0, 0.0), 0.0)
        score = jnp.where(hit, neg_inf, score)
    return sel


def _proj_kernel(x_ref, cos_ref, sin_ref, ng_ref, w_ref, qg_ref, kg_ref, seg_ref,
                 q_o, kc_o, vc_o, ks_o, vs_o, kw_o, vw_o, ksb_o, vsb_o, kwb_o, vwb_o,
                 g_o, zn_o, qr_o, kr_o, vr_o, zr_o, vst_o, vwt_o):
    x = x_ref[0]
    inv = lax.rsqrt(jnp.mean(x * x, axis=-1, keepdims=True) + EPS)
    xn = (x * inv * ng_ref[...]).astype(BF16)
    seg = seg_ref[...]

    def mm(col, width):
        return _dot(xn, w_ref[:, col:col + width])

    def chunk(y, c):
        return y[:, LANES * c:LANES * (c + 1)]

    y = mm(COL_Q, NSA_WIDTH)
    for c in range(NSA_WIDTH // LANES):
        q_o[0, :, LANES * c:LANES * (c + 1)] = _head_rms(chunk(y, c), qg_ref[...], seg).astype(BF16)
    y = mm(COL_KC, 2 * LANES)
    kc_o[0] = chunk(y, 0)
    vc_o[0] = chunk(y, 1)
    for col, k_o, v_o, kb_o, vb_o, vt_o in ((COL_KS, ks_o, vs_o, ksb_o, vsb_o, vst_o),
                                            (COL_KW, kw_o, vw_o, kwb_o, vwb_o, vwt_o)):
        y = mm(col, 2 * LANES)
        k = _head_rms(chunk(y, 0), kg_ref[...], seg)
        k_o[0] = k
        kb_o[0] = k.astype(BF16)
        v = chunk(y, 1)
        v_o[0] = v
        vb_o[0] = v.astype(BF16)
        for c in range(v.shape[0] // LANES):
            vt_o[0, c] = v[LANES * c:LANES * (c + 1)].T.astype(BF16)
    zn_o[0] = mm(COL_ZN, NSA_WIDTH)

    cos = cos_ref[...]
    sin = sin_ref[...]
    lane = lax.broadcasted_iota(jnp.int32, cos.shape, 1)
    first_half = (lane % HEAD_DIM) < (HEAD_DIM // 2)
    for col, o_ref, scale in ((COL_QR, qr_o, None), (COL_KR, kr_o, RET_KDIM ** -0.5)):
        y = mm(col, RET_WIDTH)
        for c in range(RET_WIDTH // LANES):
            yc = chunk(y, c)
            partner = jnp.where(first_half, pltpu.roll(yc, LANES - HEAD_DIM // 2, 1),
                                pltpu.roll(yc, HEAD_DIM // 2, 1))
            r = yc * cos + partner * sin
            if scale is not None:
                r = r * scale
            o_ref[0, :, LANES * c:LANES * (c + 1)] = r.astype(BF16)
    vr_o[0] = mm(COL_VR, RET_WIDTH).astype(BF16)
    zr_o[0] = mm(COL_ZR, RET_WIDTH)
    g_o[0] = jax.nn.sigmoid(mm(COL_G, LANES))


def _project(x, cos, sin, norm_g, w, qg, kg, seg):
    nb, rows, d = x.shape
    tm = min(PROJ_ROWS, rows)
    assert rows % tm == 0 and tm % LANES == 0
    row_spec = lambda width: pl.BlockSpec((1, tm, width), lambda i, j: (i, j, 0))
    const = lambda shape: pl.BlockSpec(shape, lambda i, j: (0,) * len(shape))
    outs = [(NSA_WIDTH, BF16)] + [(LANES, F32)] * 6 + [(LANES, BF16)] * 4 + [(LANES, F32), (NSA_WIDTH, F32),
            (RET_WIDTH, BF16), (RET_WIDTH, BF16), (RET_WIDTH, BF16), (RET_WIDTH, F32)]
    blocks_spec = pl.BlockSpec((1, tm // LANES, LANES, LANES), lambda i, j: (i, j, 0, 0))
    blocks_shape = jax.ShapeDtypeStruct((nb, rows // LANES, LANES, LANES), BF16)
    return pl.pallas_call(
        _proj_kernel,
        grid=(nb, rows // tm),
        in_specs=[row_spec(d), pl.BlockSpec((tm, LANES), lambda i, j: (j, 0)),
                  pl.BlockSpec((tm, LANES), lambda i, j: (j, 0)), const((1, d)), const(w.shape),
                  const((1, LANES)), const((1, LANES)), const((LANES, LANES))],
        out_specs=[row_spec(wd) for wd, _ in outs] + [blocks_spec] * 2,
        out_shape=[jax.ShapeDtypeStruct((nb, rows, wd), dt) for wd, dt in outs] + [blocks_shape] * 2,
        compiler_params=pltpu.CompilerParams(dimension_semantics=("parallel", "parallel"),
                                             vmem_limit_bytes=VMEM_LIMIT),
        name="input_projection",
    )(x, cos, sin, norm_g, w, qg, kg, seg)


def _compress_rows(buf_ref, pe_ref, w1_ref, w2_ref, n_blk, pitch=CMP_BLOCK):
    parts = []
    for i in range(CMP_BLOCK):
        rows = buf_ref[pl.ds(i, n_blk, stride=pitch), :] + pe_ref[i:i + 1, :]
        parts.append(rows.astype(BF16))
    flat = jnp.concatenate(parts, axis=1)
    hidden = _silu(_dot(flat, w1_ref[...]))
    return _dot(hidden.astype(BF16), w2_ref[...])


def _compress_kernel(kc_ref, vc_ref, pek_ref, w1k_ref, w2k_ref, pev_ref, w1v_ref, w2v_ref, kg_ref, seg_ref,
                     ck_o, cv_o):
    n_blk = ck_o.shape[1]
    ck = _compress_rows(kc_ref.at[0], pek_ref, w1k_ref, w2k_ref, n_blk)
    ck_o[0] = _head_rms(ck, kg_ref[...], seg_ref[...]).astype(BF16)
    cv_o[0] = _compress_rows(vc_ref.at[0], pev_ref, w1v_ref, w2v_ref, n_blk).astype(BF16)


def _compress_prompt(kc, vc, cw, kg, seg):
    b, t, _ = kc.shape
    n_blk = t // CMP_BLOCK
    seq = pl.BlockSpec((1, t, LANES), lambda i: (i, 0, 0))
    const = lambda a: pl.BlockSpec(a.shape, lambda i: (0,) * a.ndim)
    out = pl.BlockSpec((1, n_blk, LANES), lambda i: (i, 0, 0))
    return pl.pallas_call(
        _compress_kernel,
        grid=(b,),
        in_specs=[seq, seq] + [const(a) for a in cw] + [const(kg), const(seg)],
        out_specs=[out, out],
        out_shape=[jax.ShapeDtypeStruct((b, n_blk, LANES), BF16)] * 2,
        compiler_params=pltpu.CompilerParams(dimension_semantics=("parallel",), vmem_limit_bytes=VMEM_LIMIT),
        name="compress_prompt",
    )(kc, vc, *cw, kg, seg)


def _page_copy(pt_ref, pool_ref, buf_ref, sem, seq, page, n_pages):
    rows = pool_ref.shape[2]
    return pltpu.make_async_copy(pool_ref.at[pt_ref[seq * n_pages + page]],
                                 buf_ref.at[:, pl.ds(page * rows, rows)], sem)


def _gather_pages(pt_ref, pools, bufs, sems, seq, slot, n_pages, wait):
    for pool_ref, buf_ref, sem in zip(pools, bufs, sems):
        for page in range(n_pages):
            copy = _page_copy(pt_ref, pool_ref, buf_ref.at[slot], sem.at[slot], seq, page, n_pages)
            if wait:
                copy.wait()
            else:
                copy.start()


def _double_buffered_gather(pt_ref, pools, bufs, sems, n_pages):
    seq = pl.program_id(0)
    slot = seq % 2

    @pl.when(seq == 0)
    def _():
        _gather_pages(pt_ref, pools, bufs, sems, seq, slot, n_pages, wait=False)

    @pl.when(seq + 1 < pl.num_programs(0))
    def _():
        _gather_pages(pt_ref, pools, bufs, sems, seq + 1, 1 - slot, n_pages, wait=False)

    _gather_pages(pt_ref, pools, bufs, sems, seq, slot, n_pages, wait=True)
    return slot


def _compress_sample_kernel(pt_ref, pk_ref, pv_ref, pek_ref, w1k_ref, w2k_ref, pev_ref, w1v_ref, w2v_ref,
                            kg_ref, seg_ref, ck_o, cv_o, kbuf, vbuf, rows_ref, ksem, vsem):
    page_rows = pk_ref.shape[2]
    n_pages = kbuf.shape[2] // page_rows
    n_blk = ck_o.shape[1]
    slot = _double_buffered_gather(pt_ref, (pk_ref, pv_ref), (kbuf, vbuf), (ksem, vsem), n_pages)

    blocks_per_page = page_rows // CMP_BLOCK

    def token_major(buf):
        for page in range(n_pages):
            tokens = buf[slot, :, page * page_rows:(page + 1) * page_rows].T
            for c in range(blocks_per_page):
                r0 = (page * blocks_per_page + c) * CMP_PITCH
                rows_ref[r0:r0 + CMP_BLOCK, :] = tokens[c * CMP_BLOCK:(c + 1) * CMP_BLOCK]

    token_major(kbuf)
    ck = _compress_rows(rows_ref, pek_ref, w1k_ref, w2k_ref, n_blk, CMP_PITCH)
    ck_o[0] = _head_rms(ck, kg_ref[...], seg_ref[...]).astype(BF16)
    token_major(vbuf)
    cv_o[0] = _compress_rows(rows_ref, pev_ref, w1v_ref, w2v_ref, n_blk, CMP_PITCH).astype(BF16)


def _compress_sample(page_table, pool_k, pool_v, cw, kg, seg):
    n_seq, n_pages = page_table.shape
    page_rows = pool_k.shape[2]
    assert page_rows == LANES
    past = n_pages * page_rows
    n_blk = past // CMP_BLOCK
    hbm = pl.BlockSpec(memory_space=pl.ANY)
    const = lambda a: pl.BlockSpec(a.shape, lambda i, pt: (0,) * a.ndim)
    out = pl.BlockSpec((1, n_blk, LANES), lambda i, pt: (i, 0, 0))
    return pl.pallas_call(
        _compress_sample_kernel,
        grid_spec=pltpu.PrefetchScalarGridSpec(
            num_scalar_prefetch=1,
            grid=(n_seq,),
            in_specs=[hbm, hbm] + [const(a) for a in cw] + [const(kg), const(seg)],
            out_specs=[out, out],
            scratch_shapes=[pltpu.VMEM((2, KV_WIDTH, past), F32), pltpu.VMEM((2, KV_WIDTH, past), F32),
                            pltpu.VMEM((n_blk * CMP_PITCH, KV_WIDTH), F32),
                            pltpu.SemaphoreType.DMA((2,)), pltpu.SemaphoreType.DMA((2,))]),
        out_shape=[jax.ShapeDtypeStruct((n_seq, n_blk, LANES), BF16)] * 2,
        compiler_params=pltpu.CompilerParams(dimension_semantics=("arbitrary",), vmem_limit_bytes=VMEM_LIMIT),
        name="compress_sample",
    )(page_table.reshape(-1), pool_k, pool_v, *cw, kg, seg)


def _band_geometry(tq):
    near = -(-(FAR_DIST + tq - 1) // tq)
    rel_max = near + SLC_KEYS // tq - 2
    rel_min = -max(WINDOW // tq, SLC_KEYS // tq - 1)
    return near, rel_max, rel_max - rel_min + 1


def _nsa_prompt_kernel(rel_ref, q_ref, ck_ref, cv_ref, ks_ref, vst_ref, kw_ref, vwt_ref, e_ref, g_ref, z_ref,
                       y_ref, biasc_ref, band_ref, logit_ref, prob_ref, acc_ref, winb_ref, finb_ref):
    tq = q_ref.shape[1]
    n_cmp = ck_ref.shape[1]
    n_slc = n_cmp // (SLC_BLOCK // CMP_BLOCK)
    rows = NSA_GROUP * tq
    kt = SLC_KEYS
    chunks = kt // tq
    near, rel_max, n_band = _band_geometry(tq)
    win_tiles = WINDOW // tq
    win_keys = WINDOW + tq
    qb = pl.program_id(0)
    b = pl.program_id(1)
    t0 = qb * tq
    all_heads = tuple(range(NSA_HEADS))

    lane = lax.broadcasted_iota(jnp.int32, (tq, LANES), 1)
    row = lax.broadcasted_iota(jnp.int32, (tq, LANES), 0)

    @pl.when((qb == 0) & (b == 0))
    def _():
        for c in range(n_band):
            dist = lane - row + tq * (rel_max - c)
            for hd, bias in zip(all_heads, _rel_bias_rows(dist, rel_ref, all_heads)):
                band_ref[c, hd] = bias

    def band_bias(first_chunk, n_chunks, h):
        base = rel_max - qb + first_chunk
        return jnp.concatenate(
            [jnp.concatenate([band_ref[base + u, NSA_GROUP * h + g] for g in range(NSA_GROUP)], axis=1)
             for u in range(n_chunks)], axis=0)

    ccol = lax.broadcasted_iota(jnp.int32, (tq, n_cmp), 1)
    crow = lax.broadcasted_iota(jnp.int32, (tq, n_cmp), 0)
    cblock = (SLC_BLOCK // CMP_BLOCK) * (ccol % n_slc) + ccol // n_slc
    cdist = t0 + crow - (CMP_BLOCK * cblock + CMP_BLOCK - 1)

    @pl.when(b == 0)
    def _():
        for hd, bias in zip(all_heads, _rel_bias_rows(cdist, rel_ref, all_heads)):
            biasc_ref[hd] = bias

    q_all = q_ref[0]
    gates = g_ref[0]
    ck = ck_ref[0]
    cv = cv_ref[0]
    neg_inf = -jnp.inf
    def key_minus_query(n_keys):
        return (lax.broadcasted_iota(jnp.int32, (n_keys, rows), 0)
                - lax.broadcasted_iota(jnp.int32, (n_keys, rows), 1) % tq)

    ones = jnp.ones((), BF16)

    def with_ones(v_t, h):
        own = (lax.broadcasted_iota(jnp.int32, v_t.shape, 0) // HEAD_DIM) == h
        return jnp.where(own, v_t, ones)

    def value_blocks(ref, first_chunk, n_chunks):
        return jnp.concatenate([ref[0, first_chunk + u] for u in range(n_chunks)], axis=1)

    kv_heads = range(NSA_KV_HEADS)
    w0 = pl.multiple_of(jnp.maximum(t0 - WINDOW, 0), tq)

    @pl.when((b == 0) & (qb <= win_tiles))
    def _():
        w_dist = (t0 - w0) - key_minus_query(win_keys)
        in_window = (w_dist >= 0) & (w_dist < WINDOW)
        for h in kv_heads:
            winb_ref[h] = jnp.where(in_window, band_bias(w0 // tq, win_tiles + 1, h), MASK_VALUE)

    last = qb // chunks

    @pl.when(b == 0)
    def _():
        causal = key_minus_query(kt) <= t0 - last * kt
        for h in kv_heads:
            finb_ref[h] = jnp.where(causal, band_bias(last * chunks, chunks, h), MASK_VALUE)

    q_pads = [jnp.concatenate(
        [jnp.where((lane // HEAD_DIM) == h, q_all[:, LANES * g:LANES * (g + 1)], jnp.zeros((), BF16))
         for g in range(NSA_GROUP)], axis=0) for h in kv_heads]

    k_win = kw_ref[0, pl.ds(w0, win_keys), :]
    v_win = value_blocks(vwt_ref, w0 // tq, win_tiles + 1)
    o_ws = []
    for h in kv_heads:
        s_w = _dot_nt(k_win, q_pads[h]) + winb_ref[h]
        p_w = jnp.exp(s_w - jnp.max(s_w, axis=0, keepdims=True))
        acc_w = _dot(with_ones(v_win, h), p_w.astype(BF16))
        sum_row = HEAD_DIM * (1 - h)
        o_ws.append(acc_w / acc_w[sum_row:sum_row + 1, :])

    q_augs, o_cs = [], []
    for h in kv_heads:
        q_pad = q_pads[h]

        s_c = _dot_nt(q_pad, ck).reshape(NSA_GROUP, tq, n_cmp) + biasc_ref[NSA_GROUP * h:NSA_GROUP * (h + 1)]
        s_c = jnp.where((cdist >= 0)[None], s_c, neg_inf)
        m_c = jnp.max(s_c, axis=-1, keepdims=True)
        m_c = jnp.where(m_c == neg_inf, 0.0, m_c)
        e_c = jnp.exp(s_c - m_c)
        sum_c = jnp.sum(e_c, axis=-1, keepdims=True)
        p_c = e_c / jnp.where(sum_c > 0.0, sum_c, 1.0)
        o_cs.append(_dot(p_c.reshape(rows, n_cmp).astype(BF16), cv))

        imp = p_c[0]
        for g in range(1, NSA_GROUP):
            imp = imp + p_c[g]
        imp = imp[:, :n_slc] + imp[:, n_slc:]
        if n_slc < LANES:
            imp = jnp.concatenate([imp, jnp.zeros((tq, LANES - n_slc), F32)], axis=1)
        cur = (t0 + row) // SLC_BLOCK
        forced = (lane == 0) | (lane == cur) | (lane == cur - 1)
        score = jnp.where(lane <= cur, jnp.where(forced, jnp.inf, imp), neg_inf)
        sel = _top_blocks_cols(score.T, SLC_TOPN).T
        neg_mask = (sel - 1.0).astype(BF16)
        q_augs.append(jnp.concatenate([q_pad, jnp.concatenate([neg_mask] * NSA_GROUP, axis=0)], axis=1))

    def tile_logits(j):
        k0 = pl.multiple_of(j * kt, kt)
        k_aug = jnp.concatenate([ks_ref[0, pl.ds(k0, kt), :], e_ref[pl.ds(k0, kt), :]], axis=1)
        return [_dot_nt(k_aug, q_augs[h]) for h in kv_heads]

    def tile_pv(j):
        v_t = value_blocks(vst_ref, j * chunks, chunks)
        return [_dot(with_ones(v_t, h), prob_ref[h]) for h in kv_heads]

    for h, s in enumerate(tile_logits(0)):
        logit_ref[h] = s
        prob_ref[h] = jnp.zeros((kt, rows), BF16)
        acc_ref[h] = jnp.zeros((KV_WIDTH, rows), F32)

    def slc_tile(j, m_old, biased, final=False):
        pv = tile_pv(jnp.maximum(j - 1, 0))
        logits = [logit_ref[h] for h in kv_heads]
        if final:
            logits = [s + finb_ref[h] for h, s in enumerate(logits)]
        elif biased:
            logits = [s + band_bias(j * chunks, chunks, h) for h, s in enumerate(logits)]
        m_new = [jnp.maximum(m_old[h], jnp.max(logits[h], axis=0, keepdims=True)) for h in kv_heads]
        probs = [jnp.exp(logits[h] - m_new[h]).astype(BF16) for h in kv_heads]
        nxt = None if final else tile_logits(j + 1)
        for h in kv_heads:
            acc_ref[h] = jnp.exp(m_old[h] - m_new[h]) * (acc_ref[h] + pv[h])
            prob_ref[h] = probs[h]
            if not final:
                logit_ref[h] = nxt[h]
        return tuple(m_new)

    n_far = jnp.maximum(qb - (near - 1), 0) // chunks
    m_run = (jnp.full((1, rows), neg_inf, F32),) * NSA_KV_HEADS
    m_run = lax.fori_loop(0, n_far, functools.partial(slc_tile, biased=False), m_run)
    m_run = lax.fori_loop(n_far, last, functools.partial(slc_tile, biased=True), m_run)
    slc_tile(last, m_run, biased=True, final=True)
    acc_slc = [acc_ref[h] + pv for h, pv in enumerate(tile_pv(last))]

    gates_t = gates.T
    for h in range(NSA_KV_HEADS):
        sum_row = HEAD_DIM * (1 - h)
        o_c = o_cs[h]
        acc_s = acc_slc[h]
        o_s = acc_s / acc_s[sum_row:sum_row + 1, :]
        o_w = o_ws[h]

        mixed = []
        for g in range(NSA_GROUP):
            hd = NSA_GROUP * h + g
            rs = slice(g * tq, (g + 1) * tq)
            key_major = (gates_t[NSA_HEADS + hd:NSA_HEADS + hd + 1, :] * o_s[:, rs]
                         + gates_t[2 * NSA_HEADS + hd:2 * NSA_HEADS + hd + 1, :] * o_w[:, rs])
            mixed.append(gates[:, hd:hd + 1] * o_c[rs] + key_major.T)
        for pair in range(NSA_GROUP // 2):
            a, c = mixed[2 * pair], mixed[2 * pair + 1]
            if h == 0:
                both = jnp.where(lane < HEAD_DIM, a, pltpu.roll(c, HEAD_DIM, 1))
            else:
                both = jnp.where(lane < HEAD_DIM, pltpu.roll(a, HEAD_DIM, 1), c)
            col = (NSA_GROUP * h + 2 * pair) * HEAD_DIM
            y_ref[0, :, col:col + LANES] = (both * _silu(z_ref[0, :, col:col + LANES])).astype(BF16)


def _nsa_prompt(rel_bias, q, ck, cv, ks, vs_t, kw, vw_t, e_mat, gates, z):
    b, t, _ = q.shape
    tq = Q_TILE
    n_cmp = ck.shape[1]
    assert t % SLC_KEYS == 0 and SLC_KEYS % tq == 0 and WINDOW % tq == 0 and t >= WINDOW + tq
    assert t // SLC_BLOCK <= LANES
    _, _, n_band = _band_geometry(tq)
    tile = lambda width: pl.BlockSpec((1, tq, width), lambda i, j: (j, i, 0))
    seq = lambda rows: pl.BlockSpec((1, rows, LANES), lambda i, j: (j, 0, 0))
    seq_t = pl.BlockSpec((1, t // LANES, LANES, LANES), lambda i, j: (j, 0, 0, 0))
    assert tq == LANES
    return pl.pallas_call(
        _nsa_prompt_kernel,
        grid=(t // tq, b),
        in_specs=[pl.BlockSpec(memory_space=pltpu.SMEM), tile(NSA_WIDTH), seq(n_cmp), seq(n_cmp),
                  seq(t), seq_t, seq(t), seq_t, pl.BlockSpec((t, LANES), lambda i, j: (0, 0)),
                  tile(LANES), tile(NSA_WIDTH)],
        out_specs=tile(NSA_WIDTH),
        out_shape=jax.ShapeDtypeStruct((b, t, NSA_WIDTH), BF16),
        scratch_shapes=[pltpu.VMEM((NSA_HEADS, tq, n_cmp), F32),
                        pltpu.VMEM((n_band, NSA_HEADS, tq, tq), F32),
                        pltpu.VMEM((NSA_KV_HEADS, SLC_KEYS, NSA_GROUP * tq), F32),
                        pltpu.VMEM((NSA_KV_HEADS, SLC_KEYS, NSA_GROUP * tq), BF16),
                        pltpu.VMEM((NSA_KV_HEADS, KV_WIDTH, NSA_GROUP * tq), F32),
                        pltpu.VMEM((NSA_KV_HEADS, WINDOW + tq, NSA_GROUP * tq), F32),
                        pltpu.VMEM((NSA_KV_HEADS, SLC_KEYS, NSA_GROUP * tq), F32)],
        compiler_params=pltpu.CompilerParams(dimension_semantics=("arbitrary", "arbitrary"),
                                             vmem_limit_bytes=VMEM_LIMIT),
        name="nsa_prompt",
    )(rel_bias, q, ck, cv, ks, vs_t, kw, vw_t, e_mat, gates, z)


def _nsa_sample_kernel(pt_ref, rel_ref, q_ref, ck_ref, cv_ref, pk_ref, pv_ref, e_ref, kwc_ref, vwc_ref, new_ref,
                       g_ref, z_ref, y_ref, kbuf, vbuf, kaug_ref, biasc_ref, biass_ref, biasw_ref, ksem, vsem):
    page_rows = pk_ref.shape[2]
    past = kbuf.shape[2]
    n_cmp = ck_ref.shape[1]
    n_slc = n_cmp // (SLC_BLOCK // CMP_BLOCK)
    win_rows = kwc_ref.shape[2]
    heads = tuple(range(NSA_HEADS))

    @pl.when(pl.program_id(0) == 0)
    def _():
        ccol = lax.broadcasted_iota(jnp.int32, (1, n_cmp), 1)
        cblock = (SLC_BLOCK // CMP_BLOCK) * (ccol % n_slc) + ccol // n_slc
        cdist = past - (CMP_BLOCK * cblock + CMP_BLOCK - 1)
        biasc_ref[...] = jnp.concatenate(_rel_bias_rows(cdist, rel_ref, heads), axis=0)
        sdist = past - lax.broadcasted_iota(jnp.int32, (1, past), 1)
        biass_ref[...] = jnp.concatenate(_rel_bias_rows(sdist, rel_ref, heads), axis=0)
        wdist = win_rows - lax.broadcasted_iota(jnp.int32, (1, win_rows), 1)
        biasw_ref[...] = jnp.concatenate(_rel_bias_rows(wdist, rel_ref, heads), axis=0)
        kaug_ref[KV_WIDTH:, :] = e_ref[...]

    slot = _double_buffered_gather(pt_ref, (pk_ref, pv_ref), (kbuf, vbuf), (ksem, vsem), past // page_rows)

    q8 = q_ref[0]
    q8f = q8.astype(F32)
    bias0 = jnp.concatenate(
        [jnp.full((1, 1), rel_ref[0, hd] - rel_ref[REL_BUCKETS - 1, hd], F32) for hd in heads], axis=0)
    lane = lax.broadcasted_iota(jnp.int32, (NSA_KV_HEADS, LANES), 1)

    s_c = _dot_nt(q8, ck_ref[0]) + biasc_ref[...]
    e_c = jnp.exp(s_c - jnp.max(s_c, axis=1, keepdims=True))
    p_c = e_c / jnp.sum(e_c, axis=1, keepdims=True)
    o_c = _dot(p_c.astype(BF16), cv_ref[0])

    imps = []
    for h in range(NSA_KV_HEADS):
        acc = p_c[NSA_GROUP * h:NSA_GROUP * h + 1]
        for g in range(1, NSA_GROUP):
            acc = acc + p_c[NSA_GROUP * h + g:NSA_GROUP * h + g + 1]
        imps.append(acc)
    imp = jnp.concatenate(imps, axis=0)
    imp = imp[:, :n_slc] + imp[:, n_slc:]
    if n_slc < LANES:
        imp = jnp.concatenate([imp, jnp.zeros((NSA_KV_HEADS, LANES - n_slc), F32)], axis=1)
    forced = (lane == 0) | (lane == n_slc - 1)
    score = jnp.where(lane < n_slc, jnp.where(forced, jnp.inf, imp), -jnp.inf)
    padded = jnp.concatenate([score, jnp.full((LANES - NSA_KV_HEADS, LANES), -jnp.inf, F32)], axis=0)
    sel = _top_blocks_cols(padded.T, SLC_TOPN - 1).T[:NSA_KV_HEADS]
    neg_mask = (sel - 1.0).astype(BF16)
    neg_rows = jnp.concatenate([neg_mask[h:h + 1] for h in range(NSA_KV_HEADS) for _ in range(NSA_GROUP)], axis=0)
    q_aug = jnp.concatenate([q8, neg_rows], axis=1)

    def with_new_token(s, s_new, v_t, v_new):
        m = jnp.maximum(jnp.max(s, axis=1, keepdims=True), s_new)
        p = jnp.exp(s - m)
        p_new = jnp.exp(s_new - m)
        total = jnp.sum(p, axis=1, keepdims=True) + p_new
        return (_dot_nt(p.astype(BF16), v_t) + p_new.astype(BF16).astype(F32) * v_new.astype(F32)) / total

    kaug_ref[:KV_WIDTH, :] = kbuf[slot].astype(BF16)
    s_s = _dot(q_aug, kaug_ref[...]) + biass_ref[...]
    s_new = jnp.sum(q8f * new_ref[0, 0:1, :].astype(F32), axis=1, keepdims=True) + bias0
    o_s = with_new_token(s_s, s_new, vbuf[slot].astype(BF16), new_ref[0, 1:2, :])

    wcol = lax.broadcasted_iota(jnp.int32, (1, win_rows), 1)
    s_w = _dot(q8, kwc_ref[0].astype(BF16)) + biasw_ref[...]
    s_w = jnp.where(win_rows - wcol < WINDOW, s_w, MASK_VALUE)
    s_new = jnp.sum(q8f * new_ref[0, 2:3, :].astype(F32), axis=1, keepdims=True) + bias0
    o_w = with_new_token(s_w, s_new, vwc_ref[0].astype(BF16), new_ref[0, 3:4, :])

    gates = g_ref[0]
    low = []
    for hd in heads:
        r = (gates[:, hd:hd + 1] * o_c[hd:hd + 1]
             + gates[:, NSA_HEADS + hd:NSA_HEADS + hd + 1] * o_s[hd:hd + 1]
             + gates[:, 2 * NSA_HEADS + hd:2 * NSA_HEADS + hd + 1] * o_w[hd:hd + 1])
        low.append(r if hd < NSA_GROUP else pltpu.roll(r, HEAD_DIM, 1))
    lane1 = lax.broadcasted_iota(jnp.int32, (1, LANES), 1)
    for pair in range(NSA_HEADS // 2):
        both = jnp.where(lane1 < HEAD_DIM, low[2 * pair], pltpu.roll(low[2 * pair + 1], HEAD_DIM, 1))
        cols = slice(LANES * pair, LANES * (pair + 1))
        y_ref[0, :, cols] = (both * _silu(z_ref[0, :, cols])).astype(BF16)


def _nsa_sample(page_table, rel_bias, q8, ck, cv, pool_k, pool_v, e_mat, win_k, win_v, new_rows, gates, z):
    n_seq, n_pages = page_table.shape
    page_rows = pool_k.shape[2]
    past = n_pages * page_rows
    n_cmp = ck.shape[1]
    win_rows = win_k.shape[2]
    assert past % SLC_BLOCK == 0 and past // SLC_BLOCK <= LANES and page_rows % LANES == 0
    hbm = pl.BlockSpec(memory_space=pl.ANY)
    per_seq = lambda r, w: pl.BlockSpec((1, r, w), lambda i, pt: (i, 0, 0))
    return pl.pallas_call(
        _nsa_sample_kernel,
        grid_spec=pltpu.PrefetchScalarGridSpec(
            num_scalar_prefetch=1,
            grid=(n_seq,),
            in_specs=[pl.BlockSpec(memory_space=pltpu.SMEM), per_seq(NSA_HEADS, LANES), per_seq(n_cmp, LANES),
                      per_seq(n_cmp, LANES), hbm, hbm,
                      pl.BlockSpec((LANES, past), lambda i, pt: (0, 0)),
                      per_seq(KV_WIDTH, win_rows), per_seq(KV_WIDTH, win_rows), per_seq(4, LANES),
                      per_seq(1, LANES), per_seq(1, NSA_WIDTH)],
            out_specs=per_seq(1, NSA_WIDTH),
            scratch_shapes=[pltpu.VMEM((2, KV_WIDTH, past), F32), pltpu.VMEM((2, KV_WIDTH, past), F32),
                            pltpu.VMEM((KV_WIDTH + LANES, past), BF16),
                            pltpu.VMEM((NSA_HEADS, n_cmp), F32), pltpu.VMEM((NSA_HEADS, past), F32),
                            pltpu.VMEM((NSA_HEADS, win_rows), F32),
                            pltpu.SemaphoreType.DMA((2,)), pltpu.SemaphoreType.DMA((2,))]),
        out_shape=jax.ShapeDtypeStruct((n_seq, 1, NSA_WIDTH), BF16),
        compiler_params=pltpu.CompilerParams(dimension_semantics=("arbitrary",), vmem_limit_bytes=VMEM_LIMIT),
        name="nsa_sample",
    )(page_table.reshape(-1), rel_bias, q8, ck, cv, pool_k, pool_v, e_mat, win_k, win_v, new_rows, gates, z)


def _group_norm_gate(o, z, gn, seg):
    mu = _segment_sum(o, seg) * (1.0 / RET_VDIM)
    d = o - mu
    var = _segment_sum(d * d, seg) * (1.0 / RET_VDIM)
    return (d * lax.rsqrt(var + EPS) * gn * _silu(z)).astype(BF16)


def _retention_prompt_kernel(q_ref, k_ref, v_ref, z_ref, gn_ref, seg_ref, dec_ref, qdec_ref, kdec_ref, cdec_ref,
                             y_ref, st_ref, state):
    chunk = pl.program_id(1)
    tc = q_ref.shape[1]
    lane = lax.broadcasted_iota(jnp.int32, (tc, LANES), 1)
    srow = lax.broadcasted_iota(jnp.int32, (LANES, LANES), 0)
    scol = lax.broadcasted_iota(jnp.int32, (LANES, LANES), 1)
    same_head = (srow // RET_KDIM) == (scol // RET_VDIM)

    @pl.when(chunk == 0)
    def _():
        state[...] = jnp.zeros_like(state)

    for pair in range(RET_HEADS // 2):
        cols = slice(LANES * pair, LANES * (pair + 1))
        q = q_ref[0, :, cols]
        k = k_ref[0, :, cols]
        v = v_ref[0, :, cols]
        s_old = state[pair]
        cross = _dot(q, s_old.astype(BF16)) * qdec_ref[:, cols]
        halves = []
        for e in range(2):
            qe = jnp.where((lane // RET_KDIM) == e, q, jnp.zeros((), BF16))
            scores = _dot_nt(qe, k) * dec_ref[2 * pair + e]
            halves.append(_dot(scores.astype(BF16), v))
        o = jnp.where(lane < RET_VDIM, halves[0], halves[1]) + cross
        y_ref[0, :, cols] = _group_norm_gate(o, z_ref[0, :, cols], gn_ref[:, cols], seg_ref[...])
        kd_t = (k.astype(F32) * kdec_ref[:, cols]).T.astype(BF16)
        s_new = s_old * cdec_ref[:, cols] + jnp.where(same_head, _dot(kd_t, v), 0.0)
        state[pair] = s_new

    @pl.when(chunk == pl.num_programs(1) - 1)
    def _():
        for head in range(RET_HEADS):
            e = head % 2
            st_ref[0, head] = state[head // 2, RET_KDIM * e:RET_KDIM * (e + 1), RET_VDIM * e:RET_VDIM * (e + 1)]


def _retention_tables(tc):
    log_g = jnp.asarray(RET_GAMMA_LOG, F32)
    i = jnp.arange(tc, dtype=F32)
    diff = i[:, None] - i[None, :]
    decay = jnp.where(diff >= 0, jnp.exp(jnp.maximum(diff, 0.0)[None] * log_g[:, None, None]), 0.0)
    widen = lambda a: jnp.repeat(a, RET_VDIM, axis=-1)
    q_decay = widen(jnp.exp((i[:, None] + 1.0) * log_g[None, :]))
    k_decay = widen(jnp.exp((tc - 1.0 - i)[:, None] * log_g[None, :]))
    c_decay = widen(jnp.exp(tc * log_g)[None, :])
    return decay, q_decay, k_decay, c_decay


def _retention_prompt(q, k, v, z, gn, seg):
    b, t, _ = q.shape
    tc = RET_TILE
    assert t % tc == 0
    decay, q_decay, k_decay, c_decay = _retention_tables(tc)
    tile = pl.BlockSpec((1, tc, RET_WIDTH), lambda i, j: (i, j, 0))
    const = lambda a: pl.BlockSpec(a.shape, lambda i, j: (0,) * a.ndim)
    return pl.pallas_call(
        _retention_prompt_kernel,
        grid=(b, t // tc),
        in_specs=[tile, tile, tile, tile, const(gn), const(seg), const(decay), const(q_decay), const(k_decay),
                  const(c_decay)],
        out_specs=[tile, pl.BlockSpec((1, RET_HEADS, RET_KDIM, RET_VDIM), lambda i, j: (i, 0, 0, 0))],
        out_shape=[jax.ShapeDtypeStruct((b, t, RET_WIDTH), BF16),
                   jax.ShapeDtypeStruct((b, RET_HEADS, RET_KDIM, RET_VDIM), F32)],
        scratch_shapes=[pltpu.VMEM((RET_HEADS // 2, LANES, LANES), F32)],
        compiler_params=pltpu.CompilerParams(dimension_semantics=("arbitrary", "arbitrary"),
                                             vmem_limit_bytes=VMEM_LIMIT),
        name="retention_prompt",
    )(q, k, v, z, gn, seg, decay, q_decay, k_decay, c_decay)


def _retention_sample_kernel(qt_ref, kt_ref, v_ref, z_ref, gn_ref, gam_ref, st_ref, y_ref, so_ref):
    n = v_ref.shape[1]
    qt = qt_ref[0]
    kt = kt_ref[0]
    gam = gam_ref[...]
    qk = jnp.sum(qt * kt, axis=1)
    for s in range(n):
        st = st_ref[s]
        vs = v_ref[0, s]
        cross = jnp.sum(qt[:, :, s:s + 1] * st, axis=1)
        so_ref[s] = st * gam[:, :, None] + kt[:, :, s:s + 1] * vs[:, None, :]
        o = qk[:, s:s + 1] * vs + gam * cross
        mu = jnp.mean(o, axis=-1, keepdims=True)
        d = o - mu
        var = jnp.mean(d * d, axis=-1, keepdims=True)
        y_ref[0, s] = (d * lax.rsqrt(var + EPS) * gn_ref[...] * _silu(z_ref[0, s])).astype(BF16)


def _retention_sample(q, k, v, z, gn, state):
    n_seq = q.shape[0]
    n = min(RET_SAMPLE_SEQS, n_seq)
    assert n_seq % n == 0
    steps = n_seq // n
    to_cols = lambda a: a.astype(F32).reshape(steps, n, RET_HEADS, RET_KDIM).transpose(0, 2, 3, 1)
    to_rows = lambda a: a.astype(F32).reshape(steps, n, RET_HEADS, RET_VDIM)
    gam = jnp.exp(jnp.asarray(RET_GAMMA_LOG, F32)).reshape(RET_HEADS, 1)
    cols = pl.BlockSpec((1, RET_HEADS, RET_KDIM, n), lambda i: (i, 0, 0, 0))
    rws = pl.BlockSpec((1, n, RET_HEADS, RET_VDIM), lambda i: (i, 0, 0, 0))
    st = pl.BlockSpec((n, RET_HEADS, RET_KDIM, RET_VDIM), lambda i: (i, 0, 0, 0))
    y, new_state = pl.pallas_call(
        _retention_sample_kernel,
        grid=(steps,),
        in_specs=[cols, cols, rws, rws, pl.BlockSpec((RET_HEADS, RET_VDIM), lambda i: (0, 0)),
                  pl.BlockSpec((RET_HEADS, 1), lambda i: (0, 0)), st],
        out_specs=[rws, st],
        out_shape=[jax.ShapeDtypeStruct((steps, n, RET_HEADS, RET_VDIM), BF16),
                   jax.ShapeDtypeStruct(state.shape, F32)],
        compiler_params=pltpu.CompilerParams(dimension_semantics=("parallel",), vmem_limit_bytes=VMEM_LIMIT),
        name="retention_sample",
    )(to_cols(q), to_cols(k), to_rows(v), to_rows(z), gn.reshape(RET_HEADS, RET_VDIM), gam, state)
    return y.reshape(n_seq, RET_WIDTH), new_state


def _out_kernel(x_ref, yn_ref, yr_ref, w_ref, o_ref):
    y = jnp.concatenate([yn_ref[0], yr_ref[0]], axis=1)
    o_ref[0] = x_ref[0] + _dot(y, w_ref[...])


def _out_project(x, y_nsa, y_ret, w):
    nb, rows, d = x.shape
    tm = min(PROJ_ROWS, rows)
    spec = lambda width: pl.BlockSpec((1, tm, width), lambda i, j: (i, j, 0))
    return pl.pallas_call(
        _out_kernel,
        grid=(nb, rows // tm),
        in_specs=[spec(d), spec(NSA_WIDTH), spec(RET_WIDTH), pl.BlockSpec(w.shape, lambda i, j: (0, 0))],
        out_specs=spec(d),
        out_shape=jax.ShapeDtypeStruct(x.shape, F32),
        compiler_params=pltpu.CompilerParams(dimension_semantics=("parallel", "parallel"),
                                             vmem_limit_bytes=VMEM_LIMIT),
        name="output_projection",
    )(x, y_nsa, y_ret, w)


def _rotary_tables(pos):
    half = RET_KDIM // 2
    inv = ROPE_BASE ** (-jnp.arange(half, dtype=F32) / half)
    ang = pos.astype(F32)[:, None] * inv[None, :]
    cos, sin = jnp.cos(ang), jnp.sin(ang)
    reps = LANES // RET_KDIM
    return jnp.tile(jnp.concatenate([cos, cos], axis=1), (1, reps)), jnp.tile(jnp.concatenate([-sin, sin], axis=1), (1, reps))


def _arrange_w_in(w_in):
    d = w_in.shape[0]
    parts = jnp.split(w_in, np.cumsum(SPLIT_WIDTHS)[:-1].tolist(), axis=1)
    order = [hd for g in range(NSA_GROUP) for hd in (g, NSA_GROUP + g)]
    q = parts[0].reshape(d, NSA_HEADS, HEAD_DIM)[:, order].reshape(d, NSA_WIDTH)
    gates = parts[7].reshape(d, NSA_HEADS, 3).transpose(0, 2, 1).reshape(d, 3 * NSA_HEADS)
    gates = jnp.pad(gates, ((0, 0), (0, LANES - 3 * NSA_HEADS)))
    w = jnp.concatenate([q] + parts[1:7] + parts[8:] + [gates], axis=1)
    assert w.shape[1] == COL_END
    return w.astype(BF16)


def _compress_weights(pe, w1, w2):
    w1r = w1.reshape(CMP_BLOCK, HEAD_DIM, HEAD_DIM)
    w1_both = jnp.zeros((CMP_BLOCK, NSA_KV_HEADS, HEAD_DIM, NSA_KV_HEADS, HEAD_DIM), F32)
    w2_both = jnp.zeros((NSA_KV_HEADS, HEAD_DIM, NSA_KV_HEADS, HEAD_DIM), F32)
    for h in range(NSA_KV_HEADS):
        w1_both = w1_both.at[:, h, :, h, :].set(w1r)
        w2_both = w2_both.at[h, :, h, :].set(w2)
    return (jnp.tile(pe, (1, NSA_KV_HEADS)), w1_both.reshape(CMP_BLOCK * KV_WIDTH, KV_WIDTH).astype(BF16),
            w2_both.reshape(KV_WIDTH, KV_WIDTH).astype(BF16))


def _even_odd(c):
    n, nc, w = c.shape
    ratio = SLC_BLOCK // CMP_BLOCK
    return c.reshape(n, nc // ratio, ratio, w).transpose(0, 2, 1, 3).reshape(n, nc, w)


def _block_membership(n_keys):
    k = jnp.arange(n_keys)[:, None] // SLC_BLOCK
    return jnp.where(k == jnp.arange(LANES)[None, :], -MASK_VALUE, 0.0).astype(BF16)


def kernel(x_prompt, x_sample, cache_cmp_k, cache_cmp_v, cache_slc_k, cache_slc_v, cache_win_k, cache_win_v,
           state_ret, page_table, norm_g, w_in, q_norm_g, k_norm_g, cmp_pe_k, cmp_w1_k, cmp_w2_k, cmp_pe_v,
           cmp_w1_v, cmp_w2_v, rel_bias, ret_gn_g, w_out):
    b, t, d = x_prompt.shape
    n_seq, dec_len, _ = x_sample.shape
    n_pages = page_table.shape[1]
    page_rows = cache_cmp_k.shape[1]
    past = n_pages * page_rows
    assert dec_len == 1 and past % CMP_BLOCK == 0
    kv4 = lambda a: a.reshape(a.shape[0], a.shape[1], NSA_KV_HEADS, HEAD_DIM)

    w = _arrange_w_in(w_in)
    w_o = w_out.astype(BF16)
    ng = norm_g.reshape(1, d)
    qg = (jnp.tile(q_norm_g, LANES // HEAD_DIM) * ATTN_SCALE).reshape(1, LANES)
    kg = jnp.tile(k_norm_g, LANES // HEAD_DIM).reshape(1, LANES)
    gn = ret_gn_g.reshape(1, RET_WIDTH)
    lane = np.arange(LANES)
    seg = jnp.asarray(lane[:, None] // HEAD_DIM == lane[None, :] // HEAD_DIM, BF16)
    cw = _compress_weights(cmp_pe_k, cmp_w1_k, cmp_w2_k) + _compress_weights(cmp_pe_v, cmp_w1_v, cmp_w2_v)

    cos, sin = _rotary_tables(jnp.arange(t))
    (q, kc, vc, ks, vs, kw, vw, ksb, _, kwb, _, gates, zn, qr, kr, vr, zr, vs_t, vw_t) = _project(
        x_prompt, cos, sin, ng, w, qg, kg, seg)
    ck, cv = _compress_prompt(kc, vc, cw, kg, seg)
    y_nsa = _nsa_prompt(rel_bias, q, _even_odd(ck), _even_odd(cv), ksb, vs_t, kwb, vw_t, _block_membership(t), gates, zn)
    y_ret, p_ret = _retention_prompt(qr, kr, vr, zr, gn, seg)
    y_prompt = _out_project(x_prompt, y_nsa, y_ret, w_o)
    keep = min(WINDOW, t)
    prompt_out = (y_prompt, kv4(kc), kv4(vc), kv4(ks), kv4(vs), kv4(kw[:, t - keep:]), kv4(vw[:, t - keep:]), p_ret)

    cos, sin = _rotary_tables(jnp.full((n_seq,), past))
    xs = x_sample.reshape(1, n_seq, d)
    (q, kc, vc, ks, vs, kw, vw, ksb, vsb, kwb, vwb, gates, zn, qr, kr, vr, zr) = [
        a[0] for a in _project(xs, cos, sin, ng, w, qg, kg, seg)[:17]]
    chan = lambda a: jnp.transpose(a, (0, 2, 3, 1)).reshape(a.shape[0], KV_WIDTH, a.shape[1])
    ck, cv = _compress_sample(page_table, chan(cache_cmp_k), chan(cache_cmp_v), cw, kg, seg)
    half = jnp.asarray(lane[None, :] // HEAD_DIM == (np.arange(NSA_HEADS) // NSA_GROUP)[:, None])
    q8 = jnp.where(half[None], jnp.tile(q.reshape(n_seq, NSA_GROUP, LANES), (1, NSA_KV_HEADS, 1)), jnp.zeros((), BF16))
    new_rows = jnp.stack([ksb, vsb, kwb, vwb], axis=1)
    y_nsa = _nsa_sample(page_table, rel_bias, q8, _even_odd(ck), _even_odd(cv), chan(cache_slc_k), chan(cache_slc_v),
                        _block_membership(past).T, chan(cache_win_k), chan(cache_win_v), new_rows,
                        gates.reshape(n_seq, 1, LANES), zn.reshape(n_seq, 1, NSA_WIDTH))
    y_ret, s_ret = _retention_sample(qr, kr, vr, zr, gn, state_ret)
    y_sample = _out_project(xs, y_nsa.reshape(1, n_seq, NSA_WIDTH), y_ret.reshape(1, n_seq, RET_WIDTH), w_o)
    keep = min(WINDOW, cache_win_k.shape[1] + 1)
    new4 = lambda a: a.reshape(n_seq, 1, NSA_KV_HEADS, HEAD_DIM)
    s_win_k = jnp.concatenate([cache_win_k, new4(kw)], axis=1)[:, -keep:]
    s_win_v = jnp.concatenate([cache_win_v, new4(vw)], axis=1)[:, -keep:]
    sample_out = (y_sample.reshape(n_seq, 1, d), new4(kc), new4(vc), new4(ks), new4(vs), s_win_k, s_win_v, s_ret)

    return (prompt_out[0], sample_out[0]) + prompt_out[1:] + sample_out[1:]
```

```python
import functools

import numpy as np
import jax
import jax.numpy as jnp
from jax import lax
from jax.experimental import pallas as pl
from jax.experimental.pallas import tpu as pltpu

F32, BF16 = jnp.float32, jnp.bfloat16

NSA_HEADS = 8
NSA_KV_HEADS = 2
HEAD_DIM = 64
NSA_GROUP = NSA_HEADS // NSA_KV_HEADS
NSA_WIDTH = NSA_HEADS * HEAD_DIM
KV_WIDTH = NSA_KV_HEADS * HEAD_DIM
CMP_BLOCK = 32
SLC_BLOCK = 64
SLC_TOPN = 16
WINDOW = 512
ATTN_SCALE = HEAD_DIM ** -0.5
RET_HEADS = 8
RET_KDIM = 64
RET_VDIM = 64
RET_WIDTH = RET_HEADS * RET_VDIM
ROPE_BASE = 10000.0
REL_BUCKETS = 32
REL_MAX_DIST = 1024
EPS = 1e-6
SPLIT_WIDTHS = (NSA_WIDTH, KV_WIDTH, KV_WIDTH, KV_WIDTH, KV_WIDTH, KV_WIDTH, KV_WIDTH,
                3 * NSA_HEADS, NSA_WIDTH, RET_HEADS * RET_KDIM, RET_HEADS * RET_KDIM, RET_WIDTH, RET_WIDTH)

LANES = 128
VMEM_LIMIT = 56 * 1024 * 1024

PROJ_ROWS = 512
Q_TILE = 128
SLC_KEYS = 512
RET_TILE = 512
RET_SAMPLE_SEQS = 16
CMP_PITCH = CMP_BLOCK + 4

MASK_VALUE = -float(2 ** 30)

MXU_COLS = 256
COL_Q = 0
COL_KC = COL_Q + NSA_WIDTH
COL_VC = COL_KC + LANES
COL_KS = COL_VC + LANES
COL_VS = COL_KS + LANES
COL_KW = COL_VS + LANES
COL_VW = COL_KW + LANES
COL_ZN = COL_VW + LANES
COL_QR = COL_ZN + NSA_WIDTH
COL_KR = COL_QR + RET_WIDTH
COL_VR = COL_KR + RET_WIDTH
COL_ZR = COL_VR + RET_WIDTH
COL_G = COL_ZR + RET_WIDTH
COL_END = COL_G + LANES


def _bucket_lower_bounds():
    exact = REL_BUCKETS // 2
    ratio = REL_MAX_DIST // exact
    lows = list(range(exact))
    n = exact
    for k in range(REL_BUCKETS - exact):
        while n ** (REL_BUCKETS - exact) < exact ** (REL_BUCKETS - exact) * ratio ** k:
            n += 1
        lows.append(n)
    return tuple(lows)


BUCKET_LOW = _bucket_lower_bounds()
FAR_DIST = BUCKET_LOW[-1]
RET_GAMMA_LOG = tuple(float(np.log1p(-np.exp2(-5.0 - h))) for h in range(RET_HEADS))


def _dot(a, b):
    return jnp.dot(a, b, preferred_element_type=F32)


def _dot_nt(a, b):
    return lax.dot_general(a, b, (((1,), (1,)), ((), ())), preferred_element_type=F32)


def _segment_sum(v, seg):
    hi = v.astype(BF16)
    lo = (v - hi.astype(F32)).astype(BF16)
    return _dot(hi, seg) + _dot(lo, seg)


def _head_rms(y, g, seg):
    ms = _segment_sum(y * y, seg) * (1.0 / HEAD_DIM)
    return y * lax.rsqrt(ms + EPS) * g


def _silu(x):
    return x * jax.nn.sigmoid(x)


def _rel_bias_rows(dist, rel_ref, heads):
    out = [jnp.full(dist.shape, rel_ref[0, hd], F32) for hd in heads]
    for bkt in range(1, REL_BUCKETS):
        hit = dist >= BUCKET_LOW[bkt]
        out = [jnp.where(hit, rel_ref[bkt, hd], o) for hd, o in zip(heads, out)]
    return [o - rel_ref[REL_BUCKETS - 1, hd] for hd, o in zip(heads, out)]


def _top_blocks_cols(score, n_pick):
    n_rows = score.shape[0]
    rowi = lax.broadcasted_iota(jnp.int32, score.shape, 0).astype(F32)
    neg_inf = -jnp.inf
    left = score
    for _ in range(n_pick):
        m = jnp.max(left, axis=0, keepdims=True)
        first = jnp.min(jnp.where(left == m, rowi, float(n_rows)), axis=0, keepdims=True)
        left = jnp.where(rowi == first, neg_inf, left)
    return jnp.where(left == neg_inf, jnp.where(score > neg_inf, 1.0, 0.0), 0.0)


def _proj_kernel(x_ref, cos_ref, sin_ref, ng_ref, w_ref, qg_ref, kg_ref, seg_ref,
                 q_o, kc_o, vc_o, ks_o, vs_o, kw_o, vw_o, ksb_o, vsb_o, kwb_o, vwb_o,
                 g_o, zn_o, qr_o, kr_o, vr_o, zr_o, vst_o, vwt_o):
    x = x_ref[0]
    inv = lax.rsqrt(jnp.mean(x * x, axis=-1, keepdims=True) + EPS)
    xn = (x * inv * ng_ref[...]).astype(BF16)
    seg = seg_ref[...]

    def mm(col, width):
        return _dot(xn, w_ref[:, col:col + width])

    def chunk(y, c):
        return y[:, LANES * c:LANES * (c + 1)]

    y = mm(COL_Q, NSA_WIDTH)
    for c in range(NSA_WIDTH // LANES):
        q_o[0, :, LANES * c:LANES * (c + 1)] = _head_rms(chunk(y, c), qg_ref[...], seg).astype(BF16)
    y = mm(COL_KC, 2 * LANES)
    kc_o[0] = chunk(y, 0)
    vc_o[0] = chunk(y, 1)
    for col, k_o, v_o, kb_o, vb_o, vt_o in ((COL_KS, ks_o, vs_o, ksb_o, vsb_o, vst_o),
                                            (COL_KW, kw_o, vw_o, kwb_o, vwb_o, vwt_o)):
        y = mm(col, 2 * LANES)
        k = _head_rms(chunk(y, 0), kg_ref[...], seg)
        k_o[0] = k
        kb_o[0] = k.astype(BF16)
        v = chunk(y, 1)
        v_o[0] = v
        vb_o[0] = v.astype(BF16)
        for c in range(v.shape[0] // LANES):
            vt_o[0, c] = v[LANES * c:LANES * (c + 1)].T.astype(BF16)
    zn_o[0] = mm(COL_ZN, NSA_WIDTH)

    cos = cos_ref[...]
    sin = sin_ref[...]
    lane = lax.broadcasted_iota(jnp.int32, cos.shape, 1)
    first_half = (lane % HEAD_DIM) < (HEAD_DIM // 2)
    for col, o_ref, scale in ((COL_QR, qr_o, None), (COL_KR, kr_o, RET_KDIM ** -0.5)):
        y = mm(col, RET_WIDTH)
        for c in range(RET_WIDTH // LANES):
            yc = chunk(y, c)
            partner = jnp.where(first_half, pltpu.roll(yc, LANES - HEAD_DIM // 2, 1),
                                pltpu.roll(yc, HEAD_DIM // 2, 1))
            r = yc * cos + partner * sin
            if scale is not None:
                r = r * scale
            o_ref[0, :, LANES * c:LANES * (c + 1)] = r.astype(BF16)
    vr_o[0] = mm(COL_VR, RET_WIDTH).astype(BF16)
    zr_o[0] = mm(COL_ZR, RET_WIDTH)
    g_o[0] = jax.nn.sigmoid(mm(COL_G, LANES))


def _project(x, cos, sin, norm_g, w, qg, kg, seg):
    nb, rows, d = x.shape
    tm = min(PROJ_ROWS, rows)
    assert rows % tm == 0 and tm % LANES == 0
    row_spec = lambda width: pl.BlockSpec((1, tm, width), lambda i, j: (i, j, 0))
    const = lambda shape: pl.BlockSpec(shape, lambda i, j: (0,) * len(shape))
    outs = [(NSA_WIDTH, BF16)] + [(LANES, F32)] * 6 + [(LANES, BF16)] * 4 + [(LANES, F32), (NSA_WIDTH, F32),
            (RET_WIDTH, BF16), (RET_WIDTH, BF16), (RET_WIDTH, BF16), (RET_WIDTH, F32)]
    blocks_spec = pl.BlockSpec((1, tm // LANES, LANES, LANES), lambda i, j: (i, j, 0, 0))
    blocks_shape = jax.ShapeDtypeStruct((nb, rows // LANES, LANES, LANES), BF16)
    return pl.pallas_call(
        _proj_kernel,
        grid=(nb, rows // tm),
        in_specs=[row_spec(d), pl.BlockSpec((tm, LANES), lambda i, j: (j, 0)),
                  pl.BlockSpec((tm, LANES), lambda i, j: (j, 0)), const((1, d)), const(w.shape),
                  const((1, LANES)), const((1, LANES)), const((LANES, LANES))],
        out_specs=[row_spec(wd) for wd, _ in outs] + [blocks_spec] * 2,
        out_shape=[jax.ShapeDtypeStruct((nb, rows, wd), dt) for wd, dt in outs] + [blocks_shape] * 2,
        compiler_params=pltpu.CompilerParams(dimension_semantics=("parallel", "parallel"),
                                             vmem_limit_bytes=VMEM_LIMIT),
        name="input_projection",
    )(x, cos, sin, norm_g, w, qg, kg, seg)


def _compress_rows(buf_ref, pe_ref, w1_ref, w2_ref, n_blk, pitch=CMP_BLOCK):
    parts = []
    for i in range(CMP_BLOCK):
        rows = buf_ref[pl.ds(i, n_blk, stride=pitch), :] + pe_ref[i:i + 1, :]
        parts.append(rows.astype(BF16))
    flat = jnp.concatenate(parts, axis=1)
    hidden = _silu(_dot(flat, w1_ref[...]))
    return _dot(hidden.astype(BF16), w2_ref[...])


def _compress_kernel(kc_ref, vc_ref, pek_ref, w1k_ref, w2k_ref, pev_ref, w1v_ref, w2v_ref, kg_ref, seg_ref,
                     ck_o, cv_o):
    n_blk = ck_o.shape[1]
    ck = _compress_rows(kc_ref.at[0], pek_ref, w1k_ref, w2k_ref, n_blk)
    ck_o[0] = _head_rms(ck, kg_ref[...], seg_ref[...]).astype(BF16)
    cv_o[0] = _compress_rows(vc_ref.at[0], pev_ref, w1v_ref, w2v_ref, n_blk).astype(BF16)


def _compress_prompt(kc, vc, cw, kg, seg):
    b, t, _ = kc.shape
    n_blk = t // CMP_BLOCK
    seq = pl.BlockSpec((1, t, LANES), lambda i: (i, 0, 0))
    const = lambda a: pl.BlockSpec(a.shape, lambda i: (0,) * a.ndim)
    out = pl.BlockSpec((1, n_blk, LANES), lambda i: (i, 0, 0))
    return pl.pallas_call(
        _compress_kernel,
        grid=(b,),
        in_specs=[seq, seq] + [const(a) for a in cw] + [const(kg), const(seg)],
        out_specs=[out, out],
        out_shape=[jax.ShapeDtypeStruct((b, n_blk, LANES), BF16)] * 2,
        compiler_params=pltpu.CompilerParams(dimension_semantics=("parallel",), vmem_limit_bytes=VMEM_LIMIT),
        name="compress_prompt",
    )(kc, vc, *cw, kg, seg)


def _page_copy(pt_ref, pool_ref, buf_ref, sem, seq, page, n_pages):
    rows = pool_ref.shape[2]
    return pltpu.make_async_copy(pool_ref.at[pt_ref[seq * n_pages + page]],
                                 buf_ref.at[:, pl.ds(page * rows, rows)], sem)


def _gather_pages(pt_ref, pools, bufs, sems, seq, slot, n_pages, wait):
    for pool_ref, buf_ref, sem in zip(pools, bufs, sems):
        for page in range(n_pages):
            copy = _page_copy(pt_ref, pool_ref, buf_ref.at[slot], sem.at[slot], seq, page, n_pages)
            if wait:
                copy.wait()
            else:
                copy.start()


def _double_buffered_gather(pt_ref, pools, bufs, sems, n_pages):
    seq = pl.program_id(0)
    slot = seq % 2

    @pl.when(seq == 0)
    def _():
        _gather_pages(pt_ref, pools, bufs, sems, seq, slot, n_pages, wait=False)

    @pl.when(seq + 1 < pl.num_programs(0))
    def _():
        _gather_pages(pt_ref, pools, bufs, sems, seq + 1, 1 - slot, n_pages, wait=False)

    _gather_pages(pt_ref, pools, bufs, sems, seq, slot, n_pages, wait=True)
    return slot


def _compress_sample_kernel(pt_ref, pk_ref, pv_ref, pek_ref, w1k_ref, w2k_ref, pev_ref, w1v_ref, w2v_ref,
                            kg_ref, seg_ref, ck_o, cv_o, kbuf, vbuf, rows_ref, ksem, vsem):
    page_rows = pk_ref.shape[2]
    n_pages = kbuf.shape[2] // page_rows
    n_blk = ck_o.shape[1]
    slot = _double_buffered_gather(pt_ref, (pk_ref, pv_ref), (kbuf, vbuf), (ksem, vsem), n_pages)

    blocks_per_page = page_rows // CMP_BLOCK

    def token_major(buf):
        for page in range(n_pages):
            tokens = buf[slot, :, page * page_rows:(page + 1) * page_rows].T
            for c in range(blocks_per_page):
                r0 = (page * blocks_per_page + c) * CMP_PITCH
                rows_ref[r0:r0 + CMP_BLOCK, :] = tokens[c * CMP_BLOCK:(c + 1) * CMP_BLOCK]

    token_major(kbuf)
    ck = _compress_rows(rows_ref, pek_ref, w1k_ref, w2k_ref, n_blk, CMP_PITCH)
    ck_o[0] = _head_rms(ck, kg_ref[...], seg_ref[...]).astype(BF16)
    token_major(vbuf)
    cv_o[0] = _compress_rows(rows_ref, pev_ref, w1v_ref, w2v_ref, n_blk, CMP_PITCH).astype(BF16)


def _compress_sample(page_table, pool_k, pool_v, cw, kg, seg):
    n_seq, n_pages = page_table.shape
    page_rows = pool_k.shape[2]
    assert page_rows == LANES
    past = n_pages * page_rows
    n_blk = past // CMP_BLOCK
    hbm = pl.BlockSpec(memory_space=pl.ANY)
    const = lambda a: pl.BlockSpec(a.shape, lambda i, pt: (0,) * a.ndim)
    out = pl.BlockSpec((1, n_blk, LANES), lambda i, pt: (i, 0, 0))
    return pl.pallas_call(
        _compress_sample_kernel,
        grid_spec=pltpu.PrefetchScalarGridSpec(
            num_scalar_prefetch=1,
            grid=(n_seq,),
            in_specs=[hbm, hbm] + [const(a) for a in cw] + [const(kg), const(seg)],
            out_specs=[out, out],
            scratch_shapes=[pltpu.VMEM((2, KV_WIDTH, past), F32), pltpu.VMEM((2, KV_WIDTH, past), F32),
                            pltpu.VMEM((n_blk * CMP_PITCH, KV_WIDTH), F32),
                            pltpu.SemaphoreType.DMA((2,)), pltpu.SemaphoreType.DMA((2,))]),
        out_shape=[jax.ShapeDtypeStruct((n_seq, n_blk, LANES), BF16)] * 2,
        compiler_params=pltpu.CompilerParams(dimension_semantics=("arbitrary",), vmem_limit_bytes=VMEM_LIMIT),
        name="compress_sample",
    )(page_table.reshape(-1), pool_k, pool_v, *cw, kg, seg)


def _band_geometry(tq):
    near = -(-(FAR_DIST + tq - 1) // tq)
    rel_max = near + SLC_KEYS // tq - 2
    rel_min = -max(WINDOW // tq, SLC_KEYS // tq - 1)
    return near, rel_max, rel_max - rel_min + 1


def _nsa_prompt_kernel(rel_ref, q_ref, ck_ref, cv_ref, ks_ref, vst_ref, kw_ref, vwt_ref, e_ref, g_ref, z_ref,
                       y_ref, biasc_ref, band_ref, logit_ref, prob_ref, acc_ref, winb_ref, finb_ref):
    tq = q_ref.shape[1]
    n_cmp = ck_ref.shape[1]
    n_slc = n_cmp // (SLC_BLOCK // CMP_BLOCK)
    rows = NSA_GROUP * tq
    kt = SLC_KEYS
    chunks = kt // tq
    near, rel_max, n_band = _band_geometry(tq)
    win_tiles = WINDOW // tq
    win_keys = WINDOW + tq
    qb = pl.program_id(0)
    b = pl.program_id(1)
    t0 = qb * tq
    all_heads = tuple(range(NSA_HEADS))

    lane = lax.broadcasted_iota(jnp.int32, (tq, LANES), 1)
    row = lax.broadcasted_iota(jnp.int32, (tq, LANES), 0)

    @pl.when((qb == 0) & (b == 0))
    def _():
        for c in range(n_band):
            dist = lane - row + tq * (rel_max - c)
            for hd, bias in zip(all_heads, _rel_bias_rows(dist, rel_ref, all_heads)):
                band_ref[c, hd] = bias

    def band_bias(first_chunk, n_chunks, h):
        base = rel_max - qb + first_chunk
        return jnp.concatenate(
            [jnp.concatenate([band_ref[base + u, NSA_GROUP * h + g] for g in range(NSA_GROUP)], axis=1)
             for u in range(n_chunks)], axis=0)

    ccol = lax.broadcasted_iota(jnp.int32, (tq, n_cmp), 1)
    crow = lax.broadcasted_iota(jnp.int32, (tq, n_cmp), 0)
    cblock = (SLC_BLOCK // CMP_BLOCK) * (ccol % n_slc) + ccol // n_slc
    cdist = t0 + crow - (CMP_BLOCK * cblock + CMP_BLOCK - 1)

    @pl.when(b == 0)
    def _():
        for hd, bias in zip(all_heads, _rel_bias_rows(cdist, rel_ref, all_heads)):
            biasc_ref[hd] = bias

    q_all = q_ref[0]
    gates = g_ref[0]
    ck = ck_ref[0]
    cv = cv_ref[0]
    neg_inf = -jnp.inf
    def key_minus_query(n_keys):
        return (lax.broadcasted_iota(jnp.int32, (n_keys, rows), 0)
                - lax.broadcasted_iota(jnp.int32, (n_keys, rows), 1) % tq)

    ones = jnp.ones((), BF16)

    def with_ones(v_t, h):
        own = (lax.broadcasted_iota(jnp.int32, v_t.shape, 0) // HEAD_DIM) == h
        return jnp.where(own, v_t, ones)

    def value_blocks(ref, first_chunk, n_chunks):
        return jnp.concatenate([ref[0, first_chunk + u] for u in range(n_chunks)], axis=1)

    kv_heads = range(NSA_KV_HEADS)
    w0 = pl.multiple_of(jnp.maximum(t0 - WINDOW, 0), tq)

    @pl.when((b == 0) & (qb <= win_tiles))
    def _():
        w_dist = (t0 - w0) - key_minus_query(win_keys)
        in_window = (w_dist >= 0) & (w_dist < WINDOW)
        for h in kv_heads:
            winb_ref[h] = jnp.where(in_window, band_bias(w0 // tq, win_tiles + 1, h), MASK_VALUE)

    last = qb // chunks

    @pl.when(b == 0)
    def _():
        causal = key_minus_query(kt) <= t0 - last * kt
        for h in kv_heads:
            finb_ref[h] = jnp.where(causal, band_bias(last * chunks, chunks, h), MASK_VALUE)

    q_pads = [jnp.concatenate(
        [jnp.where((lane // HEAD_DIM) == h, q_all[:, LANES * g:LANES * (g + 1)], jnp.zeros((), BF16))
         for g in range(NSA_GROUP)], axis=0) for h in kv_heads]

    k_win = kw_ref[0, pl.ds(w0, win_keys), :]
    v_win = value_blocks(vwt_ref, w0 // tq, win_tiles + 1)
    o_ws = []
    for h in kv_heads:
        s_w = _dot_nt(k_win, q_pads[h]) + winb_ref[h]
        p_w = jnp.exp(s_w - jnp.max(s_w, axis=0, keepdims=True))
        acc_w = _dot(with_ones(v_win, h), p_w.astype(BF16))
        sum_row = HEAD_DIM * (1 - h)
        o_ws.append(acc_w / acc_w[sum_row:sum_row + 1, :])

    q_augs, o_cs = [], []
    for h in kv_heads:
        q_pad = q_pads[h]

        s_c = _dot_nt(q_pad, ck).reshape(NSA_GROUP, tq, n_cmp) + biasc_ref[NSA_GROUP * h:NSA_GROUP * (h + 1)]
        s_c = jnp.where((cdist >= 0)[None], s_c, neg_inf)
        m_c = jnp.max(s_c, axis=-1, keepdims=True)
        m_c = jnp.where(m_c == neg_inf, 0.0, m_c)
        e_c = jnp.exp(s_c - m_c)
        sum_c = jnp.sum(e_c, axis=-1, keepdims=True)
        p_c = e_c / jnp.where(sum_c > 0.0, sum_c, 1.0)
        o_cs.append(_dot(p_c.reshape(rows, n_cmp).astype(BF16), cv))

        imp = p_c[0]
        for g in range(1, NSA_GROUP):
            imp = imp + p_c[g]
        imp = imp[:, :n_slc] + imp[:, n_slc:]
        if n_slc < LANES:
            imp = jnp.concatenate([imp, jnp.zeros((tq, LANES - n_slc), F32)], axis=1)
        cur = (t0 + row) // SLC_BLOCK
        forced = (lane == 0) | (lane == cur) | (lane == cur - 1)
        score = jnp.where(lane <= cur, jnp.where(forced, jnp.inf, imp), neg_inf)
        sel = _top_blocks_cols(score.T, SLC_TOPN).T
        neg_mask = (sel - 1.0).astype(BF16)
        q_augs.append(jnp.concatenate([q_pad, jnp.concatenate([neg_mask] * NSA_GROUP, axis=0)], axis=1))

    def tile_logits(j):
        k0 = pl.multiple_of(j * kt, kt)
        k_aug = jnp.concatenate([ks_ref[0, pl.ds(k0, kt), :], e_ref[pl.ds(k0, kt), :]], axis=1)
        return [_dot_nt(k_aug, q_augs[h]) for h in kv_heads]

    def tile_pv(j):
        v_t = value_blocks(vst_ref, j * chunks, chunks)
        return [_dot(with_ones(v_t, h), prob_ref[h]) for h in kv_heads]

    for h, s in enumerate(tile_logits(0)):
        logit_ref[h] = s
        prob_ref[h] = jnp.zeros((kt, rows), BF16)
        acc_ref[h] = jnp.zeros((KV_WIDTH, rows), F32)

    def slc_tile(j, m_old, biased, final=False):
        pv = tile_pv(jnp.maximum(j - 1, 0))
        logits = [logit_ref[h] for h in kv_heads]
        if final:
            logits = [s + finb_ref[h] for h, s in enumerate(logits)]
        elif biased:
            logits = [s + band_bias(j * chunks, chunks, h) for h, s in enumerate(logits)]
        m_new = [jnp.maximum(m_old[h], jnp.max(logits[h], axis=0, keepdims=True)) for h in kv_heads]
        probs = [jnp.exp(logits[h] - m_new[h]).astype(BF16) for h in kv_heads]
        nxt = None if final else tile_logits(j + 1)
        for h in kv_heads:
            acc_ref[h] = jnp.exp(m_old[h] - m_new[h]) * (acc_ref[h] + pv[h])
            prob_ref[h] = probs[h]
            if not final:
                logit_ref[h] = nxt[h]
        return tuple(m_new)

    n_far = jnp.maximum(qb - (near - 1), 0) // chunks
    m_run = (jnp.full((1, rows), neg_inf, F32),) * NSA_KV_HEADS
    m_run = lax.fori_loop(0, n_far, functools.partial(slc_tile, biased=False), m_run)
    m_run = lax.fori_loop(n_far, last, functools.partial(slc_tile, biased=True), m_run)
    slc_tile(last, m_run, biased=True, final=True)
    acc_slc = [acc_ref[h] + pv for h, pv in enumerate(tile_pv(last))]

    gates_t = gates.T
    for h in range(NSA_KV_HEADS):
        sum_row = HEAD_DIM * (1 - h)
        o_c = o_cs[h]
        acc_s = acc_slc[h]
        o_s = acc_s / acc_s[sum_row:sum_row + 1, :]
        o_w = o_ws[h]

        mixed = []
        for g in range(NSA_GROUP):
            hd = NSA_GROUP * h + g
            rs = slice(g * tq, (g + 1) * tq)
            key_major = (gates_t[NSA_HEADS + hd:NSA_HEADS + hd + 1, :] * o_s[:, rs]
                         + gates_t[2 * NSA_HEADS + hd:2 * NSA_HEADS + hd + 1, :] * o_w[:, rs])
            mixed.append(gates[:, hd:hd + 1] * o_c[rs] + key_major.T)
        for pair in range(NSA_GROUP // 2):
            a, c = mixed[2 * pair], mixed[2 * pair + 1]
            if h == 0:
                both = jnp.where(lane < HEAD_DIM, a, pltpu.roll(c, HEAD_DIM, 1))
            else:
                both = jnp.where(lane < HEAD_DIM, pltpu.roll(a, HEAD_DIM, 1), c)
            col = (NSA_GROUP * h + 2 * pair) * HEAD_DIM
            y_ref[0, :, col:col + LANES] = (both * _silu(z_ref[0, :, col:col + LANES])).astype(BF16)


def _nsa_prompt(rel_bias, q, ck, cv, ks, vs_t, kw, vw_t, e_mat, gates, z):
    b, t, _ = q.shape
    tq = Q_TILE
    n_cmp = ck.shape[1]
    assert t % SLC_KEYS == 0 and SLC_KEYS % tq == 0 and WINDOW % tq == 0 and t >= WINDOW + tq
    assert t // SLC_BLOCK <= LANES
    _, _, n_band = _band_geometry(tq)
    tile = lambda width: pl.BlockSpec((1, tq, width), lambda i, j: (j, i, 0))
    seq = lambda rows: pl.BlockSpec((1, rows, LANES), lambda i, j: (j, 0, 0))
    seq_t = pl.BlockSpec((1, t // LANES, LANES, LANES), lambda i, j: (j, 0, 0, 0))
    assert tq == LANES
    return pl.pallas_call(
        _nsa_prompt_kernel,
        grid=(t // tq, b),
        in_specs=[pl.BlockSpec(memory_space=pltpu.SMEM), tile(NSA_WIDTH), seq(n_cmp), seq(n_cmp),
                  seq(t), seq_t, seq(t), seq_t, pl.BlockSpec((t, LANES), lambda i, j: (0, 0)),
                  tile(LANES), tile(NSA_WIDTH)],
        out_specs=tile(NSA_WIDTH),
        out_shape=jax.ShapeDtypeStruct((b, t, NSA_WIDTH), BF16),
        scratch_shapes=[pltpu.VMEM((NSA_HEADS, tq, n_cmp), F32),
                        pltpu.VMEM((n_band, NSA_HEADS, tq, tq), F32),
                        pltpu.VMEM((NSA_KV_HEADS, SLC_KEYS, NSA_GROUP * tq), F32),
                        pltpu.VMEM((NSA_KV_HEADS, SLC_KEYS, NSA_GROUP * tq), BF16),
                        pltpu.VMEM((NSA_KV_HEADS, KV_WIDTH, NSA_GROUP * tq), F32),
                        pltpu.VMEM((NSA_KV_HEADS, WINDOW + tq, NSA_GROUP * tq), F32),
                        pltpu.VMEM((NSA_KV_HEADS, SLC_KEYS, NSA_GROUP * tq), F32)],
        compiler_params=pltpu.CompilerParams(dimension_semantics=("arbitrary", "arbitrary"),
                                             vmem_limit_bytes=VMEM_LIMIT),
        name="nsa_prompt",
    )(rel_bias, q, ck, cv, ks, vs_t, kw, vw_t, e_mat, gates, z)


def _nsa_sample_kernel(pt_ref, rel_ref, q_ref, ck_ref, cv_ref, pk_ref, pv_ref, e_ref, kwc_ref, vwc_ref, new_ref,
                       g_ref, z_ref, y_ref, kbuf, vbuf, kaug_ref, biasc_ref, biass_ref, biasw_ref, ksem, vsem):
    page_rows = pk_ref.shape[2]
    past = kbuf.shape[2]
    n_cmp = ck_ref.shape[1]
    n_slc = n_cmp // (SLC_BLOCK // CMP_BLOCK)
    win_rows = kwc_ref.shape[2]
    heads = tuple(range(NSA_HEADS))

    @pl.when(pl.program_id(0) == 0)
    def _():
        ccol = lax.broadcasted_iota(jnp.int32, (1, n_cmp), 1)
        cblock = (SLC_BLOCK // CMP_BLOCK) * (ccol % n_slc) + ccol // n_slc
        cdist = past - (CMP_BLOCK * cblock + CMP_BLOCK - 1)
        biasc_ref[...] = jnp.concatenate(_rel_bias_rows(cdist, rel_ref, heads), axis=0)
        sdist = past - lax.broadcasted_iota(jnp.int32, (1, past), 1)
        biass_ref[...] = jnp.concatenate(_rel_bias_rows(sdist, rel_ref, heads), axis=0)
        wdist = win_rows - lax.broadcasted_iota(jnp.int32, (1, win_rows), 1)
        biasw_ref[...] = jnp.concatenate(_rel_bias_rows(wdist, rel_ref, heads), axis=0)
        kaug_ref[KV_WIDTH:, :] = e_ref[...]

    slot = _double_buffered_gather(pt_ref, (pk_ref, pv_ref), (kbuf, vbuf), (ksem, vsem), past // page_rows)

    q8 = q_ref[0]
    q8f = q8.astype(F32)
    bias0 = jnp.concatenate(
        [jnp.full((1, 1), rel_ref[0, hd] - rel_ref[REL_BUCKETS - 1, hd], F32) for hd in heads], axis=0)
    lane = lax.broadcasted_iota(jnp.int32, (NSA_KV_HEADS, LANES), 1)

    s_c = _dot_nt(q8, ck_ref[0]) + biasc_ref[...]
    e_c = jnp.exp(s_c - jnp.max(s_c, axis=1, keepdims=True))
    p_c = e_c / jnp.sum(e_c, axis=1, keepdims=True)
    o_c = _dot(p_c.astype(BF16), cv_ref[0])

    imps = []
    for h in range(NSA_KV_HEADS):
        acc = p_c[NSA_GROUP * h:NSA_GROUP * h + 1]
        for g in range(1, NSA_GROUP):
            acc = acc + p_c[NSA_GROUP * h + g:NSA_GROUP * h + g + 1]
        imps.append(acc)
    imp = jnp.concatenate(imps, axis=0)
    imp = imp[:, :n_slc] + imp[:, n_slc:]
    if n_slc < LANES:
        imp = jnp.concatenate([imp, jnp.zeros((NSA_KV_HEADS, LANES - n_slc), F32)], axis=1)
    forced = (lane == 0) | (lane == n_slc - 1)
    score = jnp.where(lane < n_slc, jnp.where(forced, jnp.inf, imp), -jnp.inf)
    padded = jnp.concatenate([score, jnp.full((LANES - NSA_KV_HEADS, LANES), -jnp.inf, F32)], axis=0)
    sel = _top_blocks_cols(padded.T, SLC_TOPN - 1).T[:NSA_KV_HEADS]
    neg_mask = (sel - 1.0).astype(BF16)
    neg_rows = jnp.concatenate([neg_mask[h:h + 1] for h in range(NSA_KV_HEADS) for _ in range(NSA_GROUP)], axis=0)
    q_aug = jnp.concatenate([q8, neg_rows], axis=1)

    def with_new_token(s, s_new, v_t, v_new):
        m = jnp.maximum(jnp.max(s, axis=1, keepdims=True), s_new)
        p = jnp.exp(s - m)
        p_new = jnp.exp(s_new - m)
        total = jnp.sum(p, axis=1, keepdims=True) + p_new
        return (_dot_nt(p.astype(BF16), v_t) + p_new.astype(BF16).astype(F32) * v_new.astype(F32)) / total

    kaug_ref[:KV_WIDTH, :] = kbuf[slot].astype(BF16)
    s_s = _dot(q_aug, kaug_ref[...]) + biass_ref[...]
    s_new = jnp.sum(q8f * new_ref[0, 0:1, :].astype(F32), axis=1, keepdims=True) + bias0
    o_s = with_new_token(s_s, s_new, vbuf[slot].astype(BF16), new_ref[0, 1:2, :])

    wcol = lax.broadcasted_iota(jnp.int32, (1, win_rows), 1)
    s_w = _dot(q8, kwc_ref[0].astype(BF16)) + biasw_ref[...]
    s_w = jnp.where(win_rows - wcol < WINDOW, s_w, MASK_VALUE)
    s_new = jnp.sum(q8f * new_ref[0, 2:3, :].astype(F32), axis=1, keepdims=True) + bias0
    o_w = with_new_token(s_w, s_new, vwc_ref[0].astype(BF16), new_ref[0, 3:4, :])

    gates = g_ref[0]
    low = []
    for hd in heads:
        r = (gates[:, hd:hd + 1] * o_c[hd:hd + 1]
             + gates[:, NSA_HEADS + hd:NSA_HEADS + hd + 1] * o_s[hd:hd + 1]
             + gates[:, 2 * NSA_HEADS + hd:2 * NSA_HEADS + hd + 1] * o_w[hd:hd + 1])
        low.append(r if hd < NSA_GROUP else pltpu.roll(r, HEAD_DIM, 1))
    lane1 = lax.broadcasted_iota(jnp.int32, (1, LANES), 1)
    for pair in range(NSA_HEADS // 2):
        both = jnp.where(lane1 < HEAD_DIM, low[2 * pair], pltpu.roll(low[2 * pair + 1], HEAD_DIM, 1))
        cols = slice(LANES * pair, LANES * (pair + 1))
        y_ref[0, :, cols] = (both * _silu(z_ref[0, :, cols])).astype(BF16)


def _nsa_sample(page_table, rel_bias, q8, ck, cv, pool_k, pool_v, e_mat, win_k, win_v, new_rows, gates, z):
    n_seq, n_pages = page_table.shape
    page_rows = pool_k.shape[2]
    past = n_pages * page_rows
    n_cmp = ck.shape[1]
    win_rows = win_k.shape[2]
    assert past % SLC_BLOCK == 0 and past // SLC_BLOCK <= LANES and page_rows % LANES == 0
    hbm = pl.BlockSpec(memory_space=pl.ANY)
    per_seq = lambda r, w: pl.BlockSpec((1, r, w), lambda i, pt: (i, 0, 0))
    return pl.pallas_call(
        _nsa_sample_kernel,
        grid_spec=pltpu.PrefetchScalarGridSpec(
            num_scalar_prefetch=1,
            grid=(n_seq,),
            in_specs=[pl.BlockSpec(memory_space=pltpu.SMEM), per_seq(NSA_HEADS, LANES), per_seq(n_cmp, LANES),
                      per_seq(n_cmp, LANES), hbm, hbm,
                      pl.BlockSpec((LANES, past), lambda i, pt: (0, 0)),
                      per_seq(KV_WIDTH, win_rows), per_seq(KV_WIDTH, win_rows), per_seq(4, LANES),
                      per_seq(1, LANES), per_seq(1, NSA_WIDTH)],
            out_specs=per_seq(1, NSA_WIDTH),
            scratch_shapes=[pltpu.VMEM((2, KV_WIDTH, past), F32), pltpu.VMEM((2, KV_WIDTH, past), F32),
                            pltpu.VMEM((KV_WIDTH + LANES, past), BF16),
                            pltpu.VMEM((NSA_HEADS, n_cmp), F32), pltpu.VMEM((NSA_HEADS, past), F32),
                            pltpu.VMEM((NSA_HEADS, win_rows), F32),
                            pltpu.SemaphoreType.DMA((2,)), pltpu.SemaphoreType.DMA((2,))]),
        out_shape=jax.ShapeDtypeStruct((n_seq, 1, NSA_WIDTH), BF16),
        compiler_params=pltpu.CompilerParams(dimension_semantics=("arbitrary",), vmem_limit_bytes=VMEM_LIMIT),
        name="nsa_sample",
    )(page_table.reshape(-1), rel_bias, q8, ck, cv, pool_k, pool_v, e_mat, win_k, win_v, new_rows, gates, z)


def _group_norm_gate(o, z, gn, seg):
    mu = _segment_sum(o, seg) * (1.0 / RET_VDIM)
    d = o - mu
    var = _segment_sum(d * d, seg) * (1.0 / RET_VDIM)
    return (d * lax.rsqrt(var + EPS) * gn * _silu(z)).astype(BF16)


def _retention_prompt_kernel(q_ref, k_ref, v_ref, z_ref, gn_ref, seg_ref, dec_ref, qdec_ref, kdec_ref, cdec_ref,
                             y_ref, st_ref, state):
    chunk = pl.program_id(1)
    tc = q_ref.shape[1]
    lane = lax.broadcasted_iota(jnp.int32, (tc, LANES), 1)
    srow = lax.broadcasted_iota(jnp.int32, (LANES, LANES), 0)
    scol = lax.broadcasted_iota(jnp.int32, (LANES, LANES), 1)
    same_head = (srow // RET_KDIM) == (scol // RET_VDIM)

    @pl.when(chunk == 0)
    def _():
        state[...] = jnp.zeros_like(state)

    for pair in range(RET_HEADS // 2):
        cols = slice(LANES * pair, LANES * (pair + 1))
        q = q_ref[0, :, cols]
        k = k_ref[0, :, cols]
        v = v_ref[0, :, cols]
        s_old = state[pair]
        cross = _dot(q, s_old.astype(BF16)) * qdec_ref[:, cols]
        halves = []
        for e in range(2):
            qe = jnp.where((lane // RET_KDIM) == e, q, jnp.zeros((), BF16))
            scores = _dot_nt(qe, k) * dec_ref[2 * pair + e]
            halves.append(_dot(scores.astype(BF16), v))
        o = jnp.where(lane < RET_VDIM, halves[0], halves[1]) + cross
        y_ref[0, :, cols] = _group_norm_gate(o, z_ref[0, :, cols], gn_ref[:, cols], seg_ref[...])
        kd_t = (k.astype(F32) * kdec_ref[:, cols]).T.astype(BF16)
        s_new = s_old * cdec_ref[:, cols] + jnp.where(same_head, _dot(kd_t, v), 0.0)
        state[pair] = s_new

    @pl.when(chunk == pl.num_programs(1) - 1)
    def _():
        for head in range(RET_HEADS):
            e = head % 2
            st_ref[0, head] = state[head // 2, RET_KDIM * e:RET_KDIM * (e + 1), RET_VDIM * e:RET_VDIM * (e + 1)]


def _retention_tables(tc):
    log_g = jnp.asarray(RET_GAMMA_LOG, F32)
    i = jnp.arange(tc, dtype=F32)
    diff = i[:, None] - i[None, :]
    decay = jnp.where(diff >= 0, jnp.exp(jnp.maximum(diff, 0.0)[None] * log_g[:, None, None]), 0.0)
    widen = lambda a: jnp.repeat(a, RET_VDIM, axis=-1)
    q_decay = widen(jnp.exp((i[:, None] + 1.0) * log_g[None, :]))
    k_decay = widen(jnp.exp((tc - 1.0 - i)[:, None] * log_g[None, :]))
    c_decay = widen(jnp.exp(tc * log_g)[None, :])
    return decay, q_decay, k_decay, c_decay


def _retention_prompt(q, k, v, z, gn, seg):
    b, t, _ = q.shape
    tc = RET_TILE
    assert t % tc == 0
    decay, q_decay, k_decay, c_decay = _retention_tables(tc)
    tile = pl.BlockSpec((1, tc, RET_WIDTH), lambda i, j: (i, j, 0))
    const = lambda a: pl.BlockSpec(a.shape, lambda i, j: (0,) * a.ndim)
    return pl.pallas_call(
        _retention_prompt_kernel,
        grid=(b, t // tc),
        in_specs=[tile, tile, tile, tile, const(gn), const(seg), const(decay), const(q_decay), const(k_decay),
                  const(c_decay)],
        out_specs=[tile, pl.BlockSpec((1, RET_HEADS, RET_KDIM, RET_VDIM), lambda i, j: (i, 0, 0, 0))],
        out_shape=[jax.ShapeDtypeStruct((b, t, RET_WIDTH), BF16),
                   jax.ShapeDtypeStruct((b, RET_HEADS, RET_KDIM, RET_VDIM), F32)],
        scratch_shapes=[pltpu.VMEM((RET_HEADS // 2, LANES, LANES), F32)],
        compiler_params=pltpu.CompilerParams(dimension_semantics=("arbitrary", "arbitrary"),
                                             vmem_limit_bytes=VMEM_LIMIT),
        name="retention_prompt",
    )(q, k, v, z, gn, seg, decay, q_decay, k_decay, c_decay)


def _retention_sample_kernel(qt_ref, kt_ref, v_ref, z_ref, gn_ref, gam_ref, st_ref, y_ref, so_ref):
    n = v_ref.shape[1]
    qt = qt_ref[0]
    kt = kt_ref[0]
    gam = gam_ref[...]
    qk = jnp.sum(qt * kt, axis=1)
    for s in range(n):
        st = st_ref[s]
        vs = v_ref[0, s]
        cross = jnp.sum(qt[:, :, s:s + 1] * st, axis=1)
        so_ref[s] = st * gam[:, :, None] + kt[:, :, s:s + 1] * vs[:, None, :]
        o = qk[:, s:s + 1] * vs + gam * cross
        mu = jnp.mean(o, axis=-1, keepdims=True)
        d = o - mu
        var = jnp.mean(d * d, axis=-1, keepdims=True)
        y_ref[0, s] = (d * lax.rsqrt(var + EPS) * gn_ref[...] * _silu(z_ref[0, s])).astype(BF16)


def _retention_sample(q, k, v, z, gn, state):
    n_seq = q.shape[0]
    n = min(RET_SAMPLE_SEQS, n_seq)
    assert n_seq % n == 0
    steps = n_seq // n
    to_cols = lambda a: a.astype(F32).reshape(steps, n, RET_HEADS, RET_KDIM).transpose(0, 2, 3, 1)
    to_rows = lambda a: a.astype(F32).reshape(steps, n, RET_HEADS, RET_VDIM)
    gam = jnp.exp(jnp.asarray(RET_GAMMA_LOG, F32)).reshape(RET_HEADS, 1)
    cols = pl.BlockSpec((1, RET_HEADS, RET_KDIM, n), lambda i: (i, 0, 0, 0))
    rws = pl.BlockSpec((1, n, RET_HEADS, RET_VDIM), lambda i: (i, 0, 0, 0))
    st = pl.BlockSpec((n, RET_HEADS, RET_KDIM, RET_VDIM), lambda i: (i, 0, 0, 0))
    y, new_state = pl.pallas_call(
        _retention_sample_kernel,
        grid=(steps,),
        in_specs=[cols, cols, rws, rws, pl.BlockSpec((RET_HEADS, RET_VDIM), lambda i: (0, 0)),
                  pl.BlockSpec((RET_HEADS, 1), lambda i: (0, 0)), st],
        out_specs=[rws, st],
        out_shape=[jax.ShapeDtypeStruct((steps, n, RET_HEADS, RET_VDIM), BF16),
                   jax.ShapeDtypeStruct(state.shape, F32)],
        compiler_params=pltpu.CompilerParams(dimension_semantics=("parallel",), vmem_limit_bytes=VMEM_LIMIT),
        name="retention_sample",
    )(to_cols(q), to_cols(k), to_rows(v), to_rows(z), gn.reshape(RET_HEADS, RET_VDIM), gam, state)
    return y.reshape(n_seq, RET_WIDTH), new_state


def _out_kernel(x_ref, yn_ref, yr_ref, w_ref, o_ref):
    y = jnp.concatenate([yn_ref[0], yr_ref[0]], axis=1)
    o_ref[0] = x_ref[0] + _dot(y, w_ref[...])


def _out_project(x, y_nsa, y_ret, w):
    nb, rows, d = x.shape
    tm = min(PROJ_ROWS, rows)
    spec = lambda width: pl.BlockSpec((1, tm, width), lambda i, j: (i, j, 0))
    return pl.pallas_call(
        _out_kernel,
        grid=(nb, rows // tm),
        in_specs=[spec(d), spec(NSA_WIDTH), spec(RET_WIDTH), pl.BlockSpec(w.shape, lambda i, j: (0, 0))],
        out_specs=spec(d),
        out_shape=jax.ShapeDtypeStruct(x.shape, F32),
        compiler_params=pltpu.CompilerParams(dimension_semantics=("parallel", "parallel"),
                                             vmem_limit_bytes=VMEM_LIMIT),
        name="output_projection",
    )(x, y_nsa, y_ret, w)


def _rotary_tables(pos):
    half = RET_KDIM // 2
    inv = ROPE_BASE ** (-jnp.arange(half, dtype=F32) / half)
    ang = pos.astype(F32)[:, None] * inv[None, :]
    cos, sin = jnp.cos(ang), jnp.sin(ang)
    reps = LANES // RET_KDIM
    return jnp.tile(jnp.concatenate([cos, cos], axis=1), (1, reps)), jnp.tile(jnp.concatenate([-sin, sin], axis=1), (1, reps))


def _arrange_w_in(w_in):
    d = w_in.shape[0]
    parts = jnp.split(w_in, np.cumsum(SPLIT_WIDTHS)[:-1].tolist(), axis=1)
    order = [hd for g in range(NSA_GROUP) for hd in (g, NSA_GROUP + g)]
    q = parts[0].reshape(d, NSA_HEADS, HEAD_DIM)[:, order].reshape(d, NSA_WIDTH)
    gates = parts[7].reshape(d, NSA_HEADS, 3).transpose(0, 2, 1).reshape(d, 3 * NSA_HEADS)
    gates = jnp.pad(gates, ((0, 0), (0, LANES - 3 * NSA_HEADS)))
    w = jnp.concatenate([q] + parts[1:7] + parts[8:] + [gates], axis=1)
    assert w.shape[1] == COL_END
    return w.astype(BF16)


def _compress_weights(pe, w1, w2):
    w1r = w1.reshape(CMP_BLOCK, HEAD_DIM, HEAD_DIM)
    w1_both = jnp.zeros((CMP_BLOCK, NSA_KV_HEADS, HEAD_DIM, NSA_KV_HEADS, HEAD_DIM), F32)
    w2_both = jnp.zeros((NSA_KV_HEADS, HEAD_DIM, NSA_KV_HEADS, HEAD_DIM), F32)
    for h in range(NSA_KV_HEADS):
        w1_both = w1_both.at[:, h, :, h, :].set(w1r)
        w2_both = w2_both.at[h, :, h, :].set(w2)
    return (jnp.tile(pe, (1, NSA_KV_HEADS)), w1_both.reshape(CMP_BLOCK * KV_WIDTH, KV_WIDTH).astype(BF16),
            w2_both.reshape(KV_WIDTH, KV_WIDTH).astype(BF16))


def _even_odd(c):
    n, nc, w = c.shape
    ratio = SLC_BLOCK // CMP_BLOCK
    return c.reshape(n, nc // ratio, ratio, w).transpose(0, 2, 1, 3).reshape(n, nc, w)


def _block_membership(n_keys):
    k = jnp.arange(n_keys)[:, None] // SLC_BLOCK
    return jnp.where(k == jnp.arange(LANES)[None, :], -MASK_VALUE, 0.0).astype(BF16)


def kernel(x_prompt, x_sample, cache_cmp_k, cache_cmp_v, cache_slc_k, cache_slc_v, cache_win_k, cache_win_v,
           state_ret, page_table, norm_g, w_in, q_norm_g, k_norm_g, cmp_pe_k, cmp_w1_k, cmp_w2_k, cmp_pe_v,
           cmp_w1_v, cmp_w2_v, rel_bias, ret_gn_g, w_out):
    b, t, d = x_prompt.shape
    n_seq, dec_len, _ = x_sample.shape
    n_pages = page_table.shape[1]
    page_rows = cache_cmp_k.shape[1]
    past = n_pages * page_rows
    assert dec_len == 1 and past % CMP_BLOCK == 0
    kv4 = lambda a: a.reshape(a.shape[0], a.shape[1], NSA_KV_HEADS, HEAD_DIM)

    w = _arrange_w_in(w_in)
    w_o = w_out.astype(BF16)
    ng = norm_g.reshape(1, d)
    qg = (jnp.tile(q_norm_g, LANES // HEAD_DIM) * ATTN_SCALE).reshape(1, LANES)
    kg = jnp.tile(k_norm_g, LANES // HEAD_DIM).reshape(1, LANES)
    gn = ret_gn_g.reshape(1, RET_WIDTH)
    lane = np.arange(LANES)
    seg = jnp.asarray(lane[:, None] // HEAD_DIM == lane[None, :] // HEAD_DIM, BF16)
    cw = _compress_weights(cmp_pe_k, cmp_w1_k, cmp_w2_k) + _compress_weights(cmp_pe_v, cmp_w1_v, cmp_w2_v)

    cos, sin = _rotary_tables(jnp.arange(t))
    (q, kc, vc, ks, vs, kw, vw, ksb, _, kwb, _, gates, zn, qr, kr, vr, zr, vs_t, vw_t) = _project(
        x_prompt, cos, sin, ng, w, qg, kg, seg)
    ck, cv = _compress_prompt(kc, vc, cw, kg, seg)
    y_nsa = _nsa_prompt(rel_bias, q, _even_odd(ck), _even_odd(cv), ksb, vs_t, kwb, vw_t, _block_membership(t), gates, zn)
    y_ret, p_ret = _retention_prompt(qr, kr, vr, zr, gn, seg)
    y_prompt = _out_project(x_prompt, y_nsa, y_ret, w_o)
    keep = min(WINDOW, t)
    prompt_out = (y_prompt, kv4(kc), kv4(vc), kv4(ks), kv4(vs), kv4(kw[:, t - keep:]), kv4(vw[:, t - keep:]), p_ret)

    cos, sin = _rotary_tables(jnp.full((n_seq,), past))
    xs = x_sample.reshape(1, n_seq, d)
    (q, kc, vc, ks, vs, kw, vw, ksb, vsb, kwb, vwb, gates, zn, qr, kr, vr, zr) = [
        a[0] for a in _project(xs, cos, sin, ng, w, qg, kg, seg)[:17]]
    chan = lambda a: jnp.transpose(a, (0, 2, 3, 1)).reshape(a.shape[0], KV_WIDTH, a.shape[1])
    ck, cv = _compress_sample(page_table, chan(cache_cmp_k), chan(cache_cmp_v), cw, kg, seg)
    half = jnp.asarray(lane[None, :] // HEAD_DIM == (np.arange(NSA_HEADS) // NSA_GROUP)[:, None])
    q8 = jnp.where(half[None], jnp.tile(q.reshape(n_seq, NSA_GROUP, LANES), (1, NSA_KV_HEADS, 1)), jnp.zeros((), BF16))
    new_rows = jnp.stack([ksb, vsb, kwb, vwb], axis=1)
    y_nsa = _nsa_sample(page_table, rel_bias, q8, _even_odd(ck), _even_odd(cv), chan(cache_slc_k), chan(cache_slc_v),
                        _block_membership(past).T, chan(cache_win_k), chan(cache_win_v), new_rows,
                        gates.reshape(n_seq, 1, LANES), zn.reshape(n_seq, 1, NSA_WIDTH))
    y_ret, s_ret = _retention_sample(qr, kr, vr, zr, gn, state_ret)
    y_sample = _out_project(xs, y_nsa.reshape(1, n_seq, NSA_WIDTH), y_ret.reshape(1, n_seq, RET_WIDTH), w_o)
    keep = min(WINDOW, cache_win_k.shape[1] + 1)
    new4 = lambda a: a.reshape(n_seq, 1, NSA_KV_HEADS, HEAD_DIM)
    s_win_k = jnp.concatenate([cache_win_k, new4(kw)], axis=1)[:, -keep:]
    s_win_v = jnp.concatenate([cache_win_v, new4(vw)], axis=1)[:, -keep:]
    sample_out = (y_sample.reshape(n_seq, 1, d), new4(kc), new4(vc), new4(ks), new4(vs), s_win_k, s_win_v, s_ret)

    return (prompt_out[0], sample_out[0]) + prompt_out[1:] + sample_out[1:]
```

```python
import functools

import numpy as np
import jax
import jax.numpy as jnp
from jax import lax
from jax.experimental import pallas as pl
from jax.experimental.pallas import tpu as pltpu

F32, BF16 = jnp.float32, jnp.bfloat16

NSA_HEADS = 8
NSA_KV_HEADS = 2
HEAD_DIM = 64
NSA_GROUP = NSA_HEADS // NSA_KV_HEADS
NSA_WIDTH = NSA_HEADS * HEAD_DIM
KV_WIDTH = NSA_KV_HEADS * HEAD_DIM
CMP_BLOCK = 32
SLC_BLOCK = 64
SLC_TOPN = 16
WINDOW = 512
ATTN_SCALE = HEAD_DIM ** -0.5
RET_HEADS = 8
RET_KDIM = 64
RET_VDIM = 64
RET_WIDTH = RET_HEADS * RET_VDIM
ROPE_BASE = 10000.0
REL_BUCKETS = 32
REL_MAX_DIST = 1024
EPS = 1e-6
SPLIT_WIDTHS = (NSA_WIDTH, KV_WIDTH, KV_WIDTH, KV_WIDTH, KV_WIDTH, KV_WIDTH, KV_WIDTH,
                3 * NSA_HEADS, NSA_WIDTH, RET_HEADS * RET_KDIM, RET_HEADS * RET_KDIM, RET_WIDTH, RET_WIDTH)

LANES = 128
VMEM_LIMIT = 56 * 1024 * 1024

PROJ_ROWS = 512
Q_TILE = 128
SLC_KEYS = 512
RET_TILE = 512
RET_SAMPLE_SEQS = 16
CMP_PITCH = CMP_BLOCK + 4

MASK_VALUE = -float(2 ** 30)

MXU_COLS = 256
COL_Q = 0
COL_KC = COL_Q + NSA_WIDTH
COL_VC = COL_KC + LANES
COL_KS = COL_VC + LANES
COL_VS = COL_KS + LANES
COL_KW = COL_VS + LANES
COL_VW = COL_KW + LANES
COL_ZN = COL_VW + LANES
COL_QR = COL_ZN + NSA_WIDTH
COL_KR = COL_QR + RET_WIDTH
COL_VR = COL_KR + RET_WIDTH
COL_ZR = COL_VR + RET_WIDTH
COL_G = COL_ZR + RET_WIDTH
COL_END = COL_G + LANES


def _bucket_lower_bounds():
    exact = REL_BUCKETS // 2
    ratio = REL_MAX_DIST // exact
    lows = list(range(exact))
    n = exact
    for k in range(REL_BUCKETS - exact):
        while n ** (REL_BUCKETS - exact) < exact ** (REL_BUCKETS - exact) * ratio ** k:
            n += 1
        lows.append(n)
    return tuple(lows)


BUCKET_LOW = _bucket_lower_bounds()
FAR_DIST = BUCKET_LOW[-1]
RET_GAMMA_LOG = tuple(float(np.log1p(-np.exp2(-5.0 - h))) for h in range(RET_HEADS))


def _dot(a, b):
    return jnp.dot(a, b, preferred_element_type=F32)


def _dot_nt(a, b):
    return lax.dot_general(a, b, (((1,), (1,)), ((), ())), preferred_element_type=F32)


def _segment_sum(v, seg):
    hi = v.astype(BF16)
    lo = (v - hi.astype(F32)).astype(BF16)
    return _dot(hi, seg) + _dot(lo, seg)


def _head_rms(y, g, seg):
    ms = _segment_sum(y * y, seg) * (1.0 / HEAD_DIM)
    return y * lax.rsqrt(ms + EPS) * g


def _silu(x):
    return x * jax.nn.sigmoid(x)


def _rel_bias_rows(dist, rel_ref, heads):
    out = [jnp.full(dist.shape, rel_ref[0, hd], F32) for hd in heads]
    for bkt in range(1, REL_BUCKETS):
        hit = dist >= BUCKET_LOW[bkt]
        out = [jnp.where(hit, rel_ref[bkt, hd], o) for hd, o in zip(heads, out)]
    return [o - rel_ref[REL_BUCKETS - 1, hd] for hd, o in zip(heads, out)]


def _top_blocks_cols(score, n_pick):
    n_rows = score.shape[0]
    rowi = lax.broadcasted_iota(jnp.int32, score.shape, 0).astype(F32)
    neg_inf = -jnp.inf
    left = score
    for _ in range(n_pick):
        m = jnp.max(left, axis=0, keepdims=True)
        first = jnp.min(jnp.where(left == m, rowi, float(n_rows)), axis=0, keepdims=True)
        left = jnp.where(rowi == first, neg_inf, left)
    return jnp.where(left == neg_inf, jnp.where(score > neg_inf, 1.0, 0.0), 0.0)


def _proj_kernel(x_ref, cos_ref, sin_ref, ng_ref, w_ref, qg_ref, kg_ref, seg_ref,
                 q_o, kc_o, vc_o, ks_o, vs_o, kw_o, vw_o, ksb_o, vsb_o, kwb_o, vwb_o,
                 g_o, zn_o, qr_o, kr_o, vr_o, zr_o, vst_o, vwt_o):
    x = x_ref[0]
    inv = lax.rsqrt(jnp.mean(x * x, axis=-1, keepdims=True) + EPS)
    xn = (x * inv * ng_ref[...]).astype(BF16)
    seg = seg_ref[...]

    def mm(col, width):
        return _dot(xn, w_ref[:, col:col + width])

    def chunk(y, c):
        return y[:, LANES * c:LANES * (c + 1)]

    y = mm(COL_Q, NSA_WIDTH)
    for c in range(NSA_WIDTH // LANES):
        q_o[0, :, LANES * c:LANES * (c + 1)] = _head_rms(chunk(y, c), qg_ref[...], seg).astype(BF16)
    y = mm(COL_KC, 2 * LANES)
    kc_o[0] = chunk(y, 0)
    vc_o[0] = chunk(y, 1)
    for col, k_o, v_o, kb_o, vb_o, vt_o in ((COL_KS, ks_o, vs_o, ksb_o, vsb_o, vst_o),
                                            (COL_KW, kw_o, vw_o, kwb_o, vwb_o, vwt_o)):
        y = mm(col, 2 * LANES)
        k = _head_rms(chunk(y, 0), kg_ref[...], seg)
        k_o[0] = k
        kb_o[0] = k.astype(BF16)
        v = chunk(y, 1)
        v_o[0] = v
        vb_o[0] = v.astype(BF16)
        for c in range(v.shape[0] // LANES):
            vt_o[0, c] = v[LANES * c:LANES * (c + 1)].T.astype(BF16)
    zn_o[0] = mm(COL_ZN, NSA_WIDTH)

    cos = cos_ref[...]
    sin = sin_ref[...]
    lane = lax.broadcasted_iota(jnp.int32, cos.shape, 1)
    first_half = (lane % HEAD_DIM) < (HEAD_DIM // 2)
    for col, o_ref, scale in ((COL_QR, qr_o, None), (COL_KR, kr_o, RET_KDIM ** -0.5)):
        y = mm(col, RET_WIDTH)
        for c in range(RET_WIDTH // LANES):
            yc = chunk(y, c)
            partner = jnp.where(first_half, pltpu.roll(yc, LANES - HEAD_DIM // 2, 1),
                                pltpu.roll(yc, HEAD_DIM // 2, 1))
            r = yc * cos + partner * sin
            if scale is not None:
                r = r * scale
            o_ref[0, :, LANES * c:LANES * (c + 1)] = r.astype(BF16)
    vr_o[0] = mm(COL_VR, RET_WIDTH).astype(BF16)
    zr_o[0] = mm(COL_ZR, RET_WIDTH)
    g_o[0] = jax.nn.sigmoid(mm(COL_G, LANES))


def _project(x, cos, sin, norm_g, w, qg, kg, seg):
    nb, rows, d = x.shape
    tm = min(PROJ_ROWS, rows)
    assert rows % tm == 0 and tm % LANES == 0
    row_spec = lambda width: pl.BlockSpec((1, tm, width), lambda i, j: (i, j, 0))
    const = lambda shape: pl.BlockSpec(shape, lambda i, j: (0,) * len(shape))
    outs = [(NSA_WIDTH, BF16)] + [(LANES, F32)] * 6 + [(LANES, BF16)] * 4 + [(LANES, F32), (NSA_WIDTH, F32),
            (RET_WIDTH, BF16), (RET_WIDTH, BF16), (RET_WIDTH, BF16), (RET_WIDTH, F32)]
    blocks_spec = pl.BlockSpec((1, tm // LANES, LANES, LANES), lambda i, j: (i, j, 0, 0))
    blocks_shape = jax.ShapeDtypeStruct((nb, rows // LANES, LANES, LANES), BF16)
    return pl.pallas_call(
        _proj_kernel,
        grid=(nb, rows // tm),
        in_specs=[row_spec(d), pl.BlockSpec((tm, LANES), lambda i, j: (j, 0)),
                  pl.BlockSpec((tm, LANES), lambda i, j: (j, 0)), const((1, d)), const(w.shape),
                  const((1, LANES)), const((1, LANES)), const((LANES, LANES))],
        out_specs=[row_spec(wd) for wd, _ in outs] + [blocks_spec] * 2,
        out_shape=[jax.ShapeDtypeStruct((nb, rows, wd), dt) for wd, dt in outs] + [blocks_shape] * 2,
        compiler_params=pltpu.CompilerParams(dimension_semantics=("parallel", "parallel"),
                                             vmem_limit_bytes=VMEM_LIMIT),
        name="input_projection",
    )(x, cos, sin, norm_g, w, qg, kg, seg)


def _compress_rows(buf_ref, pe_ref, w1_ref, w2_ref, n_blk, pitch=CMP_BLOCK):
    parts = []
    for i in range(CMP_BLOCK):
        rows = buf_ref[pl.ds(i, n_blk, stride=pitch), :] + pe_ref[i:i + 1, :]
        parts.append(rows.astype(BF16))
    flat = jnp.concatenate(parts, axis=1)
    hidden = _silu(_dot(flat, w1_ref[...]))
    return _dot(hidden.astype(BF16), w2_ref[...])


def _compress_kernel(kc_ref, vc_ref, pek_ref, w1k_ref, w2k_ref, pev_ref, w1v_ref, w2v_ref, kg_ref, seg_ref,
                     ck_o, cv_o):
    n_blk = ck_o.shape[1]
    ck = _compress_rows(kc_ref.at[0], pek_ref, w1k_ref, w2k_ref, n_blk)
    ck_o[0] = _head_rms(ck, kg_ref[...], seg_ref[...]).astype(BF16)
    cv_o[0] = _compress_rows(vc_ref.at[0], pev_ref, w1v_ref, w2v_ref, n_blk).astype(BF16)


def _compress_prompt(kc, vc, cw, kg, seg):
    b, t, _ = kc.shape
    n_blk = t // CMP_BLOCK
    seq = pl.BlockSpec((1, t, LANES), lambda i: (i, 0, 0))
    const = lambda a: pl.BlockSpec(a.shape, lambda i: (0,) * a.ndim)
    out = pl.BlockSpec((1, n_blk, LANES), lambda i: (i, 0, 0))
    return pl.pallas_call(
        _compress_kernel,
        grid=(b,),
        in_specs=[seq, seq] + [const(a) for a in cw] + [const(kg), const(seg)],
        out_specs=[out, out],
        out_shape=[jax.ShapeDtypeStruct((b, n_blk, LANES), BF16)] * 2,
        compiler_params=pltpu.CompilerParams(dimension_semantics=("parallel",), vmem_limit_bytes=VMEM_LIMIT),
        name="compress_prompt",
    )(kc, vc, *cw, kg, seg)


def _page_copy(pt_ref, pool_ref, buf_ref, sem, seq, page, n_pages):
    rows = pool_ref.shape[2]
    return pltpu.make_async_copy(pool_ref.at[pt_ref[seq * n_pages + page]],
                                 buf_ref.at[:, pl.ds(page * rows, rows)], sem)


def _gather_pages(pt_ref, pools, bufs, sems, seq, slot, n_pages, wait):
    for pool_ref, buf_ref, sem in zip(pools, bufs, sems):
        for page in range(n_pages):
            copy = _page_copy(pt_ref, pool_ref, buf_ref.at[slot], sem.at[slot], seq, page, n_pages)
            if wait:
                copy.wait()
            else:
                copy.start(priority=page % 2)


def _double_buffered_gather(pt_ref, pools, bufs, sems, n_pages):
    seq = pl.program_id(0)
    slot = seq % 2

    @pl.when(seq == 0)
    def _():
        _gather_pages(pt_ref, pools, bufs, sems, seq, slot, n_pages, wait=False)

    @pl.when(seq + 1 < pl.num_programs(0))
    def _():
        _gather_pages(pt_ref, pools, bufs, sems, seq + 1, 1 - slot, n_pages, wait=False)

    _gather_pages(pt_ref, pools, bufs, sems, seq, slot, n_pages, wait=True)
    return slot


def _compress_sample_kernel(pt_ref, pk_ref, pv_ref, pek_ref, w1k_ref, w2k_ref, pev_ref, w1v_ref, w2v_ref,
                            kg_ref, seg_ref, ck_o, cv_o, kbuf, vbuf, rows_ref, ksem, vsem):
    page_rows = pk_ref.shape[2]
    n_pages = kbuf.shape[2] // page_rows
    n_blk = ck_o.shape[1]
    slot = _double_buffered_gather(pt_ref, (pk_ref, pv_ref), (kbuf, vbuf), (ksem, vsem), n_pages)

    blocks_per_page = page_rows // CMP_BLOCK

    def token_major(buf):
        for page in range(n_pages):
            tokens = buf[slot, :, page * page_rows:(page + 1) * page_rows].T
            for c in range(blocks_per_page):
                r0 = (page * blocks_per_page + c) * CMP_PITCH
                rows_ref[r0:r0 + CMP_BLOCK, :] = tokens[c * CMP_BLOCK:(c + 1) * CMP_BLOCK]

    token_major(kbuf)
    ck = _compress_rows(rows_ref, pek_ref, w1k_ref, w2k_ref, n_blk, CMP_PITCH)
    ck_o[0] = _head_rms(ck, kg_ref[...], seg_ref[...]).astype(BF16)
    token_major(vbuf)
    cv_o[0] = _compress_rows(rows_ref, pev_ref, w1v_ref, w2v_ref, n_blk, CMP_PITCH).astype(BF16)


def _compress_sample(page_table, pool_k, pool_v, cw, kg, seg):
    n_seq, n_pages = page_table.shape
    page_rows = pool_k.shape[2]
    assert page_rows == LANES
    past = n_pages * page_rows
    n_blk = past // CMP_BLOCK
    hbm = pl.BlockSpec(memory_space=pl.ANY)
    const = lambda a: pl.BlockSpec(a.shape, lambda i, pt: (0,) * a.ndim)
    out = pl.BlockSpec((1, n_blk, LANES), lambda i, pt: (i, 0, 0))
    return pl.pallas_call(
        _compress_sample_kernel,
        grid_spec=pltpu.PrefetchScalarGridSpec(
            num_scalar_prefetch=1,
            grid=(n_seq,),
            in_specs=[hbm, hbm] + [const(a) for a in cw] + [const(kg), const(seg)],
            out_specs=[out, out],
            scratch_shapes=[pltpu.VMEM((2, KV_WIDTH, past), F32), pltpu.VMEM((2, KV_WIDTH, past), F32),
                            pltpu.VMEM((n_blk * CMP_PITCH, KV_WIDTH), F32),
                            pltpu.SemaphoreType.DMA((2,)), pltpu.SemaphoreType.DMA((2,))]),
        out_shape=[jax.ShapeDtypeStruct((n_seq, n_blk, LANES), BF16)] * 2,
        compiler_params=pltpu.CompilerParams(dimension_semantics=("arbitrary",), vmem_limit_bytes=VMEM_LIMIT),
        name="compress_sample",
    )(page_table.reshape(-1), pool_k, pool_v, *cw, kg, seg)


def _band_geometry(tq):
    near = -(-(FAR_DIST + tq - 1) // tq)
    rel_max = near + SLC_KEYS // tq - 2
    rel_min = -max(WINDOW // tq, SLC_KEYS // tq - 1)
    return near, rel_max, rel_max - rel_min + 1


def _nsa_prompt_kernel(rel_ref, q_ref, ck_ref, cv_ref, ks_ref, vst_ref, kw_ref, vwt_ref, e_ref, g_ref, z_ref,
                       y_ref, biasc_ref, band_ref, logit_ref, prob_ref, acc_ref, winb_ref, finb_ref):
    tq = q_ref.shape[1]
    n_cmp = ck_ref.shape[1]
    n_slc = n_cmp // (SLC_BLOCK // CMP_BLOCK)
    rows = NSA_GROUP * tq
    kt = SLC_KEYS
    chunks = kt // tq
    near, rel_max, n_band = _band_geometry(tq)
    win_tiles = WINDOW // tq
    win_keys = WINDOW + tq
    qb = pl.program_id(0)
    b = pl.program_id(1)
    t0 = qb * tq
    all_heads = tuple(range(NSA_HEADS))

    lane = lax.broadcasted_iota(jnp.int32, (tq, LANES), 1)
    row = lax.broadcasted_iota(jnp.int32, (tq, LANES), 0)

    @pl.when((qb == 0) & (b == 0))
    def _():
        for c in range(n_band):
            dist = lane - row + tq * (rel_max - c)
            for hd, bias in zip(all_heads, _rel_bias_rows(dist, rel_ref, all_heads)):
                band_ref[c, hd] = bias

    def band_bias(first_chunk, n_chunks, h):
        base = rel_max - qb + first_chunk
        return jnp.concatenate(
            [jnp.concatenate([band_ref[base + u, NSA_GROUP * h + g] for g in range(NSA_GROUP)], axis=1)
             for u in range(n_chunks)], axis=0)

    ccol = lax.broadcasted_iota(jnp.int32, (tq, n_cmp), 1)
    crow = lax.broadcasted_iota(jnp.int32, (tq, n_cmp), 0)
    cblock = (SLC_BLOCK // CMP_BLOCK) * (ccol % n_slc) + ccol // n_slc
    cdist = t0 + crow - (CMP_BLOCK * cblock + CMP_BLOCK - 1)

    @pl.when(b == 0)
    def _():
        for hd, bias in zip(all_heads, _rel_bias_rows(cdist, rel_ref, all_heads)):
            biasc_ref[hd] = bias

    q_all = q_ref[0]
    gates = g_ref[0]
    ck = ck_ref[0]
    cv = cv_ref[0]
    neg_inf = -jnp.inf
    def key_minus_query(n_keys):
        return (lax.broadcasted_iota(jnp.int32, (n_keys, rows), 0)
                - lax.broadcasted_iota(jnp.int32, (n_keys, rows), 1) % tq)

    ones = jnp.ones((), BF16)

    def with_ones(v_t, h):
        own = (lax.broadcasted_iota(jnp.int32, v_t.shape, 0) // HEAD_DIM) == h
        return jnp.where(own, v_t, ones)

    def value_blocks(ref, first_chunk, n_chunks):
        return jnp.concatenate([ref[0, first_chunk + u] for u in range(n_chunks)], axis=1)

    kv_heads = range(NSA_KV_HEADS)
    w0 = pl.multiple_of(jnp.maximum(t0 - WINDOW, 0), tq)

    @pl.when((b == 0) & (qb <= win_tiles))
    def _():
        w_dist = (t0 - w0) - key_minus_query(win_keys)
        in_window = (w_dist >= 0) & (w_dist < WINDOW)
        for h in kv_heads:
            winb_ref[h] = jnp.where(in_window, band_bias(w0 // tq, win_tiles + 1, h), MASK_VALUE)

    last = qb // chunks

    @pl.when(b == 0)
    def _():
        causal = key_minus_query(kt) <= t0 - last * kt
        for h in kv_heads:
            finb_ref[h] = jnp.where(causal, band_bias(last * chunks, chunks, h), MASK_VALUE)

    q_pads = [jnp.concatenate(
        [jnp.where((lane // HEAD_DIM) == h, q_all[:, LANES * g:LANES * (g + 1)], jnp.zeros((), BF16))
         for g in range(NSA_GROUP)], axis=0) for h in kv_heads]

    k_win = kw_ref[0, pl.ds(w0, win_keys), :]
    v_win = value_blocks(vwt_ref, w0 // tq, win_tiles + 1)
    o_ws = []
    for h in kv_heads:
        s_w = _dot_nt(k_win, q_pads[h]) + winb_ref[h]
        p_w = jnp.exp(s_w - jnp.max(s_w, axis=0, keepdims=True))
        acc_w = _dot(with_ones(v_win, h), p_w.astype(BF16))
        sum_row = HEAD_DIM * (1 - h)
        o_ws.append(acc_w / acc_w[sum_row:sum_row + 1, :])

    q_augs, o_cs = [], []
    for h in kv_heads:
        q_pad = q_pads[h]

        s_c = _dot_nt(q_pad, ck).reshape(NSA_GROUP, tq, n_cmp) + biasc_ref[NSA_GROUP * h:NSA_GROUP * (h + 1)]
        s_c = jnp.where((cdist >= 0)[None], s_c, neg_inf)
        m_c = jnp.max(s_c, axis=-1, keepdims=True)
        m_c = jnp.where(m_c == neg_inf, 0.0, m_c)
        e_c = jnp.exp(s_c - m_c)
        sum_c = jnp.sum(e_c, axis=-1, keepdims=True)
        p_c = e_c / jnp.where(sum_c > 0.0, sum_c, 1.0)
        o_cs.append(_dot(p_c.reshape(rows, n_cmp).astype(BF16), cv))

        imp = p_c[0]
        for g in range(1, NSA_GROUP):
            imp = imp + p_c[g]
        imp = imp[:, :n_slc] + imp[:, n_slc:]
        if n_slc < LANES:
            imp = jnp.concatenate([imp, jnp.zeros((tq, LANES - n_slc), F32)], axis=1)
        cur = (t0 + row) // SLC_BLOCK
        forced = (lane == 0) | (lane == cur) | (lane == cur - 1)
        score = jnp.where(lane <= cur, jnp.where(forced, jnp.inf, imp), neg_inf)
        sel = _top_blocks_cols(score.T, SLC_TOPN).T
        neg_mask = (sel - 1.0).astype(BF16)
        q_augs.append(jnp.concatenate([q_pad, jnp.concatenate([neg_mask] * NSA_GROUP, axis=0)], axis=1))

    def tile_logits(j):
        k0 = pl.multiple_of(j * kt, kt)
        k_aug = jnp.concatenate([ks_ref[0, pl.ds(k0, kt), :], e_ref[pl.ds(k0, kt), :]], axis=1)
        return [_dot_nt(k_aug, q_augs[h]) for h in kv_heads]

    def tile_pv(j):
        v_t = value_blocks(vst_ref, j * chunks, chunks)
        return [_dot(with_ones(v_t, h), prob_ref[h]) for h in kv_heads]

    for h, s in enumerate(tile_logits(0)):
        logit_ref[h] = s
        prob_ref[h] = jnp.zeros((kt, rows), BF16)
        acc_ref[h] = jnp.zeros((KV_WIDTH, rows), F32)

    def slc_tile(j, m_old, biased, final=False):
        pv = tile_pv(jnp.maximum(j - 1, 0))
        logits = [logit_ref[h] for h in kv_heads]
        if final:
            logits = [s + finb_ref[h] for h, s in enumerate(logits)]
        elif biased:
            logits = [s + band_bias(j * chunks, chunks, h) for h, s in enumerate(logits)]
        m_new = [jnp.maximum(m_old[h], jnp.max(logits[h], axis=0, keepdims=True)) for h in kv_heads]
        probs = [jnp.exp(logits[h] - m_new[h]).astype(BF16) for h in kv_heads]
        nxt = None if final else tile_logits(j + 1)
        for h in kv_heads:
            acc_ref[h] = jnp.exp(m_old[h] - m_new[h]) * (acc_ref[h] + pv[h])
            prob_ref[h] = probs[h]
            if not final:
                logit_ref[h] = nxt[h]
        return tuple(m_new)

    n_far = jnp.maximum(qb - (near - 1), 0) // chunks
    m_run = (jnp.full((1, rows), neg_inf, F32),) * NSA_KV_HEADS
    m_run = lax.fori_loop(0, n_far, functools.partial(slc_tile, biased=False), m_run)
    m_run = lax.fori_loop(n_far, last, functools.partial(slc_tile, biased=True), m_run)
    slc_tile(last, m_run, biased=True, final=True)
    acc_slc = [acc_ref[h] + pv for h, pv in enumerate(tile_pv(last))]

    gates_t = gates.T
    for h in range(NSA_KV_HEADS):
        sum_row = HEAD_DIM * (1 - h)
        o_c = o_cs[h]
        acc_s = acc_slc[h]
        o_s = acc_s / acc_s[sum_row:sum_row + 1, :]
        o_w = o_ws[h]

        mixed = []
        for g in range(NSA_GROUP):
            hd = NSA_GROUP * h + g
            rs = slice(g * tq, (g + 1) * tq)
            key_major = (gates_t[NSA_HEADS + hd:NSA_HEADS + hd + 1, :] * o_s[:, rs]
                         + gates_t[2 * NSA_HEADS + hd:2 * NSA_HEADS + hd + 1, :] * o_w[:, rs])
            mixed.append(gates[:, hd:hd + 1] * o_c[rs] + key_major.T)
        for pair in range(NSA_GROUP // 2):
            a, c = mixed[2 * pair], mixed[2 * pair + 1]
            if h == 0:
                both = jnp.where(lane < HEAD_DIM, a, pltpu.roll(c, HEAD_DIM, 1))
            else:
                both = jnp.where(lane < HEAD_DIM, pltpu.roll(a, HEAD_DIM, 1), c)
            col = (NSA_GROUP * h + 2 * pair) * HEAD_DIM
            y_ref[0, :, col:col + LANES] = (both * _silu(z_ref[0, :, col:col + LANES])).astype(BF16)


def _nsa_prompt(rel_bias, q, ck, cv, ks, vs_t, kw, vw_t, e_mat, gates, z):
    b, t, _ = q.shape
    tq = Q_TILE
    n_cmp = ck.shape[1]
    assert t % SLC_KEYS == 0 and SLC_KEYS % tq == 0 and WINDOW % tq == 0 and t >= WINDOW + tq
    assert t // SLC_BLOCK <= LANES
    _, _, n_band = _band_geometry(tq)
    tile = lambda width: pl.BlockSpec((1, tq, width), lambda i, j: (j, i, 0))
    seq = lambda rows: pl.BlockSpec((1, rows, LANES), lambda i, j: (j, 0, 0))
    seq_t = pl.BlockSpec((1, t // LANES, LANES, LANES), lambda i, j: (j, 0, 0, 0))
    assert tq == LANES
    return pl.pallas_call(
        _nsa_prompt_kernel,
        grid=(t // tq, b),
        in_specs=[pl.BlockSpec(memory_space=pltpu.SMEM), tile(NSA_WIDTH), seq(n_cmp), seq(n_cmp),
                  seq(t), seq_t, seq(t), seq_t, pl.BlockSpec((t, LANES), lambda i, j: (0, 0)),
                  tile(LANES), tile(NSA_WIDTH)],
        out_specs=tile(NSA_WIDTH),
        out_shape=jax.ShapeDtypeStruct((b, t, NSA_WIDTH), BF16),
        scratch_shapes=[pltpu.VMEM((NSA_HEADS, tq, n_cmp), F32),
                        pltpu.VMEM((n_band, NSA_HEADS, tq, tq), F32),
                        pltpu.VMEM((NSA_KV_HEADS, SLC_KEYS, NSA_GROUP * tq), F32),
                        pltpu.VMEM((NSA_KV_HEADS, SLC_KEYS, NSA_GROUP * tq), BF16),
                        pltpu.VMEM((NSA_KV_HEADS, KV_WIDTH, NSA_GROUP * tq), F32),
                        pltpu.VMEM((NSA_KV_HEADS, WINDOW + tq, NSA_GROUP * tq), F32),
                        pltpu.VMEM((NSA_KV_HEADS, SLC_KEYS, NSA_GROUP * tq), F32)],
        compiler_params=pltpu.CompilerParams(dimension_semantics=("arbitrary", "arbitrary"),
                                             vmem_limit_bytes=VMEM_LIMIT),
        name="nsa_prompt",
    )(rel_bias, q, ck, cv, ks, vs_t, kw, vw_t, e_mat, gates, z)


def _nsa_sample_kernel(pt_ref, rel_ref, q_ref, ck_ref, cv_ref, pk_ref, pv_ref, e_ref, kwc_ref, vwc_ref, new_ref,
                       g_ref, z_ref, y_ref, kbuf, vbuf, kaug_ref, biasc_ref, biass_ref, biasw_ref, ksem, vsem):
    page_rows = pk_ref.shape[2]
    past = kbuf.shape[2]
    n_cmp = ck_ref.shape[1]
    n_slc = n_cmp // (SLC_BLOCK // CMP_BLOCK)
    win_rows = kwc_ref.shape[2]
    heads = tuple(range(NSA_HEADS))

    @pl.when(pl.program_id(0) == 0)
    def _():
        ccol = lax.broadcasted_iota(jnp.int32, (1, n_cmp), 1)
        cblock = (SLC_BLOCK // CMP_BLOCK) * (ccol % n_slc) + ccol // n_slc
        cdist = past - (CMP_BLOCK * cblock + CMP_BLOCK - 1)
        biasc_ref[...] = jnp.concatenate(_rel_bias_rows(cdist, rel_ref, heads), axis=0)
        sdist = past - lax.broadcasted_iota(jnp.int32, (1, past), 1)
        biass_ref[...] = jnp.concatenate(_rel_bias_rows(sdist, rel_ref, heads), axis=0)
        wdist = win_rows - lax.broadcasted_iota(jnp.int32, (1, win_rows), 1)
        biasw_ref[...] = jnp.concatenate(_rel_bias_rows(wdist, rel_ref, heads), axis=0)
        kaug_ref[KV_WIDTH:, :] = e_ref[...]

    slot = _double_buffered_gather(pt_ref, (pk_ref, pv_ref), (kbuf, vbuf), (ksem, vsem), past // page_rows)

    q8 = q_ref[0]
    q8f = q8.astype(F32)
    bias0 = jnp.concatenate(
        [jnp.full((1, 1), rel_ref[0, hd] - rel_ref[REL_BUCKETS - 1, hd], F32) for hd in heads], axis=0)
    lane = lax.broadcasted_iota(jnp.int32, (NSA_KV_HEADS, LANES), 1)

    s_c = _dot_nt(q8, ck_ref[0]) + biasc_ref[...]
    e_c = jnp.exp(s_c - jnp.max(s_c, axis=1, keepdims=True))
    p_c = e_c / jnp.sum(e_c, axis=1, keepdims=True)
    o_c = _dot(p_c.astype(BF16), cv_ref[0])

    imps = []
    for h in range(NSA_KV_HEADS):
        acc = p_c[NSA_GROUP * h:NSA_GROUP * h + 1]
        for g in range(1, NSA_GROUP):
            acc = acc + p_c[NSA_GROUP * h + g:NSA_GROUP * h + g + 1]
        imps.append(acc)
    imp = jnp.concatenate(imps, axis=0)
    imp = imp[:, :n_slc] + imp[:, n_slc:]
    if n_slc < LANES:
        imp = jnp.concatenate([imp, jnp.zeros((NSA_KV_HEADS, LANES - n_slc), F32)], axis=1)
    forced = (lane == 0) | (lane == n_slc - 1)
    score = jnp.where(lane < n_slc, jnp.where(forced, jnp.inf, imp), -jnp.inf)
    padded = jnp.concatenate([score, jnp.full((LANES - NSA_KV_HEADS, LANES), -jnp.inf, F32)], axis=0)
    sel = _top_blocks_cols(padded.T, SLC_TOPN - 1).T[:NSA_KV_HEADS]
    neg_mask = (sel - 1.0).astype(BF16)
    neg_rows = jnp.concatenate([neg_mask[h:h + 1] for h in range(NSA_KV_HEADS) for _ in range(NSA_GROUP)], axis=0)
    q_aug = jnp.concatenate([q8, neg_rows], axis=1)

    def with_new_token(s, s_new, v_t, v_new):
        m = jnp.maximum(jnp.max(s, axis=1, keepdims=True), s_new)
        p = jnp.exp(s - m)
        p_new = jnp.exp(s_new - m)
        total = jnp.sum(p, axis=1, keepdims=True) + p_new
        return (_dot_nt(p.astype(BF16), v_t) + p_new.astype(BF16).astype(F32) * v_new.astype(F32)) / total

    kaug_ref[:KV_WIDTH, :] = kbuf[slot].astype(BF16)
    s_s = _dot(q_aug, kaug_ref[...]) + biass_ref[...]
    s_new = jnp.sum(q8f * new_ref[0, 0:1, :].astype(F32), axis=1, keepdims=True) + bias0
    o_s = with_new_token(s_s, s_new, vbuf[slot].astype(BF16), new_ref[0, 1:2, :])

    wcol = lax.broadcasted_iota(jnp.int32, (1, win_rows), 1)
    s_w = _dot(q8, kwc_ref[0].astype(BF16)) + biasw_ref[...]
    s_w = jnp.where(win_rows - wcol < WINDOW, s_w, MASK_VALUE)
    s_new = jnp.sum(q8f * new_ref[0, 2:3, :].astype(F32), axis=1, keepdims=True) + bias0
    o_w = with_new_token(s_w, s_new, vwc_ref[0].astype(BF16), new_ref[0, 3:4, :])

    gates = g_ref[0]
    low = []
    for hd in heads:
        r = (gates[:, hd:hd + 1] * o_c[hd:hd + 1]
             + gates[:, NSA_HEADS + hd:NSA_HEADS + hd + 1] * o_s[hd:hd + 1]
             + gates[:, 2 * NSA_HEADS + hd:2 * NSA_HEADS + hd + 1] * o_w[hd:hd + 1])
        low.append(r if hd < NSA_GROUP else pltpu.roll(r, HEAD_DIM, 1))
    lane1 = lax.broadcasted_iota(jnp.int32, (1, LANES), 1)
    for pair in range(NSA_HEADS // 2):
        both = jnp.where(lane1 < HEAD_DIM, low[2 * pair], pltpu.roll(low[2 * pair + 1], HEAD_DIM, 1))
        cols = slice(LANES * pair, LANES * (pair + 1))
        y_ref[0, :, cols] = (both * _silu(z_ref[0, :, cols])).astype(BF16)


def _nsa_sample(page_table, rel_bias, q8, ck, cv, pool_k, pool_v, e_mat, win_k, win_v, new_rows, gates, z):
    n_seq, n_pages = page_table.shape
    page_rows = pool_k.shape[2]
    past = n_pages * page_rows
    n_cmp = ck.shape[1]
    win_rows = win_k.shape[2]
    assert past % SLC_BLOCK == 0 and past // SLC_BLOCK <= LANES and page_rows % LANES == 0
    hbm = pl.BlockSpec(memory_space=pl.ANY)
    per_seq = lambda r, w: pl.BlockSpec((1, r, w), lambda i, pt: (i, 0, 0))
    return pl.pallas_call(
        _nsa_sample_kernel,
        grid_spec=pltpu.PrefetchScalarGridSpec(
            num_scalar_prefetch=1,
            grid=(n_seq,),
            in_specs=[pl.BlockSpec(memory_space=pltpu.SMEM), per_seq(NSA_HEADS, LANES), per_seq(n_cmp, LANES),
                      per_seq(n_cmp, LANES), hbm, hbm,
                      pl.BlockSpec((LANES, past), lambda i, pt: (0, 0)),
                      per_seq(KV_WIDTH, win_rows), per_seq(KV_WIDTH, win_rows), per_seq(4, LANES),
                      per_seq(1, LANES), per_seq(1, NSA_WIDTH)],
            out_specs=per_seq(1, NSA_WIDTH),
            scratch_shapes=[pltpu.VMEM((2, KV_WIDTH, past), F32), pltpu.VMEM((2, KV_WIDTH, past), F32),
                            pltpu.VMEM((KV_WIDTH + LANES, past), BF16),
                            pltpu.VMEM((NSA_HEADS, n_cmp), F32), pltpu.VMEM((NSA_HEADS, past), F32),
                            pltpu.VMEM((NSA_HEADS, win_rows), F32),
                            pltpu.SemaphoreType.DMA((2,)), pltpu.SemaphoreType.DMA((2,))]),
        out_shape=jax.ShapeDtypeStruct((n_seq, 1, NSA_WIDTH), BF16),
        compiler_params=pltpu.CompilerParams(dimension_semantics=("arbitrary",), vmem_limit_bytes=VMEM_LIMIT),
        name="nsa_sample",
    )(page_table.reshape(-1), rel_bias, q8, ck, cv, pool_k, pool_v, e_mat, win_k, win_v, new_rows, gates, z)


def _group_norm_gate(o, z, gn, seg):
    mu = _segment_sum(o, seg) * (1.0 / RET_VDIM)
    d = o - mu
    var = _segment_sum(d * d, seg) * (1.0 / RET_VDIM)
    return (d * lax.rsqrt(var + EPS) * gn * _silu(z)).astype(BF16)


def _retention_prompt_kernel(q_ref, k_ref, v_ref, z_ref, gn_ref, seg_ref, dec_ref, qdec_ref, kdec_ref, cdec_ref,
                             y_ref, st_ref, state):
    chunk = pl.program_id(1)
    tc = q_ref.shape[1]
    lane = lax.broadcasted_iota(jnp.int32, (tc, LANES), 1)
    srow = lax.broadcasted_iota(jnp.int32, (LANES, LANES), 0)
    scol = lax.broadcasted_iota(jnp.int32, (LANES, LANES), 1)
    same_head = (srow // RET_KDIM) == (scol // RET_VDIM)

    @pl.when(chunk == 0)
    def _():
        state[...] = jnp.zeros_like(state)

    for pair in range(RET_HEADS // 2):
        cols = slice(LANES * pair, LANES * (pair + 1))
        q = q_ref[0, :, cols]
        k = k_ref[0, :, cols]
        v = v_ref[0, :, cols]
        s_old = state[pair]
        cross = _dot(q, s_old.astype(BF16)) * qdec_ref[:, cols]
        halves = []
        for e in range(2):
            qe = jnp.where((lane // RET_KDIM) == e, q, jnp.zeros((), BF16))
            scores = _dot_nt(qe, k) * dec_ref[2 * pair + e]
            halves.append(_dot(scores.astype(BF16), v))
        o = jnp.where(lane < RET_VDIM, halves[0], halves[1]) + cross
        y_ref[0, :, cols] = _group_norm_gate(o, z_ref[0, :, cols], gn_ref[:, cols], seg_ref[...])
        kd_t = (k.astype(F32) * kdec_ref[:, cols]).T.astype(BF16)
        s_new = s_old * cdec_ref[:, cols] + jnp.where(same_head, _dot(kd_t, v), 0.0)
        state[pair] = s_new

    @pl.when(chunk == pl.num_programs(1) - 1)
    def _():
        for head in range(RET_HEADS):
            e = head % 2
            st_ref[0, head] = state[head // 2, RET_KDIM * e:RET_KDIM * (e + 1), RET_VDIM * e:RET_VDIM * (e + 1)]


def _retention_tables(tc):
    log_g = jnp.asarray(RET_GAMMA_LOG, F32)
    i = jnp.arange(tc, dtype=F32)
    diff = i[:, None] - i[None, :]
    decay = jnp.where(diff >= 0, jnp.exp(jnp.maximum(diff, 0.0)[None] * log_g[:, None, None]), 0.0)
    widen = lambda a: jnp.repeat(a, RET_VDIM, axis=-1)
    q_decay = widen(jnp.exp((i[:, None] + 1.0) * log_g[None, :]))
    k_decay = widen(jnp.exp((tc - 1.0 - i)[:, None] * log_g[None, :]))
    c_decay = widen(jnp.exp(tc * log_g)[None, :])
    return decay, q_decay, k_decay, c_decay


def _retention_prompt(q, k, v, z, gn, seg):
    b, t, _ = q.shape
    tc = RET_TILE
    assert t % tc == 0
    decay, q_decay, k_decay, c_decay = _retention_tables(tc)
    tile = pl.BlockSpec((1, tc, RET_WIDTH), lambda i, j: (i, j, 0))
    const = lambda a: pl.BlockSpec(a.shape, lambda i, j: (0,) * a.ndim)
    return pl.pallas_call(
        _retention_prompt_kernel,
        grid=(b, t // tc),
        in_specs=[tile, tile, tile, tile, const(gn), const(seg), const(decay), const(q_decay), const(k_decay),
                  const(c_decay)],
        out_specs=[tile, pl.BlockSpec((1, RET_HEADS, RET_KDIM, RET_VDIM), lambda i, j: (i, 0, 0, 0))],
        out_shape=[jax.ShapeDtypeStruct((b, t, RET_WIDTH), BF16),
                   jax.ShapeDtypeStruct((b, RET_HEADS, RET_KDIM, RET_VDIM), F32)],
        scratch_shapes=[pltpu.VMEM((RET_HEADS // 2, LANES, LANES), F32)],
        compiler_params=pltpu.CompilerParams(dimension_semantics=("arbitrary", "arbitrary"),
                                             vmem_limit_bytes=VMEM_LIMIT),
        name="retention_prompt",
    )(q, k, v, z, gn, seg, decay, q_decay, k_decay, c_decay)


def _retention_sample_kernel(qt_ref, kt_ref, v_ref, z_ref, gn_ref, gam_ref, st_ref, y_ref, so_ref):
    n = v_ref.shape[1]
    qt = qt_ref[0]
    kt = kt_ref[0]
    gam = gam_ref[...]
    qk = jnp.sum(qt * kt, axis=1)
    for s in range(n):
        st = st_ref[s]
        vs = v_ref[0, s]
        cross = jnp.sum(qt[:, :, s:s + 1] * st, axis=1)
        so_ref[s] = st * gam[:, :, None] + kt[:, :, s:s + 1] * vs[:, None, :]
        o = qk[:, s:s + 1] * vs + gam * cross
        mu = jnp.mean(o, axis=-1, keepdims=True)
        d = o - mu
        var = jnp.mean(d * d, axis=-1, keepdims=True)
        y_ref[0, s] = (d * lax.rsqrt(var + EPS) * gn_ref[...] * _silu(z_ref[0, s])).astype(BF16)


def _retention_sample(q, k, v, z, gn, state):
    n_seq = q.shape[0]
    n = min(RET_SAMPLE_SEQS, n_seq)
    assert n_seq % n == 0
    steps = n_seq // n
    to_cols = lambda a: a.astype(F32).reshape(steps, n, RET_HEADS, RET_KDIM).transpose(0, 2, 3, 1)
    to_rows = lambda a: a.astype(F32).reshape(steps, n, RET_HEADS, RET_VDIM)
    gam = jnp.exp(jnp.asarray(RET_GAMMA_LOG, F32)).reshape(RET_HEADS, 1)
    cols = pl.BlockSpec((1, RET_HEADS, RET_KDIM, n), lambda i: (i, 0, 0, 0))
    rws = pl.BlockSpec((1, n, RET_HEADS, RET_VDIM), lambda i: (i, 0, 0, 0))
    st = pl.BlockSpec((n, RET_HEADS, RET_KDIM, RET_VDIM), lambda i: (i, 0, 0, 0))
    y, new_state = pl.pallas_call(
        _retention_sample_kernel,
        grid=(steps,),
        in_specs=[cols, cols, rws, rws, pl.BlockSpec((RET_HEADS, RET_VDIM), lambda i: (0, 0)),
                  pl.BlockSpec((RET_HEADS, 1), lambda i: (0, 0)), st],
        out_specs=[rws, st],
        out_shape=[jax.ShapeDtypeStruct((steps, n, RET_HEADS, RET_VDIM), BF16),
                   jax.ShapeDtypeStruct(state.shape, F32)],
        compiler_params=pltpu.CompilerParams(dimension_semantics=("parallel",), vmem_limit_bytes=VMEM_LIMIT),
        name="retention_sample",
    )(to_cols(q), to_cols(k), to_rows(v), to_rows(z), gn.reshape(RET_HEADS, RET_VDIM), gam, state)
    return y.reshape(n_seq, RET_WIDTH), new_state


def _out_kernel(x_ref, yn_ref, yr_ref, w_ref, o_ref):
    y = jnp.concatenate([yn_ref[0], yr_ref[0]], axis=1)
    o_ref[0] = x_ref[0] + _dot(y, w_ref[...])


def _out_project(x, y_nsa, y_ret, w):
    nb, rows, d = x.shape
    tm = min(PROJ_ROWS, rows)
    spec = lambda width: pl.BlockSpec((1, tm, width), lambda i, j: (i, j, 0))
    return pl.pallas_call(
        _out_kernel,
        grid=(nb, rows // tm),
        in_specs=[spec(d), spec(NSA_WIDTH), spec(RET_WIDTH), pl.BlockSpec(w.shape, lambda i, j: (0, 0))],
        out_specs=spec(d),
        out_shape=jax.ShapeDtypeStruct(x.shape, F32),
        compiler_params=pltpu.CompilerParams(dimension_semantics=("parallel", "parallel"),
                                             vmem_limit_bytes=VMEM_LIMIT),
        name="output_projection",
    )(x, y_nsa, y_ret, w)


def _rotary_tables(pos):
    half = RET_KDIM // 2
    inv = ROPE_BASE ** (-jnp.arange(half, dtype=F32) / half)
    ang = pos.astype(F32)[:, None] * inv[None, :]
    cos, sin = jnp.cos(ang), jnp.sin(ang)
    reps = LANES // RET_KDIM
    return jnp.tile(jnp.concatenate([cos, cos], axis=1), (1, reps)), jnp.tile(jnp.concatenate([-sin, sin], axis=1), (1, reps))


def _arrange_w_in(w_in):
    d = w_in.shape[0]
    parts = jnp.split(w_in, np.cumsum(SPLIT_WIDTHS)[:-1].tolist(), axis=1)
    order = [hd for g in range(NSA_GROUP) for hd in (g, NSA_GROUP + g)]
    q = parts[0].reshape(d, NSA_HEADS, HEAD_DIM)[:, order].reshape(d, NSA_WIDTH)
    gates = parts[7].reshape(d, NSA_HEADS, 3).transpose(0, 2, 1).reshape(d, 3 * NSA_HEADS)
    gates = jnp.pad(gates, ((0, 0), (0, LANES - 3 * NSA_HEADS)))
    w = jnp.concatenate([q] + parts[1:7] + parts[8:] + [gates], axis=1)
    assert w.shape[1] == COL_END
    return w.astype(BF16)


def _compress_weights(pe, w1, w2):
    w1r = w1.reshape(CMP_BLOCK, HEAD_DIM, HEAD_DIM)
    w1_both = jnp.zeros((CMP_BLOCK, NSA_KV_HEADS, HEAD_DIM, NSA_KV_HEADS, HEAD_DIM), F32)
    w2_both = jnp.zeros((NSA_KV_HEADS, HEAD_DIM, NSA_KV_HEADS, HEAD_DIM), F32)
    for h in range(NSA_KV_HEADS):
        w1_both = w1_both.at[:, h, :, h, :].set(w1r)
        w2_both = w2_both.at[h, :, h, :].set(w2)
    return (jnp.tile(pe, (1, NSA_KV_HEADS)), w1_both.reshape(CMP_BLOCK * KV_WIDTH, KV_WIDTH).astype(BF16),
            w2_both.reshape(KV_WIDTH, KV_WIDTH).astype(BF16))


def _even_odd(c):
    n, nc, w = c.shape
    ratio = SLC_BLOCK // CMP_BLOCK
    return c.reshape(n, nc // ratio, ratio, w).transpose(0, 2, 1, 3).reshape(n, nc, w)


def _block_membership(n_keys):
    k = jnp.arange(n_keys)[:, None] // SLC_BLOCK
    return jnp.where(k == jnp.arange(LANES)[None, :], -MASK_VALUE, 0.0).astype(BF16)


def kernel(x_prompt, x_sample, cache_cmp_k, cache_cmp_v, cache_slc_k, cache_slc_v, cache_win_k, cache_win_v,
           state_ret, page_table, norm_g, w_in, q_norm_g, k_norm_g, cmp_pe_k, cmp_w1_k, cmp_w2_k, cmp_pe_v,
           cmp_w1_v, cmp_w2_v, rel_bias, ret_gn_g, w_out):
    b, t, d = x_prompt.shape
    n_seq, dec_len, _ = x_sample.shape
    n_pages = page_table.shape[1]
    page_rows = cache_cmp_k.shape[1]
    past = n_pages * page_rows
    assert dec_len == 1 and past % CMP_BLOCK == 0
    kv4 = lambda a: a.reshape(a.shape[0], a.shape[1], NSA_KV_HEADS, HEAD_DIM)

    w = _arrange_w_in(w_in)
    w_o = w_out.astype(BF16)
    ng = norm_g.reshape(1, d)
    qg = (jnp.tile(q_norm_g, LANES // HEAD_DIM) * ATTN_SCALE).reshape(1, LANES)
    kg = jnp.tile(k_norm_g, LANES // HEAD_DIM).reshape(1, LANES)
    gn = ret_gn_g.reshape(1, RET_WIDTH)
    lane = np.arange(LANES)
    seg = jnp.asarray(lane[:, None] // HEAD_DIM == lane[None, :] // HEAD_DIM, BF16)
    cw = _compress_weights(cmp_pe_k, cmp_w1_k, cmp_w2_k) + _compress_weights(cmp_pe_v, cmp_w1_v, cmp_w2_v)

    cos, sin = _rotary_tables(jnp.arange(t))
    (q, kc, vc, ks, vs, kw, vw, ksb, _, kwb, _, gates, zn, qr, kr, vr, zr, vs_t, vw_t) = _project(
        x_prompt, cos, sin, ng, w, qg, kg, seg)
    ck, cv = _compress_prompt(kc, vc, cw, kg, seg)
    y_nsa = _nsa_prompt(rel_bias, q, _even_odd(ck), _even_odd(cv), ksb, vs_t, kwb, vw_t, _block_membership(t), gates, zn)
    y_ret, p_ret = _retention_prompt(qr, kr, vr, zr, gn, seg)
    y_prompt = _out_project(x_prompt, y_nsa, y_ret, w_o)
    keep = min(WINDOW, t)
    prompt_out = (y_prompt, kv4(kc), kv4(vc), kv4(ks), kv4(vs), kv4(kw[:, t - keep:]), kv4(vw[:, t - keep:]), p_ret)

    cos, sin = _rotary_tables(jnp.full((n_seq,), past))
    xs = x_sample.reshape(1, n_seq, d)
    (q, kc, vc, ks, vs, kw, vw, ksb, vsb, kwb, vwb, gates, zn, qr, kr, vr, zr) = [
        a[0] for a in _project(xs, cos, sin, ng, w, qg, kg, seg)[:17]]
    chan = lambda a: jnp.transpose(a, (0, 2, 3, 1)).reshape(a.shape[0], KV_WIDTH, a.shape[1])
    ck, cv = _compress_sample(page_table, chan(cache_cmp_k), chan(cache_cmp_v), cw, kg, seg)
    half = jnp.asarray(lane[None, :] // HEAD_DIM == (np.arange(NSA_HEADS) // NSA_GROUP)[:, None])
    q8 = jnp.where(half[None], jnp.tile(q.reshape(n_seq, NSA_GROUP, LANES), (1, NSA_KV_HEADS, 1)), jnp.zeros((), BF16))
    new_rows = jnp.stack([ksb, vsb, kwb, vwb], axis=1)
    y_nsa = _nsa_sample(page_table, rel_bias, q8, _even_odd(ck), _even_odd(cv), chan(cache_slc_k), chan(cache_slc_v),
                        _block_membership(past).T, chan(cache_win_k), chan(cache_win_v), new_rows,
                        gates.reshape(n_seq, 1, LANES), zn.reshape(n_seq, 1, NSA_WIDTH))
    y_ret, s_ret = _retention_sample(qr, kr, vr, zr, gn, state_ret)
    y_sample = _out_project(xs, y_nsa.reshape(1, n_seq, NSA_WIDTH), y_ret.reshape(1, n_seq, RET_WIDTH), w_o)
    keep = min(WINDOW, cache_win_k.shape[1] + 1)
    new4 = lambda a: a.reshape(n_seq, 1, NSA_KV_HEADS, HEAD_DIM)
    s_win_k = jnp.concatenate([cache_win_k, new4(kw)], axis=1)[:, -keep:]
    s_win_v = jnp.concatenate([cache_win_v, new4(vw)], axis=1)[:, -keep:]
    sample_out = (y_sample.reshape(n_seq, 1, d), new4(kc), new4(vc), new4(ks), new4(vs), s_win_k, s_win_v, s_ret)

    return (prompt_out[0], sample_out[0]) + prompt_out[1:] + sample_out[1:]
```
